```python
import math
import jax, jax.numpy as jnp
from jax import lax
import numpy as np

D_MODEL = 1024
BATCH = 8
SEQ = 4096
DEPTH = 1
DEC_BATCH = 16
DEC_SEQ = 2048
PAST_LEN = 128

MIX_W = D_MODEL
ATT_W = MIX_W // 2
ATT_HEADS = 4
ATT_V_DIM = ATT_W // ATT_HEADS
ATT_QK_DIM = ATT_V_DIM // 2
ROT_DIM = ATT_QK_DIM // 4
ROPE_THETA = 500000.0
Q_BLOCK = 128
M_W = MIX_W - ATT_W
M_HEADS = 4
M_HEAD_DIM = M_W // M_HEADS
M_CHUNK = 128
CONV_K = 5
N_GATES = 4 * M_HEADS
SPLITS = (ATT_W, ATT_W, ATT_W, M_W, M_W, M_W, M_W, N_GATES)
PROJ_W = sum(SPLITS)
SPLIT_AT = tuple(sum(SPLITS[:j + 1]) for j in range(len(SPLITS) - 1))
PEER_HEADS = 8
PEER_QDIM = 256
PEER_HALF = PEER_QDIM // 2
N_KEYS = 128
N_EXPERTS = N_KEYS * N_KEYS
PEER_TOPK = 16
PEER_BLOCK = 128
PLE_DIM = 256
EPS = 1e-6

kernel_name = "hymba_diffattn_mlstm_peer_encoder"


def rms_norm(x, w):
    x32 = x.astype(jnp.float32)
    y = x32 * lax.rsqrt(jnp.mean(x32 * x32, axis=-1, keepdims=True) + EPS)
    return (y * w.astype(jnp.float32)).astype(x.dtype)


def rope_tables(seq):
    inv = ROPE_THETA ** (-jnp.arange(0, ROT_DIM, 2, dtype=jnp.float32) / ROT_DIM)
    ang = jnp.arange(seq, dtype=jnp.float32)[:, None] * inv[None, :]
    return jnp.cos(ang), jnp.sin(ang)


def apply_partial_rope(x, cos, sin):
    half = ROT_DIM // 2
    xf = x.astype(jnp.float32)
    c = cos[:, None, None, :]
    s = sin[:, None, None, :]
    x1 = xf[..., :half]
    x2 = xf[..., half:ROT_DIM]
    out = jnp.concatenate([x1 * c - x2 * s, x2 * c + x1 * s, xf[..., ROT_DIM:]], axis=-1)
    return out.astype(x.dtype)


def diff_attention(q, k, v, lam):
    b, s = q.shape[:2]
    nb = s // Q_BLOCK
    qb = jnp.moveaxis(q.reshape(b, nb, Q_BLOCK, ATT_HEADS, 2, ATT_QK_DIM), 1, 0)

    def block(qi):
        sc = jnp.einsum('bqhcd,bkhcd->bhcqk', qi, k).astype(jnp.float32)
        p = jax.nn.softmax(sc, axis=-1)
        a = p[:, :, 0] - lam * p[:, :, 1]
        return jnp.einsum('bhqk,bkhe->bqhe', a.astype(v.dtype), v)

    out = lax.map(block, qb)
    return jnp.moveaxis(out, 0, 1).reshape(b, s, ATT_HEADS, ATT_V_DIM)


def centered_conv(x, w):
    c = x.shape[-1]
    return lax.conv_general_dilated(
        x, w[:, None, :].astype(x.dtype), window_strides=(1,),
        padding=[(CONV_K // 2, CONV_K // 2)],
        dimension_numbers=('NWC', 'WIO', 'NWC'), feature_group_count=c)


def mlstm_scan(q, k, v, ig, fg):
    b, h, s, d = q.shape
    nc = s // M_CHUNK

    def chunks(t):
        return jnp.moveaxis(t.reshape(b, h, nc, M_CHUNK, *t.shape[3:]), 2, 0)

    logf = jax.nn.log_sigmoid(fg)
    lower = jnp.tril(jnp.ones((M_CHUNK, M_CHUNK), dtype=bool))

    def step(carry, xs):
        C, n, m = carry
        qc, kc, vc, ic, lfc = xs
        bcum = jnp.cumsum(lfc, axis=-1)
        dlog = jnp.where(lower, bcum[..., :, None] - bcum[..., None, :] + ic[..., None, :], -jnp.inf)
        inter = bcum + m[..., None]
        m_t = jnp.maximum(inter, jnp.max(dlog, axis=-1))
        w = jnp.exp(dlog - m_t[..., None])
        sc = jnp.exp(inter - m_t)
        wqk = w * jnp.einsum('bhtd,bhsd->bhts', qc, kc)
        num = sc[..., None] * jnp.einsum('bhtd,bhde->bhte', qc, C) + jnp.einsum('bhts,bhse->bhte', wqk, vc)
        den = sc * jnp.einsum('bhtd,bhd->bht', qc, n) + jnp.sum(wqk, axis=-1)
        h_out = num / jnp.maximum(jnp.abs(den), jnp.exp(-m_t))[..., None]
        b_last = bcum[..., -1]
        rel = b_last[..., None] - bcum + ic
        m_new = jnp.maximum(b_last + m, jnp.max(rel, axis=-1))
        dec = jnp.exp(rel - m_new[..., None])
        keep = jnp.exp(b_last + m - m_new)
        C_new = keep[..., None, None] * C + jnp.einsum('bhs,bhsd,bhse->bhde', dec, kc, vc)
        n_new = keep[..., None] * n + jnp.einsum('bhs,bhsd->bhd', dec, kc)
        return (C_new, n_new, m_new), h_out

    init = (jnp.zeros((b, h, d, d), jnp.float32), jnp.zeros((b, h, d), jnp.float32),
            jnp.zeros((b, h), jnp.float32))
    _, hs = lax.scan(step, init, (chunks(q), chunks(k), chunks(v), chunks(ig), chunks(logf)))
    return jnp.moveaxis(hs, 0, 2).reshape(b, h, s, d)


def mlstm_mixer(mq, mk, mv, mo, gates, conv_w, gain):
    b, s, _ = mq.shape
    qk = jax.nn.silu(centered_conv(jnp.concatenate([mq, mk], axis=-1), conv_w))
    mq, mk = jnp.split(qk, 2, axis=-1)

    def heads(t):
        return t.reshape(b, s, M_HEADS, M_HEAD_DIM).transpose(0, 2, 1, 3).astype(jnp.float32)

    q = heads(mq)
    k = heads(mk) * (M_HEAD_DIM ** -0.5)
    v = heads(mv)
    g = gates.astype(jnp.float32).transpose(0, 2, 1).reshape(b, 4, M_HEADS, s)
    i_f, f_f, i_b, f_b = g[:, 0], g[:, 1], g[:, 2], g[:, 3]
    h_fwd = mlstm_scan(q, k, v, i_f, f_f)
    flip = lambda t: jnp.flip(t, axis=2)
    h_bwd = flip(mlstm_scan(flip(q), flip(k), flip(v), flip(i_b), flip(f_b)))
    hsum = (h_fwd + h_bwd).transpose(0, 2, 1, 3)
    o = jax.nn.sigmoid(mo.astype(jnp.float32)).reshape(b, s, M_HEADS, M_HEAD_DIM)
    hn = rms_norm(o * hsum, gain.reshape(M_HEADS, M_HEAD_DIM))
    return hn.reshape(b, s, M_W).astype(mq.dtype)


def peer(xn, wq, subkeys, u, v):
    b, s, d = xn.shape
    t = b * s
    xf = xn.reshape(t, d)
    q = (xf @ wq).reshape(t, PEER_HEADS, 2, PEER_HALF)
    s1 = jnp.einsum('thd,hkd->thk', q[:, :, 0], subkeys[:, 0]).astype(jnp.float32)
    s2 = jnp.einsum('thd,hkd->thk', q[:, :, 1], subkeys[:, 1]).astype(jnp.float32)
    v1, i1 = lax.top_k(s1, PEER_TOPK)
    v2, i2 = lax.top_k(s2, PEER_TOPK)
    cand = (v1[..., :, None] + v2[..., None, :]).reshape(t, PEER_HEADS, PEER_TOPK * PEER_TOPK)
    sc, ci = lax.top_k(cand, PEER_TOPK)
    e1 = jnp.take_along_axis(i1, ci // PEER_TOPK, axis=-1)
    e2 = jnp.take_along_axis(i2, ci % PEER_TOPK, axis=-1)
    nblk = t // PEER_BLOCK
    idx = (e1 * N_KEYS + e2).reshape(nblk, PEER_BLOCK, PEER_HEADS * PEER_TOPK)
    g = jax.nn.softmax(sc, axis=-1).reshape(nblk, PEER_BLOCK, PEER_HEADS * PEER_TOPK)

    def block(args):
        xb, ib, gb = args
        a = jnp.einsum('td,tkd->tk', xb, u[ib]).astype(jnp.float32)
        act = (jax.nn.gelu(a, approximate=False) * gb).astype(xb.dtype)
        return jnp.einsum('tk,tkd->td', act, v[ib])

    out = lax.map(block, (xf.reshape(nblk, PEER_BLOCK, d), idx, g))
    return out.reshape(b, s, d)


def encoder_trunk(x, p, norm_mix, w_in, gate_bias, conv_qk, lambda_qk, attn_gain, mlstm_gain,
                  w_out, norm_ffn, peer_wq, peer_subkeys, peer_u, peer_v, norm_ple,
                  ple_w_gate, ple_w_proj, norm_final):
    b, s, _ = x.shape
    cos, sin = rope_tables(s)
    for li in range(DEPTH):
        lam_init = 0.8 - 0.6 * math.exp(-0.3 * li)
        lq = lambda_qk[li].astype(jnp.float32)
        lam = jnp.exp(jnp.sum(lq[0] * lq[1])) - jnp.exp(jnp.sum(lq[2] * lq[3])) + lam_init
        h = rms_norm(x, norm_mix[li])
        proj = h @ w_in[li]
        aq, ak, av, mq, mk, mv, mo, gt = jnp.split(proj, SPLIT_AT, axis=-1)
        aq = apply_partial_rope(aq.reshape(b, s, ATT_HEADS, 2, ATT_QK_DIM), cos, sin) * (ATT_QK_DIM ** -0.5)
        ak = apply_partial_rope(ak.reshape(b, s, ATT_HEADS, 2, ATT_QK_DIM), cos, sin)
        av = av.reshape(b, s, ATT_HEADS, ATT_V_DIM)
        att = diff_attention(aq, ak, av, lam)
        att = (rms_norm(att, attn_gain[li]) * (1.0 - lam_init)).reshape(b, s, ATT_W)
        mem = mlstm_mixer(mq, mk, mv, mo, gt + gate_bias[li], conv_qk[li], mlstm_gain[li])
        x = x + jnp.concatenate([att.astype(x.dtype), mem.astype(x.dtype)], axis=-1) @ w_out[li]
        x = x + peer(rms_norm(x, norm_ffn[li]), peer_wq[li], peer_subkeys[li], peer_u[li], peer_v[li])
        gate = jax.nn.sigmoid((rms_norm(x, norm_ple[li]) @ ple_w_gate[li]).astype(jnp.float32))
        x = x + ((p[li] @ ple_w_proj[li]).astype(jnp.float32) * gate).astype(x.dtype)
    return rms_norm(x, norm_final)


def setup_inputs(seed: int = 0) -> dict:
    key = jax.random.key(seed)
    ks = jax.random.split(key, 24)
    f32 = jnp.float32
    nrm = lambda k, shape, scale: jax.random.normal(k, shape, f32) * scale
    gain = lambda k, shape: 1.0 + 0.02 * jax.random.normal(k, shape, f32)
    f_bias = jnp.linspace(3.0, 6.0, M_HEADS, dtype=f32)
    zb = jnp.zeros((M_HEADS,), f32)
    gate_off = jnp.stack([zb, f_bias, zb, f_bias])
    gate_bias = (0.1 * jax.random.normal(ks[6], (DEPTH, 4, M_HEADS), f32) + gate_off).reshape(DEPTH, N_GATES)
    return {
        "x_prompt": nrm(ks[0], (BATCH, SEQ, D_MODEL), 1.0),
        "x_sample": nrm(ks[1], (DEC_BATCH, DEC_SEQ, D_MODEL), 1.0),
        "p_prompt": nrm(ks[2], (DEPTH, BATCH, SEQ, PLE_DIM), 1.0),
        "p_sample": nrm(ks[3], (DEPTH, DEC_BATCH, DEC_SEQ, PLE_DIM), 1.0),
        "norm_mix": gain(ks[4], (DEPTH, D_MODEL)),
        "w_in": nrm(ks[5], (DEPTH, D_MODEL, PROJ_W), D_MODEL ** -0.5),
        "gate_bias": gate_bias,
        "conv_qk": nrm(ks[7], (DEPTH, CONV_K, 2 * M_W), CONV_K ** -0.5),
        "lambda_qk": nrm(ks[8], (DEPTH, 4, ATT_QK_DIM), 0.1),
        "attn_gain": gain(ks[9], (DEPTH, ATT_V_DIM)),
        "mlstm_gain": gain(ks[10], (DEPTH, M_W)),
        "w_out": nrm(ks[11], (DEPTH, MIX_W, D_MODEL), MIX_W ** -0.5),
        "norm_ffn": gain(ks[12], (DEPTH, D_MODEL)),
        "peer_wq": nrm(ks[13], (DEPTH, D_MODEL, PEER_HEADS * PEER_QDIM), D_MODEL ** -0.5),
        "peer_subkeys": nrm(ks[14], (DEPTH, PEER_HEADS, 2, N_KEYS, PEER_HALF), PEER_HALF ** -0.5),
        "peer_u": nrm(ks[15], (DEPTH, N_EXPERTS, D_MODEL), D_MODEL ** -0.5),
        "peer_v": nrm(ks[16], (DEPTH, N_EXPERTS, D_MODEL), PEER_HEADS ** -0.5),
        "norm_ple": gain(ks[17], (DEPTH, D_MODEL)),
        "ple_w_gate": nrm(ks[18], (DEPTH, D_MODEL, D_MODEL), D_MODEL ** -0.5),
        "ple_w_proj": nrm(ks[19], (DEPTH, PLE_DIM, D_MODEL), PLE_DIM ** -0.5),
        "norm_final": gain(ks[20], (D_MODEL,)),
    }


def reference(x_prompt, x_sample, p_prompt, p_sample, norm_mix, w_in, gate_bias, conv_qk,
              lambda_qk, attn_gain, mlstm_gain, w_out, norm_ffn, peer_wq, peer_subkeys,
              peer_u, peer_v, norm_ple, ple_w_gate, ple_w_proj, norm_final):
    weights = (norm_mix, w_in, gate_bias, conv_qk, lambda_qk, attn_gain, mlstm_gain, w_out,
               norm_ffn, peer_wq, peer_subkeys, peer_u, peer_v, norm_ple, ple_w_gate,
               ple_w_proj, norm_final)
    y_prompt = encoder_trunk(x_prompt, p_prompt, *weights)
    y_sample = encoder_trunk(x_sample, p_sample, *weights)
    return (y_prompt, y_sample)
```

```python
import functools
import math

import jax
import jax.numpy as jnp
from jax import lax
from jax.experimental import pallas as pl
from jax.experimental.pallas import tpu as pltpu

F32 = jnp.float32
BF16 = jnp.bfloat16

D_MODEL = 1024
ATT_W = 512
ATT_HEADS = 4
ATT_V_DIM = 128
ATT_QK_DIM = 64
ROT_DIM = 16
ROPE_THETA = 500000.0
M_W = 512
M_HEADS = 4
M_HEAD_DIM = 128
M_CHUNK = 128
CONV_K = 5
N_GATES = 16
PEER_HEADS = 8
PEER_QDIM = 256
PEER_HALF = 128
N_KEYS = 128
PEER_TOPK = 16
PEER_SEL = PEER_HEADS * PEER_TOPK
PLE_DIM = 256
EPS = 1e-6
LAM_INIT = 0.8 - 0.6 * math.exp(-0.3 * 0)

LANES = 128
SUBLANES = 8
VMEM_LIMIT = 56 * 1024 * 1024

NEG_INF = float("-inf")


def _cparams(sem):
    return pltpu.CompilerParams(dimension_semantics=sem, vmem_limit_bytes=VMEM_LIMIT)


def _dot(a, b):
    return jnp.dot(a, b, preferred_element_type=F32)


def _dot_nt(a, b):
    return lax.dot_general(a, b, (((1,), (1,)), ((), ())), preferred_element_type=F32)


def _rms(x, w):
    ms = jnp.mean(x * x, axis=-1, keepdims=True)
    return (x * lax.rsqrt(ms + EPS)) * w


def _sigmoid(x):
    return 1.0 / (1.0 + jnp.exp(-x))


def _log_sigmoid(x):
    return jnp.minimum(x, 0.0) - jnp.log(1.0 + jnp.exp(-jnp.abs(x)))


def _inproj_kernel(x_ref, xp_ref, xn_ref, nw_ref, wqk_ref, wav_ref, wmqk_ref, wmvo_ref,
                   wg_ref, wgt_ref, gb_ref, gbt_ref, cw_ref, rc_ref, rs1_ref, rs2_ref,
                   aq_ref, ak_ref, av_ref, mq_ref, mk_ref, mv_ref, mo_ref, gc_ref, gr_ref):
    i = pl.program_id(1)
    last = pl.num_programs(1) - 1
    nw = nw_ref[...]
    tm = x_ref.shape[1]

    h = _rms(x_ref[0], nw).astype(BF16)
    hp = _rms(xp_ref[0, 0], nw).astype(BF16)
    hn = _rms(xn_ref[0, 0], nw).astype(BF16)

    qk = _dot(h, wqk_ref[...])
    rc = rc_ref[...]
    rs1 = rs1_ref[...]
    rs2 = rs2_ref[...]
    half = ROT_DIM // 2
    for c in range(2 * ATT_W // LANES):
        xc = qk[:, c * LANES:(c + 1) * LANES]
        r = xc * rc + pltpu.roll(xc, half, 1) * rs1 + pltpu.roll(xc, LANES - half, 1) * rs2
        if c < ATT_W // LANES:
            aq_ref[0, :, c * LANES:(c + 1) * LANES] = (r * (ATT_QK_DIM ** -0.5)).astype(BF16)
        else:
            cc = c - ATT_W // LANES
            ak_ref[0, :, cc * LANES:(cc + 1) * LANES] = r.astype(BF16)

    av_ref[0] = _dot(h, wav_ref[...]).astype(BF16)

    wm = wmqk_ref[...]
    pm = _dot(h, wm)
    pp = _dot(hp, wm) * (i > 0).astype(F32)
    pn = _dot(hn, wm) * (i < last).astype(F32)
    ext = jnp.concatenate([pp, pm, pn], axis=0)
    cw = cw_ref[...]
    off = SUBLANES - CONV_K // 2
    conv = cw[0:1, :] * ext[off:off + tm, :]
    for j in range(1, CONV_K):
        conv = conv + cw[j:j + 1, :] * ext[off + j:off + j + tm, :]
    act = conv * _sigmoid(conv)
    mq_ref[0] = act[:, :M_W].astype(BF16)
    mk_ref[0] = (act[:, M_W:] * (M_HEAD_DIM ** -0.5)).astype(BF16)

    vo = _dot(h, wmvo_ref[...])
    mv_ref[0] = vo[:, :M_W].astype(BF16)
    mo_ref[0] = vo[:, M_W:].astype(BF16)

    gc_ref[0] = _dot(h, wg_ref[...]) + gb_ref[...]
    gr_ref[0] = _dot_nt(wgt_ref[...], h) + gbt_ref[...]


def _inproj(x, norm_w, wqk, wav, wmqk, wmvo, wg, wgt, gb, gbt, conv_w, rc, rs1, rs2, tm):
    b, s, d = x.shape
    nt = s // tm
    x8 = x.reshape(b, s // SUBLANES, SUBLANES, d)
    r8 = tm // SUBLANES
    nb8 = s // SUBLANES
    full = lambda shape: pl.BlockSpec(shape, lambda bi, i: (0,) * len(shape))
    tok = lambda w: pl.BlockSpec((1, tm, w), lambda bi, i: (bi, i, 0))
    out_shapes = (
        [jax.ShapeDtypeStruct((b, s, ATT_W), BF16)] * 3
        + [jax.ShapeDtypeStruct((b, s, M_W), BF16)] * 4
        + [jax.ShapeDtypeStruct((b, s, N_GATES), F32), jax.ShapeDtypeStruct((b, N_GATES, s), F32)]
    )
    return pl.pallas_call(
        _inproj_kernel,
        out_shape=out_shapes,
        grid=(b, nt),
        in_specs=[
            tok(d),
            pl.BlockSpec((1, 1, SUBLANES, d), lambda bi, i: (bi, jnp.maximum(i * r8 - 1, 0), 0, 0)),
            pl.BlockSpec((1, 1, SUBLANES, d), lambda bi, i: (bi, jnp.minimum((i + 1) * r8, nb8 - 1), 0, 0)),
            full((1, d)),
            full(wqk.shape), full(wav.shape), full(wmqk.shape), full(wmvo.shape),
            full(wg.shape), full(wgt.shape), full(gb.shape), full(gbt.shape), full(conv_w.shape),
            pl.BlockSpec((tm, LANES), lambda bi, i: (i, 0)),
            pl.BlockSpec((tm, LANES), lambda bi, i: (i, 0)),
            pl.BlockSpec((tm, LANES), lambda bi, i: (i, 0)),
        ],
        out_specs=[tok(ATT_W)] * 3 + [tok(M_W)] * 4
        + [tok(N_GATES), pl.BlockSpec((1, N_GATES, tm), lambda bi, i: (bi, 0, i))],
        compiler_params=_cparams(("parallel", "parallel")),
        name="inproj",
    )(x, x8, x8, norm_w, wqk, wav, wmqk, wmvo, wg, wgt, gb, gbt, conv_w, rc, rs1, rs2)


def _attn_kernel(lq_ref, q_ref, k_ref, v_ref, gain_ref, o_ref):
    lq = lq_ref[...]
    lam = (jnp.exp(jnp.sum(lq[0:1] * lq[1:2], axis=1, keepdims=True))
           - jnp.exp(jnp.sum(lq[2:3] * lq[3:4], axis=1, keepdims=True)) + LAM_INIT)
    q = q_ref[0]
    k = k_ref[0]
    v = v_ref[0]
    lane = lax.broadcasted_iota(jnp.int32, q.shape, 1)
    zero = jnp.zeros_like(q)

    def soft(qh):
        sc = _dot_nt(qh, k)
        m = jnp.max(sc, axis=1, keepdims=True)
        e = jnp.exp(sc - m)
        return e, jnp.sum(e, axis=1, keepdims=True)

    e0, l0 = soft(jnp.where(lane < ATT_QK_DIM, q, zero))
    e1, l1 = soft(jnp.where(lane >= ATT_QK_DIM, q, zero))
    a = e0 * (1.0 / l0) - e1 * (lam / l1)
    o = _dot(a.astype(BF16), v)
    o_ref[0] = (_rms(o, gain_ref[...]) * (1.0 - LAM_INIT)).astype(BF16)


def _attn(lq, q, k, v, gain, tq):
    b, s, _ = q.shape
    return pl.pallas_call(
        _attn_kernel,
        out_shape=jax.ShapeDtypeStruct((b, s, ATT_W), BF16),
        grid=(b, ATT_HEADS, s // tq),
        in_specs=[
            pl.BlockSpec(lq.shape, lambda bi, h, i: (0, 0)),
            pl.BlockSpec((1, tq, ATT_V_DIM), lambda bi, h, i: (bi, i, h)),
            pl.BlockSpec((1, s, ATT_V_DIM), lambda bi, h, i: (bi, 0, h)),
            pl.BlockSpec((1, s, ATT_V_DIM), lambda bi, h, i: (bi, 0, h)),
            pl.BlockSpec((1, ATT_V_DIM), lambda bi, h, i: (0, 0)),
        ],
        out_specs=pl.BlockSpec((1, tq, ATT_V_DIM), lambda bi, h, i: (bi, i, h)),
        compiler_params=_cparams(("parallel", "parallel", "parallel")),
        name="attn",
    )(lq, q, k, v, gain)


def _mlstm_chunk(q, k, v, ic_col, lf_col, ic_row, lf_row, cst, n, m, fwd):
    ln = q.shape[0]
    row = lax.broadcasted_iota(jnp.int32, (ln, ln), 0)
    col = lax.broadcasted_iota(jnp.int32, (ln, ln), 1)
    lower = col <= row
    upper = col >= row
    mask, mask_t = (lower, upper) if fwd else (upper, lower)
    bcum_col = jnp.sum(jnp.where(mask, lf_row, 0.0), axis=1, keepdims=True)
    bcum_row = jnp.sum(jnp.where(mask_t, lf_col, 0.0), axis=0, keepdims=True)
    dlog = jnp.where(mask, bcum_col - bcum_row + ic_row, NEG_INF)
    inter = bcum_col + m
    m_t = jnp.maximum(inter, jnp.max(dlog, axis=1, keepdims=True))
    w = jnp.exp(dlog - m_t)
    sc = jnp.exp(inter - m_t)
    wqk = w * _dot_nt(q, k)
    num = sc * _dot(q, cst.astype(BF16)) + _dot(wqk.astype(BF16), v)
    qn = jnp.sum(q.astype(F32) * n, axis=1, keepdims=True)
    den = sc * qn + jnp.sum(wqk, axis=1, keepdims=True)
    h = num / jnp.maximum(jnp.abs(den), jnp.exp(-m_t))
    b_last = jnp.sum(lf_row, axis=1, keepdims=True)
    rel_row = b_last - bcum_row + ic_row
    rel_col = b_last - bcum_col + ic_col
    m_new = jnp.maximum(b_last + m, jnp.max(rel_row, axis=1, keepdims=True))
    dec_col = jnp.exp(rel_col - m_new)
    keep = jnp.exp(b_last + m - m_new)
    kd = k.astype(F32) * dec_col
    c_new = keep * cst + _dot(kd.T.astype(BF16), v)
    n_new = keep * n + jnp.sum(kd, axis=0, keepdims=True)
    return h, c_new, n_new, m_new


def _mlstm_kernel(q_ref, k_ref, v_ref, o_ref, gc_ref, gr_ref, gain_ref, out_ref, hf_ref):
    s = q_ref.shape[1]
    nc = s // M_CHUNK
    d = M_HEAD_DIM
    gain = gain_ref[...]

    def step(c, carry, fwd):
        cst, n, m = carry
        sl = pl.ds(pl.multiple_of(c * M_CHUNK, M_CHUNK), M_CHUNK)
        gc = gc_ref[0, 0, c]
        gr = gr_ref[0, 0, c]
        t = 0 if fwd else 2
        h, cst, n, m = _mlstm_chunk(
            q_ref[0, sl, :], k_ref[0, sl, :], v_ref[0, sl, :],
            gc[:, t:t + 1], _log_sigmoid(gc[:, t + 1:t + 2]),
            gr[t:t + 1, :], _log_sigmoid(gr[t + 1:t + 2, :]),
            cst, n, m, fwd)
        return sl, h, (cst, n, m)

    init = (jnp.zeros((d, d), F32), jnp.zeros((1, d), F32), jnp.zeros((1, 1), F32))

    def fwd_body(c, carry):
        sl, h, carry = step(c, carry, True)
        hf_ref[sl, :] = h
        return carry

    lax.fori_loop(0, nc, fwd_body, init)

    def bwd_body(j, carry):
        sl, h, carry = step(nc - 1 - j, carry, False)
        hs = hf_ref[sl, :] + h
        y = _sigmoid(o_ref[0, sl, :].astype(F32)) * hs
        out_ref[0, sl, :] = _rms(y, gain).astype(BF16)
        return carry

    lax.fori_loop(0, nc, bwd_body, init)


def _mlstm(q, k, v, o, gc, gr, gain):
    b, s, _ = q.shape
    nc = s // M_CHUNK
    head = pl.BlockSpec((1, s, M_HEAD_DIM), lambda bi, h: (bi, 0, h))
    return pl.pallas_call(
        _mlstm_kernel,
        out_shape=jax.ShapeDtypeStruct((b, s, M_W), BF16),
        grid=(b, M_HEADS),
        in_specs=[
            head, head, head, head,
            pl.BlockSpec((1, 1, nc, M_CHUNK, 4), lambda bi, h: (bi, h, 0, 0, 0)),
            pl.BlockSpec((1, 1, nc, 4, M_CHUNK), lambda bi, h: (bi, h, 0, 0, 0)),
            pl.BlockSpec((1, M_HEAD_DIM), lambda bi, h: (0, h)),
        ],
        out_specs=head,
        scratch_shapes=[pltpu.VMEM((s, M_HEAD_DIM), F32)],
        compiler_params=_cparams(("parallel", "parallel")),
        name="mlstm",
    )(q, k, v, o, gc, gr, gain)


def _outproj_kernel(x_ref, att_ref, mem_ref, wa_ref, wm_ref, nw_ref, wq_ref, x1_ref, xn_ref, qp_ref):
    x1 = x_ref[...] + _dot(att_ref[...], wa_ref[...]) + _dot(mem_ref[...], wm_ref[...])
    x1_ref[...] = x1
    xn = _rms(x1, nw_ref[...])
    xn_ref[...] = xn
    qp = _dot(xn.astype(BF16), wq_ref[...])
    for c in range(qp_ref.shape[0]):
        qp_ref[c] = qp[:, c * LANES:(c + 1) * LANES].astype(BF16)


def _outproj(x, att, mem, wa, wm, nw, wq, tm):
    t, d = x.shape
    nq = wq.shape[1] // LANES
    full = lambda shape: pl.BlockSpec(shape, lambda i: (0,) * len(shape))
    tok = lambda w: pl.BlockSpec((tm, w), lambda i: (i, 0))
    return pl.pallas_call(
        _outproj_kernel,
        out_shape=[jax.ShapeDtypeStruct((t, d), F32), jax.ShapeDtypeStruct((t, d), F32),
                   jax.ShapeDtypeStruct((nq, t, LANES), BF16)],
        grid=(t // tm,),
        in_specs=[tok(d), tok(ATT_W), tok(M_W), full(wa.shape), full(wm.shape), full(nw.shape),
                  full(wq.shape)],
        out_specs=[tok(d), tok(d), pl.BlockSpec((nq, tm, LANES), lambda i: (0, i, 0))],
        compiler_params=_cparams(("parallel",)),
        name="outproj",
    )(x, att, mem, wa, wm, nw, wq)


def _topk_rows(sc, n):
    pos = lax.broadcasted_iota(jnp.int32, sc.shape, 0)
    vals, poss = [], []
    for _ in range(PEER_TOPK):
        m = jnp.max(sc, axis=0, keepdims=True)
        p = jnp.min(jnp.where(sc == m, pos, n), axis=0, keepdims=True)
        vals.append(m)
        poss.append(p)
        sc = jnp.where(pos == p, NEG_INF, sc)
    return jnp.concatenate(vals, axis=0), jnp.concatenate(poss, axis=0)


def _route_kernel(qp_ref, sk_ref, idx_ref, g_ref, idx_t, g_t):
    def head(h, carry):
        s1 = _dot_nt(sk_ref[2 * h], qp_ref[2 * h])
        s2 = _dot_nt(sk_ref[2 * h + 1], qp_ref[2 * h + 1])
        v1, i1 = _topk_rows(s1, N_KEYS)
        v2, i2 = _topk_rows(s2, N_KEYS)
        cand = jnp.concatenate([v1[a:a + 1, :] + v2 for a in range(PEER_TOPK)], axis=0)
        sc, ci = _topk_rows(cand, PEER_TOPK * PEER_TOPK)
        ca = ci // PEER_TOPK
        cb = ci % PEER_TOPK
        e1 = jnp.zeros_like(ci)
        e2 = jnp.zeros_like(ci)
        for a in range(PEER_TOPK):
            e1 = jnp.where(ca == a, i1[a:a + 1, :], e1)
            e2 = jnp.where(cb == a, i2[a:a + 1, :], e2)
        ex = jnp.exp(sc - sc[0:1, :])
        rows = pl.ds(pl.multiple_of(h * PEER_TOPK, PEER_TOPK), PEER_TOPK)
        idx_t[rows, :] = e1 * N_KEYS + e2
        g_t[rows, :] = ex / jnp.sum(ex, axis=0, keepdims=True)
        return carry

    lax.fori_loop(0, PEER_HEADS, head, 0)
    idx_ref[...] = idx_t[...].T
    g_ref[...] = g_t[...].T


def _route(qp, sk, tm):
    nq, t, _ = qp.shape
    return pl.pallas_call(
        _route_kernel,
        out_shape=[jax.ShapeDtypeStruct((t, PEER_SEL), jnp.int32),
                   jax.ShapeDtypeStruct((t, PEER_SEL), F32)],
        grid=(t // tm,),
        in_specs=[pl.BlockSpec((nq, tm, LANES), lambda i: (0, i, 0)),
                  pl.BlockSpec(sk.shape, lambda i: (0, 0, 0))],
        out_specs=[pl.BlockSpec((tm, PEER_SEL), lambda i: (i, 0))] * 2,
        scratch_shapes=[pltpu.VMEM((PEER_SEL, tm), jnp.int32), pltpu.VMEM((PEER_SEL, tm), F32)],
        compiler_params=_cparams(("parallel",)),
        name="route",
    )(qp, sk)


PEER_TB = 128
PEER_UNROLL = 8


def _gelu(a):
    return 0.5 * a * (1.0 + lax.erf(a * (2.0 ** -0.5)))


def _peer_kernel(idx_hbm, xn_ref, g_ref, u_hbm, v_hbm, out_ref, idx_s, ubuf, vbuf, sem_i, sem_u, sem_v):
    step = pl.program_id(0)
    tb = xn_ref.shape[0]
    nsel = PEER_SEL

    cp = pltpu.make_async_copy(idx_hbm.at[step], idx_s, sem_i.at[0])
    cp.start()
    cp.wait()

    def issue(tok, slot):
        base = tok * nsel

        def body(kk, carry):
            for j in range(PEER_UNROLL):
                k = kk * PEER_UNROLL + j
                e = idx_s[base + k]
                pltpu.make_async_copy(u_hbm.at[pl.ds(e, 1)], ubuf.at[slot, pl.ds(k, 1)], sem_u.at[slot]).start()
                pltpu.make_async_copy(v_hbm.at[pl.ds(e, 1)], vbuf.at[slot, pl.ds(k, 1)], sem_v.at[slot]).start()
            return carry

        lax.fori_loop(0, nsel // PEER_UNROLL, body, 0)

    def wait(slot):
        pltpu.make_async_copy(u_hbm.at[pl.ds(0, nsel)], ubuf.at[slot], sem_u.at[slot]).wait()
        pltpu.make_async_copy(v_hbm.at[pl.ds(0, nsel)], vbuf.at[slot], sem_v.at[slot]).wait()

    def compute(xr, gr, slot):
        nch = D_MODEL // LANES
        ub = ubuf[slot]
        prod = ub * xr
        part = prod[:, 0:LANES]
        for c in range(1, nch):
            part = part + prod[:, c * LANES:(c + 1) * LANES]
        a_row = jnp.sum(part.T, axis=0, keepdims=True)
        act = _gelu(a_row) * gr
        act_b = jnp.broadcast_to(act, (LANES, nsel)).T
        vb = vbuf[slot]
        outs = [jnp.sum(act_b * vb[:, c * LANES:(c + 1) * LANES], axis=0, keepdims=True)
                for c in range(nch)]
        return jnp.concatenate(outs, axis=1)

    issue(0, 0)

    def group(gi, carry):
        r0 = pl.multiple_of(gi * SUBLANES, SUBLANES)
        xg = xn_ref[pl.ds(r0, SUBLANES), :]
        gg = g_ref[pl.ds(r0, SUBLANES), :]
        rows = []
        for j in range(SUBLANES):
            tok = gi * SUBLANES + j
            slot = j % 2

            @pl.when(tok + 1 < tb)
            def _():
                issue(tok + 1, 1 - slot)

            wait(slot)
            rows.append(compute(xg[j:j + 1, :], gg[j:j + 1, :], slot))
        out_ref[pl.ds(r0, SUBLANES), :] = jnp.concatenate(rows, axis=0)
        return carry

    lax.fori_loop(0, tb // SUBLANES, group, 0)


def _peer(idx, xn, g, u, v):
    t, d = xn.shape
    tb = PEER_TB
    idx2 = idx.reshape(t // tb, tb * PEER_SEL)
    return pl.pallas_call(
        _peer_kernel,
        out_shape=jax.ShapeDtypeStruct((t, d), F32),
        grid=(t // tb,),
        in_specs=[pl.BlockSpec(memory_space=pl.ANY),
                  pl.BlockSpec((tb, d), lambda i: (i, 0)),
                  pl.BlockSpec((tb, PEER_SEL), lambda i: (i, 0)),
                  pl.BlockSpec(memory_space=pl.ANY),
                  pl.BlockSpec(memory_space=pl.ANY)],
        out_specs=pl.BlockSpec((tb, d), lambda i: (i, 0)),
        scratch_shapes=[pltpu.SMEM((tb * PEER_SEL,), jnp.int32),
                        pltpu.VMEM((2, PEER_SEL, d), F32),
                        pltpu.VMEM((2, PEER_SEL, d), F32),
                        pltpu.SemaphoreType.DMA((1,)),
                        pltpu.SemaphoreType.DMA((2,)),
                        pltpu.SemaphoreType.DMA((2,))],
        compiler_params=_cparams(("arbitrary",)),
        name="peer",
    )(idx2, xn, g, u, v)


def _ple_kernel(x1_ref, pe_ref, p_ref, nw_ref, wg_ref, wp_ref, nf_ref, y_ref):
    x2 = x1_ref[...] + pe_ref[...]
    gate = _sigmoid(_dot(_rms(x2, nw_ref[...]).astype(BF16), wg_ref[...]))
    x3 = x2 + _dot(p_ref[...].astype(BF16), wp_ref[...]) * gate
    y_ref[...] = _rms(x3, nf_ref[...])


def _ple(x1, pe, p, nw, wg, wp, nf, tm):
    t, d = x1.shape
    full = lambda shape: pl.BlockSpec(shape, lambda i: (0,) * len(shape))
    tok = lambda w: pl.BlockSpec((tm, w), lambda i: (i, 0))
    return pl.pallas_call(
        _ple_kernel,
        out_shape=jax.ShapeDtypeStruct((t, d), F32),
        grid=(t // tm,),
        in_specs=[tok(d), tok(d), tok(PLE_DIM), full(nw.shape), full(wg.shape), full(wp.shape),
                  full(nf.shape)],
        out_specs=tok(d),
        compiler_params=_cparams(("parallel",)),
        name="ple",
    )(x1, pe, p, nw, wg, wp, nf)


def _rope_tables(s):
    half = ROT_DIM // 2
    inv = ROPE_THETA ** (-jnp.arange(0, ROT_DIM, 2, dtype=F32) / ROT_DIM)
    ang = jnp.arange(s, dtype=F32)[:, None] * inv[None, :]
    cos, sin = jnp.cos(ang), jnp.sin(ang)
    ones = jnp.ones((s, ATT_QK_DIM - ROT_DIM), F32)
    zeros = jnp.zeros((s, ATT_QK_DIM - ROT_DIM), F32)
    zh = jnp.zeros((s, half), F32)
    c64 = jnp.concatenate([cos, cos, ones], axis=1)
    s1_64 = jnp.concatenate([zh, sin, zeros], axis=1)
    s2_64 = jnp.concatenate([-sin, zh, zeros], axis=1)
    rep = LANES // ATT_QK_DIM
    return jnp.tile(c64, (1, rep)), jnp.tile(s1_64, (1, rep)), jnp.tile(s2_64, (1, rep))


def _trunk(x, p, w):
    b, s, d = x.shape
    t = b * s
    nc = s // M_CHUNK
    rc, rs1, rs2 = _rope_tables(s)
    aq, ak, av, mq, mk, mv, mo, gc, gr = _inproj(
        x, w["norm_mix"], w["wqk"], w["wav"], w["wmqk"], w["wmvo"], w["wg"], w["wgt"], w["gb"],
        w["gbt"], w["conv"], rc, rs1, rs2, tm=512)
    att = _attn(w["lq"], aq, ak, av, w["attn_gain"], tq=128)
    gc5 = gc.reshape(b, nc, M_CHUNK, M_HEADS, 4).transpose(0, 3, 1, 2, 4)
    gr5 = gr.reshape(b, M_HEADS, 4, nc, M_CHUNK).transpose(0, 1, 3, 2, 4)
    mem = _mlstm(mq, mk, mv, mo, gc5, gr5, w["mlstm_gain"])
    x1, xn, qp = _outproj(x.reshape(t, d), att.reshape(t, ATT_W), mem.reshape(t, M_W),
                          w["wo_a"], w["wo_m"], w["norm_ffn"], w["wq"], tm=512)
    idx, g = _route(qp, w["sk"], tm=128)
    pe = _peer(idx, xn, g, w["u"], w["v"])
    y = _ple(x1, pe, p.reshape(t, PLE_DIM), w["norm_ple"], w["ple_wg"], w["ple_wp"],
             w["norm_final"], tm=512)
    return y.reshape(b, s, d)


def kernel(x_prompt, x_sample, p_prompt, p_sample, norm_mix, w_in, gate_bias, conv_qk, lambda_qk,
           attn_gain, mlstm_gain, w_out, norm_ffn, peer_wq, peer_subkeys, peer_u, peer_v, norm_ple,
           ple_w_gate, ple_w_proj, norm_final):
    li = 0
    win = w_in[li]
    c0 = 0
    cols = {}
    for name, width in (("aq", ATT_W), ("ak", ATT_W), ("av", ATT_W), ("mq", M_W), ("mk", M_W),
                        ("mv", M_W), ("mo", M_W), ("gt", N_GATES)):
        cols[name] = win[:, c0:c0 + width]
        c0 += width
    perm = jnp.array([ty * M_HEADS + h for h in range(M_HEADS) for ty in range(4)], jnp.int32)
    wg = cols["gt"][:, perm].astype(BF16)
    gb = gate_bias[li][perm]
    w = {
        "norm_mix": norm_mix[li][None, :],
        "wqk": jnp.concatenate([cols["aq"], cols["ak"]], axis=1).astype(BF16),
        "wav": cols["av"].astype(BF16),
        "wmqk": jnp.concatenate([cols["mq"], cols["mk"]], axis=1).astype(BF16),
        "wmvo": jnp.concatenate([cols["mv"], cols["mo"]], axis=1).astype(BF16),
        "wg": wg,
        "wgt": wg.T,
        "gb": gb[None, :],
        "gbt": gb[:, None],
        "conv": conv_qk[li],
        "lq": lambda_qk[li],
        "attn_gain": attn_gain[li][None, :],
        "mlstm_gain": mlstm_gain[li][None, :],
        "wo_a": w_out[li][:ATT_W].astype(BF16),
        "wo_m": w_out[li][ATT_W:].astype(BF16),
        "norm_ffn": norm_ffn[li][None, :],
        "wq": peer_wq[li].astype(BF16),
        "sk": peer_subkeys[li].reshape(PEER_HEADS * 2, N_KEYS, PEER_HALF).astype(BF16),
        "u": peer_u[li],
        "v": peer_v[li],
        "norm_ple": norm_ple[li][None, :],
        "ple_wg": ple_w_gate[li].astype(BF16),
        "ple_wp": ple_w_proj[li].astype(BF16),
        "norm_final": norm_final[None, :],
    }
    y_prompt = _trunk(x_prompt, p_prompt[li], w)
    y_sample = _trunk(x_sample, p_sample[li], w)
    return (y_prompt, y_sample)
```

```python
import functools
import math

import jax
import jax.numpy as jnp
from jax import lax
from jax.experimental import pallas as pl
from jax.experimental.pallas import tpu as pltpu

F32 = jnp.float32
BF16 = jnp.bfloat16

D_MODEL = 1024
ATT_W = 512
ATT_HEADS = 4
ATT_V_DIM = 128
ATT_QK_DIM = 64
ROT_DIM = 16
ROPE_THETA = 500000.0
M_W = 512
M_HEADS = 4
M_HEAD_DIM = 128
M_CHUNK = 128
CONV_K = 5
N_GATES = 16
PEER_HEADS = 8
PEER_QDIM = 256
PEER_HALF = 128
N_KEYS = 128
PEER_TOPK = 16
PEER_SEL = PEER_HEADS * PEER_TOPK
PLE_DIM = 256
EPS = 1e-6
LAM_INIT = 0.8 - 0.6 * math.exp(-0.3 * 0)

LANES = 128
SUBLANES = 8
VMEM_LIMIT = 56 * 1024 * 1024

NEG_INF = float("-inf")


def _cparams(sem):
    return pltpu.CompilerParams(dimension_semantics=sem, vmem_limit_bytes=VMEM_LIMIT)


def _dot(a, b):
    return jnp.dot(a, b, preferred_element_type=F32)


def _dot_nt(a, b):
    return lax.dot_general(a, b, (((1,), (1,)), ((), ())), preferred_element_type=F32)


def _rms(x, w):
    ms = jnp.mean(x * x, axis=-1, keepdims=True)
    return (x * lax.rsqrt(ms + EPS)) * w


def _sigmoid(x):
    return 1.0 / (1.0 + jnp.exp(-x))


def _log_sigmoid(x):
    return jnp.minimum(x, 0.0) - jnp.log(1.0 + jnp.exp(-jnp.abs(x)))


def _inproj_kernel(x_ref, xp_ref, xn_ref, nw_ref, wqk_ref, wav_ref, wmqk_ref, wmvo_ref,
                   wg_ref, wgt_ref, gb_ref, gbt_ref, cw_ref, rc_ref, rs1_ref, rs2_ref,
                   aq_ref, ak_ref, av_ref, mq_ref, mk_ref, mv_ref, mo_ref, gc_ref, gr_ref):
    i = pl.program_id(1)
    last = pl.num_programs(1) - 1
    nw = nw_ref[...]
    tm = x_ref.shape[1]

    h = _rms(x_ref[0], nw).astype(BF16)
    hp = _rms(xp_ref[0, 0], nw).astype(BF16)
    hn = _rms(xn_ref[0, 0], nw).astype(BF16)

    qk = _dot(h, wqk_ref[...])
    rc = rc_ref[...]
    rs1 = rs1_ref[...]
    rs2 = rs2_ref[...]
    half = ROT_DIM // 2
    for c in range(2 * ATT_W // LANES):
        xc = qk[:, c * LANES:(c + 1) * LANES]
        r = xc * rc + pltpu.roll(xc, half, 1) * rs1 + pltpu.roll(xc, LANES - half, 1) * rs2
        if c < ATT_W // LANES:
            aq_ref[0, :, c * LANES:(c + 1) * LANES] = (r * (ATT_QK_DIM ** -0.5)).astype(BF16)
        else:
            cc = c - ATT_W // LANES
            ak_ref[0, :, cc * LANES:(cc + 1) * LANES] = r.astype(BF16)

    av_ref[0] = _dot(h, wav_ref[...]).astype(BF16)

    wm = wmqk_ref[...]
    pm = _dot(h, wm)
    pp = _dot(hp, wm) * (i > 0).astype(F32)
    pn = _dot(hn, wm) * (i < last).astype(F32)
    ext = jnp.concatenate([pp, pm, pn], axis=0)
    cw = cw_ref[...]
    off = SUBLANES - CONV_K // 2
    conv = cw[0:1, :] * ext[off:off + tm, :]
    for j in range(1, CONV_K):
        conv = conv + cw[j:j + 1, :] * ext[off + j:off + j + tm, :]
    act = conv * _sigmoid(conv)
    mq_ref[0] = act[:, :M_W].astype(BF16)
    mk_ref[0] = (act[:, M_W:] * (M_HEAD_DIM ** -0.5)).astype(BF16)

    vo = _dot(h, wmvo_ref[...])
    mv_ref[0] = vo[:, :M_W].astype(BF16)
    mo_ref[0] = vo[:, M_W:].astype(BF16)

    gc_ref[0] = _dot(h, wg_ref[...]) + gb_ref[...]
    gr_ref[0] = _dot_nt(wgt_ref[...], h) + gbt_ref[...]


def _inproj(x, norm_w, wqk, wav, wmqk, wmvo, wg, wgt, gb, gbt, conv_w, rc, rs1, rs2, tm):
    b, s, d = x.shape
    nt = s // tm
    x8 = x.reshape(b, s // SUBLANES, SUBLANES, d)
    r8 = tm // SUBLANES
    nb8 = s // SUBLANES
    full = lambda shape: pl.BlockSpec(shape, lambda bi, i: (0,) * len(shape))
    tok = lambda w: pl.BlockSpec((1, tm, w), lambda bi, i: (bi, i, 0))
    out_shapes = (
        [jax.ShapeDtypeStruct((b, s, ATT_W), BF16)] * 3
        + [jax.ShapeDtypeStruct((b, s, M_W), BF16)] * 4
        + [jax.ShapeDtypeStruct((b, s, N_GATES), F32), jax.ShapeDtypeStruct((b, N_GATES, s), F32)]
    )
    return pl.pallas_call(
        _inproj_kernel,
        out_shape=out_shapes,
        grid=(b, nt),
        in_specs=[
            tok(d),
            pl.BlockSpec((1, 1, SUBLANES, d), lambda bi, i: (bi, jnp.maximum(i * r8 - 1, 0), 0, 0)),
            pl.BlockSpec((1, 1, SUBLANES, d), lambda bi, i: (bi, jnp.minimum((i + 1) * r8, nb8 - 1), 0, 0)),
            full((1, d)),
            full(wqk.shape), full(wav.shape), full(wmqk.shape), full(wmvo.shape),
            full(wg.shape), full(wgt.shape), full(gb.shape), full(gbt.shape), full(conv_w.shape),
            pl.BlockSpec((tm, LANES), lambda bi, i: (i, 0)),
            pl.BlockSpec((tm, LANES), lambda bi, i: (i, 0)),
            pl.BlockSpec((tm, LANES), lambda bi, i: (i, 0)),
        ],
        out_specs=[tok(ATT_W)] * 3 + [tok(M_W)] * 4
        + [tok(N_GATES), pl.BlockSpec((1, N_GATES, tm), lambda bi, i: (bi, 0, i))],
        compiler_params=_cparams(("parallel", "parallel")),
        name="inproj",
    )(x, x8, x8, norm_w, wqk, wav, wmqk, wmvo, wg, wgt, gb, gbt, conv_w, rc, rs1, rs2)


def _attn_kernel(lq_ref, q_ref, k_ref, v_ref, gain_ref, o_ref):
    lq = lq_ref[...]
    lam = (jnp.exp(jnp.sum(lq[0:1] * lq[1:2], axis=1, keepdims=True))
           - jnp.exp(jnp.sum(lq[2:3] * lq[3:4], axis=1, keepdims=True)) + LAM_INIT)
    q = q_ref[0]
    k = k_ref[0]
    v = v_ref[0]
    lane = lax.broadcasted_iota(jnp.int32, q.shape, 1)
    zero = jnp.zeros_like(q)

    def soft(qh):
        sc = _dot_nt(qh, k)
        m = jnp.max(sc, axis=1, keepdims=True)
        e = jnp.exp(sc - m)
        return e, jnp.sum(e, axis=1, keepdims=True)

    e0, l0 = soft(jnp.where(lane < ATT_QK_DIM, q, zero))
    e1, l1 = soft(jnp.where(lane >= ATT_QK_DIM, q, zero))
    a = e0 * (1.0 / l0) - e1 * (lam / l1)
    o = _dot(a.astype(BF16), v)
    o_ref[0] = (_rms(o, gain_ref[...]) * (1.0 - LAM_INIT)).astype(BF16)


def _attn(lq, q, k, v, gain, tq):
    b, s, _ = q.shape
    return pl.pallas_call(
        _attn_kernel,
        out_shape=jax.ShapeDtypeStruct((b, s, ATT_W), BF16),
        grid=(b, ATT_HEADS, s // tq),
        in_specs=[
            pl.BlockSpec(lq.shape, lambda bi, h, i: (0, 0)),
            pl.BlockSpec((1, tq, ATT_V_DIM), lambda bi, h, i: (bi, i, h)),
            pl.BlockSpec((1, s, ATT_V_DIM), lambda bi, h, i: (bi, 0, h)),
            pl.BlockSpec((1, s, ATT_V_DIM), lambda bi, h, i: (bi, 0, h)),
            pl.BlockSpec((1, ATT_V_DIM), lambda bi, h, i: (0, 0)),
        ],
        out_specs=pl.BlockSpec((1, tq, ATT_V_DIM), lambda bi, h, i: (bi, i, h)),
        compiler_params=_cparams(("parallel", "parallel", "parallel")),
        name="attn",
    )(lq, q, k, v, gain)


def _mlstm_chunk(q, k, v, ic_col, lf_col, ic_row, lf_row, cst, n, m, fwd):
    ln = q.shape[0]
    row = lax.broadcasted_iota(jnp.int32, (ln, ln), 0)
    col = lax.broadcasted_iota(jnp.int32, (ln, ln), 1)
    lower = col <= row
    upper = col >= row
    mask, mask_t = (lower, upper) if fwd else (upper, lower)
    bcum_col = jnp.sum(jnp.where(mask, lf_row, 0.0), axis=1, keepdims=True)
    bcum_row = jnp.sum(jnp.where(mask_t, lf_col, 0.0), axis=0, keepdims=True)
    dlog = jnp.where(mask, bcum_col - bcum_row + ic_row, NEG_INF)
    inter = bcum_col + m
    m_t = jnp.maximum(inter, jnp.max(dlog, axis=1, keepdims=True))
    w = jnp.exp(dlog - m_t)
    sc = jnp.exp(inter - m_t)
    wqk = w * _dot_nt(q, k)
    num = sc * _dot(q, cst.astype(BF16)) + _dot(wqk.astype(BF16), v)
    qn = jnp.sum(q.astype(F32) * n, axis=1, keepdims=True)
    den = sc * qn + jnp.sum(wqk, axis=1, keepdims=True)
    h = num / jnp.maximum(jnp.abs(den), jnp.exp(-m_t))
    b_last = jnp.sum(lf_row, axis=1, keepdims=True)
    rel_row = b_last - bcum_row + ic_row
    rel_col = b_last - bcum_col + ic_col
    m_new = jnp.maximum(b_last + m, jnp.max(rel_row, axis=1, keepdims=True))
    dec_col = jnp.exp(rel_col - m_new)
    keep = jnp.exp(b_last + m - m_new)
    kd = k.astype(F32) * dec_col
    c_new = keep * cst + _dot(kd.T.astype(BF16), v)
    n_new = keep * n + jnp.sum(kd, axis=0, keepdims=True)
    return h, c_new, n_new, m_new


def _mlstm_kernel(q_ref, k_ref, v_ref, o_ref, gc_ref, gr_ref, gain_ref, out_ref, hf_ref):
    s = q_ref.shape[1]
    nc = s // M_CHUNK
    d = M_HEAD_DIM
    gain = gain_ref[...]

    def step(c, carry, fwd):
        cst, n, m = carry
        sl = pl.ds(pl.multiple_of(c * M_CHUNK, M_CHUNK), M_CHUNK)
        gc = gc_ref[0, 0, c]
        gr = gr_ref[0, 0, c]
        t = 0 if fwd else 2
        h, cst, n, m = _mlstm_chunk(
            q_ref[0, sl, :], k_ref[0, sl, :], v_ref[0, sl, :],
            gc[:, t:t + 1], _log_sigmoid(gc[:, t + 1:t + 2]),
            gr[t:t + 1, :], _log_sigmoid(gr[t + 1:t + 2, :]),
            cst, n, m, fwd)
        return sl, h, (cst, n, m)

    init = (jnp.zeros((d, d), F32), jnp.zeros((1, d), F32), jnp.zeros((1, 1), F32))

    def fwd_body(c, carry):
        sl, h, carry = step(c, carry, True)
        hf_ref[sl, :] = h
        return carry

    lax.fori_loop(0, nc, fwd_body, init)

    def bwd_body(j, carry):
        sl, h, carry = step(nc - 1 - j, carry, False)
        hs = hf_ref[sl, :] + h
        y = _sigmoid(o_ref[0, sl, :].astype(F32)) * hs
        out_ref[0, sl, :] = _rms(y, gain).astype(BF16)
        return carry

    lax.fori_loop(0, nc, bwd_body, init)


def _mlstm(q, k, v, o, gc, gr, gain):
    b, s, _ = q.shape
    nc = s // M_CHUNK
    head = pl.BlockSpec((1, s, M_HEAD_DIM), lambda bi, h: (bi, 0, h))
    return pl.pallas_call(
        _mlstm_kernel,
        out_shape=jax.ShapeDtypeStruct((b, s, M_W), BF16),
        grid=(b, M_HEADS),
        in_specs=[
            head, head, head, head,
            pl.BlockSpec((1, 1, nc, M_CHUNK, 4), lambda bi, h: (bi, h, 0, 0, 0)),
            pl.BlockSpec((1, 1, nc, 4, M_CHUNK), lambda bi, h: (bi, h, 0, 0, 0)),
            pl.BlockSpec((1, M_HEAD_DIM), lambda bi, h: (0, h)),
        ],
        out_specs=head,
        scratch_shapes=[pltpu.VMEM((s, M_HEAD_DIM), F32)],
        compiler_params=_cparams(("parallel", "parallel")),
        name="mlstm",
    )(q, k, v, o, gc, gr, gain)


def _outproj_kernel(x_ref, att_ref, mem_ref, wa_ref, wm_ref, nw_ref, wq_ref, x1_ref, xn_ref, qp_ref):
    x1 = x_ref[...] + _dot(att_ref[...], wa_ref[...]) + _dot(mem_ref[...], wm_ref[...])
    x1_ref[...] = x1
    xn = _rms(x1, nw_ref[...])
    xn_ref[...] = xn
    qp = _dot(xn.astype(BF16), wq_ref[...])
    for c in range(qp_ref.shape[0]):
        qp_ref[c] = qp[:, c * LANES:(c + 1) * LANES].astype(BF16)


def _outproj(x, att, mem, wa, wm, nw, wq, tm):
    t, d = x.shape
    nq = wq.shape[1] // LANES
    full = lambda shape: pl.BlockSpec(shape, lambda i: (0,) * len(shape))
    tok = lambda w: pl.BlockSpec((tm, w), lambda i: (i, 0))
    return pl.pallas_call(
        _outproj_kernel,
        out_shape=[jax.ShapeDtypeStruct((t, d), F32), jax.ShapeDtypeStruct((t, d), F32),
                   jax.ShapeDtypeStruct((nq, t, LANES), BF16)],
        grid=(t // tm,),
        in_specs=[tok(d), tok(ATT_W), tok(M_W), full(wa.shape), full(wm.shape), full(nw.shape),
                  full(wq.shape)],
        out_specs=[tok(d), tok(d), pl.BlockSpec((nq, tm, LANES), lambda i: (0, i, 0))],
        compiler_params=_cparams(("parallel",)),
        name="outproj",
    )(x, att, mem, wa, wm, nw, wq)


def _topk_rows(sc, n):
    pos = lax.broadcasted_iota(jnp.int32, sc.shape, 0)
    vals, poss = [], []
    for _ in range(PEER_TOPK):
        m = jnp.max(sc, axis=0, keepdims=True)
        p = jnp.min(jnp.where(sc == m, pos, n), axis=0, keepdims=True)
        vals.append(m)
        poss.append(p)
        sc = jnp.where(pos == p, NEG_INF, sc)
    return jnp.concatenate(vals, axis=0), jnp.concatenate(poss, axis=0)


def _route_kernel(qp_ref, sk_ref, idx_ref, g_ref, idx_t, g_t):
    def head(h, carry):
        s1 = _dot_nt(sk_ref[2 * h], qp_ref[2 * h])
        s2 = _dot_nt(sk_ref[2 * h + 1], qp_ref[2 * h + 1])
        v1, i1 = _topk_rows(s1, N_KEYS)
        v2, i2 = _topk_rows(s2, N_KEYS)
        cand = jnp.concatenate([v1[a:a + 1, :] + v2 for a in range(PEER_TOPK)], axis=0)
        sc, ci = _topk_rows(cand, PEER_TOPK * PEER_TOPK)
        ca = ci // PEER_TOPK
        cb = ci % PEER_TOPK
        e1 = jnp.zeros_like(ci)
        e2 = jnp.zeros_like(ci)
        for a in range(PEER_TOPK):
            e1 = jnp.where(ca == a, i1[a:a + 1, :], e1)
            e2 = jnp.where(cb == a, i2[a:a + 1, :], e2)
        ex = jnp.exp(sc - sc[0:1, :])
        rows = pl.ds(pl.multiple_of(h * PEER_TOPK, PEER_TOPK), PEER_TOPK)
        idx_t[rows, :] = e1 * N_KEYS + e2
        g_t[rows, :] = ex / jnp.sum(ex, axis=0, keepdims=True)
        return carry

    lax.fori_loop(0, PEER_HEADS, head, 0)
    idx_ref[...] = idx_t[...].T
    g_ref[...] = g_t[...].T


def _route(qp, sk, tm):
    nq, t, _ = qp.shape
    return pl.pallas_call(
        _route_kernel,
        out_shape=[jax.ShapeDtypeStruct((t, PEER_SEL), jnp.int32),
                   jax.ShapeDtypeStruct((t, PEER_SEL), F32)],
        grid=(t // tm,),
        in_specs=[pl.BlockSpec((nq, tm, LANES), lambda i: (0, i, 0)),
                  pl.BlockSpec(sk.shape, lambda i: (0, 0, 0))],
        out_specs=[pl.BlockSpec((tm, PEER_SEL), lambda i: (i, 0))] * 2,
        scratch_shapes=[pltpu.VMEM((PEER_SEL, tm), jnp.int32), pltpu.VMEM((PEER_SEL, tm), F32)],
        compiler_params=_cparams(("parallel",)),
        name="route",
    )(qp, sk)


PEER_TB = 256
PEER_NB = SUBLANES
PEER_NG = 4
PEER_LA = 2
HI16 = -65536


def _gelu(a):
    return 0.5 * a * (1.0 + lax.erf(a * (2.0 ** -0.5)))


def _peer_kernel(idx_hbm, xn_ref, g_ref, uv_hbm, out_ref, idx_s, buf, sem_i, sem):
    step = pl.program_id(0)
    nbatch = xn_ref.shape[0] // PEER_NB
    nch = D_MODEL // LANES
    nkt = PEER_SEL // SUBLANES

    cp = pltpu.make_async_copy(idx_hbm.at[step], idx_s, sem_i.at[0])
    cp.start()
    cp.wait()

    def issue(b):
        grp = b & (PEER_NG - 1)

        def tok_body(tb, carry):
            base = (b * PEER_NB + tb) * PEER_SEL
            for k in range(PEER_SEL):
                e = idx_s[base + k]
                pltpu.make_async_copy(
                    uv_hbm.at[e], buf.at[grp, tb, k // SUBLANES, :, k % SUBLANES, :], sem.at[grp]
                ).start(priority=k % 2)
            return carry

        lax.fori_loop(0, PEER_NB, tok_body, 0)

    def wait(b):
        grp = b & (PEER_NG - 1)
        pltpu.make_async_copy(buf.at[grp], buf.at[grp], sem.at[grp]).wait()

    def compute(b):
        grp = b & (PEER_NG - 1)
        r0 = pl.multiple_of(b * PEER_NB, PEER_NB)
        xg = xn_ref[pl.ds(r0, PEER_NB), :]
        gg = g_ref[pl.ds(r0, PEER_NB), :]
        rows = []
        for tb in range(PEER_NB):
            xb = [jnp.broadcast_to(xg[tb:tb + 1, c * LANES:(c + 1) * LANES], (SUBLANES, LANES))
                  for c in range(nch)]
            parts = []
            for kt in range(nkt):
                acc = None
                for c in range(nch):
                    u = lax.bitcast_convert_type(buf[grp, tb, kt, c] & HI16, F32)
                    acc = u * xb[c] if acc is None else acc + u * xb[c]
                parts.append(acc)
            part = jnp.concatenate(parts, axis=0)
            a_row = jnp.sum(part.T, axis=0, keepdims=True)
            act = _gelu(a_row) * gg[tb:tb + 1, :]
            act_b = jnp.broadcast_to(act, (LANES, PEER_SEL)).T
            outs = []
            for c in range(nch):
                acc = None
                for kt in range(nkt):
                    v = lax.bitcast_convert_type(buf[grp, tb, kt, c] << 16, F32)
                    t = act_b[kt * SUBLANES:(kt + 1) * SUBLANES, :] * v
                    acc = t if acc is None else acc + t
                outs.append(jnp.sum(acc, axis=0, keepdims=True))
            rows.append(jnp.concatenate(outs, axis=1))
        out_ref[pl.ds(r0, PEER_NB), :] = jnp.concatenate(rows, axis=0)

    for b in range(PEER_LA):
        issue(b)

    def body(b, carry):
        wait(b)

        @pl.when(b + PEER_LA < nbatch)
        def _():
            issue(b + PEER_LA)

        compute(b)
        return carry

    lax.fori_loop(0, nbatch, body, 0)


def _peer(idx, xn, g, uv):
    t, d = xn.shape
    tb = PEER_TB
    idx2 = idx.reshape(t // tb, tb * PEER_SEL)
    return pl.pallas_call(
        _peer_kernel,
        out_shape=jax.ShapeDtypeStruct((t, d), F32),
        grid=(t // tb,),
        in_specs=[pl.BlockSpec(memory_space=pl.ANY),
                  pl.BlockSpec((tb, d), lambda i: (i, 0)),
                  pl.BlockSpec((tb, PEER_SEL), lambda i: (i, 0)),
                  pl.BlockSpec(memory_space=pl.ANY)],
        out_specs=pl.BlockSpec((tb, d), lambda i: (i, 0)),
        scratch_shapes=[pltpu.SMEM((tb * PEER_SEL,), jnp.int32),
                        pltpu.VMEM((PEER_NG, PEER_NB, PEER_SEL // SUBLANES, d // LANES, SUBLANES, LANES),
                                   jnp.int32),
                        pltpu.SemaphoreType.DMA((1,)),
                        pltpu.SemaphoreType.DMA((PEER_NG,))],
        compiler_params=_cparams(("arbitrary",)),
        name="peer",
    )(idx2, xn, g, uv)


def _ple_kernel(x1_ref, pe_ref, p_ref, nw_ref, wg_ref, wp_ref, nf_ref, y_ref):
    x2 = x1_ref[...] + pe_ref[...]
    gate = _sigmoid(_dot(_rms(x2, nw_ref[...]).astype(BF16), wg_ref[...]))
    x3 = x2 + _dot(p_ref[...].astype(BF16), wp_ref[...]) * gate
    y_ref[...] = _rms(x3, nf_ref[...])


def _ple(x1, pe, p, nw, wg, wp, nf, tm):
    t, d = x1.shape
    full = lambda shape: pl.BlockSpec(shape, lambda i: (0,) * len(shape))
    tok = lambda w: pl.BlockSpec((tm, w), lambda i: (i, 0))
    return pl.pallas_call(
        _ple_kernel,
        out_shape=jax.ShapeDtypeStruct((t, d), F32),
        grid=(t // tm,),
        in_specs=[tok(d), tok(d), tok(PLE_DIM), full(nw.shape), full(wg.shape), full(wp.shape),
                  full(nf.shape)],
        out_specs=tok(d),
        compiler_params=_cparams(("parallel",)),
        name="ple",
    )(x1, pe, p, nw, wg, wp, nf)


def _rope_tables(s):
    half = ROT_DIM // 2
    inv = ROPE_THETA ** (-jnp.arange(0, ROT_DIM, 2, dtype=F32) / ROT_DIM)
    ang = jnp.arange(s, dtype=F32)[:, None] * inv[None, :]
    cos, sin = jnp.cos(ang), jnp.sin(ang)
    ones = jnp.ones((s, ATT_QK_DIM - ROT_DIM), F32)
    zeros = jnp.zeros((s, ATT_QK_DIM - ROT_DIM), F32)
    zh = jnp.zeros((s, half), F32)
    c64 = jnp.concatenate([cos, cos, ones], axis=1)
    s1_64 = jnp.concatenate([zh, sin, zeros], axis=1)
    s2_64 = jnp.concatenate([-sin, zh, zeros], axis=1)
    rep = LANES // ATT_QK_DIM
    return jnp.tile(c64, (1, rep)), jnp.tile(s1_64, (1, rep)), jnp.tile(s2_64, (1, rep))


def _pack_uv(u, v):
    ub = lax.bitcast_convert_type(u.astype(BF16), jnp.uint16).astype(jnp.uint32)
    vb = lax.bitcast_convert_type(v.astype(BF16), jnp.uint16).astype(jnp.uint32)
    words = lax.bitcast_convert_type((ub << 16) | vb, jnp.int32)
    return words.reshape(u.shape[0], D_MODEL // LANES, LANES)


def _trunk(x, p, w):
    b, s, d = x.shape
    t = b * s
    nc = s // M_CHUNK
    rc, rs1, rs2 = _rope_tables(s)
    aq, ak, av, mq, mk, mv, mo, gc, gr = _inproj(
        x, w["norm_mix"], w["wqk"], w["wav"], w["wmqk"], w["wmvo"], w["wg"], w["wgt"], w["gb"],
        w["gbt"], w["conv"], rc, rs1, rs2, tm=512)
    att = _attn(w["lq"], aq, ak, av, w["attn_gain"], tq=128)
    gc5 = gc.reshape(b, nc, M_CHUNK, M_HEADS, 4).transpose(0, 3, 1, 2, 4)
    gr5 = gr.reshape(b, M_HEADS, 4, nc, M_CHUNK).transpose(0, 1, 3, 2, 4)
    mem = _mlstm(mq, mk, mv, mo, gc5, gr5, w["mlstm_gain"])
    x1, xn, qp = _outproj(x.reshape(t, d), att.reshape(t, ATT_W), mem.reshape(t, M_W),
                          w["wo_a"], w["wo_m"], w["norm_ffn"], w["wq"], tm=512)
    idx, g = _route(qp, w["sk"], tm=128)
    pe = _peer(idx, xn, g, w["uv"])
    y = _ple(x1, pe, p.reshape(t, PLE_DIM), w["norm_ple"], w["ple_wg"], w["ple_wp"],
             w["norm_final"], tm=512)
    return y.reshape(b, s, d)


def kernel(x_prompt, x_sample, p_prompt, p_sample, norm_mix, w_in, gate_bias, conv_qk, lambda_qk,
           attn_gain, mlstm_gain, w_out, norm_ffn, peer_wq, peer_subkeys, peer_u, peer_v, norm_ple,
           ple_w_gate, ple_w_proj, norm_final):
    li = 0
    win = w_in[li]
    c0 = 0
    cols = {}
    for name, width in (("aq", ATT_W), ("ak", ATT_W), ("av", ATT_W), ("mq", M_W), ("mk", M_W),
                        ("mv", M_W), ("mo", M_W), ("gt", N_GATES)):
        cols[name] = win[:, c0:c0 + width]
        c0 += width
    perm = jnp.array([ty * M_HEADS + h for h in range(M_HEADS) for ty in range(4)], jnp.int32)
    wg = cols["gt"][:, perm].astype(BF16)
    gb = gate_bias[li][perm]
    w = {
        "norm_mix": norm_mix[li][None, :],
        "wqk": jnp.concatenate([cols["aq"], cols["ak"]], axis=1).astype(BF16),
        "wav": cols["av"].astype(BF16),
        "wmqk": jnp.concatenate([cols["mq"], cols["mk"]], axis=1).astype(BF16),
        "wmvo": jnp.concatenate([cols["mv"], cols["mo"]], axis=1).astype(BF16),
        "wg": wg,
        "wgt": wg.T,
        "gb": gb[None, :],
        "gbt": gb[:, None],
        "conv": conv_qk[li],
        "lq": lambda_qk[li],
        "attn_gain": attn_gain[li][None, :],
        "mlstm_gain": mlstm_gain[li][None, :],
        "wo_a": w_out[li][:ATT_W].astype(BF16),
        "wo_m": w_out[li][ATT_W:].astype(BF16),
        "norm_ffn": norm_ffn[li][None, :],
        "wq": peer_wq[li].astype(BF16),
        "sk": peer_subkeys[li].reshape(PEER_HEADS * 2, N_KEYS, PEER_HALF).astype(BF16),
        "uv": _pack_uv(peer_u[li], peer_v[li]),
        "norm_ple": norm_ple[li][None, :],
        "ple_wg": ple_w_gate[li].astype(BF16),
        "ple_wp": ple_w_proj[li].astype(BF16),
        "norm_final": norm_final[None, :],
    }
    y_prompt = _trunk(x_prompt, p_prompt[li], w)
    y_sample = _trunk(x_sample, p_sample[li], w)
    return (y_prompt, y_sample)
```

```python
import functools
import math

import jax
import jax.numpy as jnp
from jax import lax
from jax.experimental import pallas as pl
from jax.experimental.pallas import tpu as pltpu
from jax.experimental.pallas import tpu_sc as plsc

F32 = jnp.float32
BF16 = jnp.bfloat16

D_MODEL = 1024
ATT_W = 512
ATT_HEADS = 4
ATT_V_DIM = 128
ATT_QK_DIM = 64
ROT_DIM = 16
ROPE_THETA = 500000.0
M_W = 512
M_HEADS = 4
M_HEAD_DIM = 128
M_CHUNK = 128
CONV_K = 5
N_GATES = 16
PEER_HEADS = 8
PEER_QDIM = 256
PEER_HALF = 128
N_KEYS = 128
PEER_TOPK = 16
PEER_SEL = PEER_HEADS * PEER_TOPK
PLE_DIM = 256
EPS = 1e-6
LAM_INIT = 0.8 - 0.6 * math.exp(-0.3 * 0)

LANES = 128
SUBLANES = 8
VMEM_LIMIT = 56 * 1024 * 1024

NEG_INF = float("-inf")


def _cparams(sem):
    return pltpu.CompilerParams(dimension_semantics=sem, vmem_limit_bytes=VMEM_LIMIT)


def _dot(a, b):
    return jnp.dot(a, b, preferred_element_type=F32)


def _dot_nt(a, b):
    return lax.dot_general(a, b, (((1,), (1,)), ((), ())), preferred_element_type=F32)


def _rms(x, w):
    ms = jnp.mean(x * x, axis=-1, keepdims=True)
    return (x * lax.rsqrt(ms + EPS)) * w


def _sigmoid(x):
    return 1.0 / (1.0 + jnp.exp(-x))


def _log_sigmoid(x):
    return jnp.minimum(x, 0.0) - jnp.log(1.0 + jnp.exp(-jnp.abs(x)))


def _inproj_kernel(x_ref, xp_ref, xn_ref, nw_ref, wqk_ref, wav_ref, wmqk_ref, wmvo_ref,
                   wg_ref, wgt_ref, gb_ref, gbt_ref, cw_ref, rc_ref, rs1_ref, rs2_ref,
                   aq_ref, ak_ref, av_ref, mq_ref, mk_ref, mv_ref, mo_ref, gc_ref, gr_ref):
    i = pl.program_id(1)
    last = pl.num_programs(1) - 1
    nw = nw_ref[...]
    tm = x_ref.shape[1]

    h = _rms(x_ref[0], nw).astype(BF16)
    hp = _rms(xp_ref[0, 0], nw).astype(BF16)
    hn = _rms(xn_ref[0, 0], nw).astype(BF16)

    qk = _dot(h, wqk_ref[...])
    rc = rc_ref[...]
    rs1 = rs1_ref[...]
    rs2 = rs2_ref[...]
    half = ROT_DIM // 2
    for c in range(2 * ATT_W // LANES):
        xc = qk[:, c * LANES:(c + 1) * LANES]
        r = xc * rc + pltpu.roll(xc, half, 1) * rs1 + pltpu.roll(xc, LANES - half, 1) * rs2
        if c < ATT_W // LANES:
            aq_ref[0, :, c * LANES:(c + 1) * LANES] = (r * (ATT_QK_DIM ** -0.5)).astype(BF16)
        else:
            cc = c - ATT_W // LANES
            ak_ref[0, :, cc * LANES:(cc + 1) * LANES] = r.astype(BF16)

    av_ref[0] = _dot(h, wav_ref[...]).astype(BF16)

    wm = wmqk_ref[...]
    pm = _dot(h, wm)
    pp = _dot(hp, wm) * (i > 0).astype(F32)
    pn = _dot(hn, wm) * (i < last).astype(F32)
    ext = jnp.concatenate([pp, pm, pn], axis=0)
    cw = cw_ref[...]
    off = SUBLANES - CONV_K // 2
    conv = cw[0:1, :] * ext[off:off + tm, :]
    for j in range(1, CONV_K):
        conv = conv + cw[j:j + 1, :] * ext[off + j:off + j + tm, :]
    act = conv * _sigmoid(conv)
    mq_ref[0] = act[:, :M_W].astype(BF16)
    mk_ref[0] = (act[:, M_W:] * (M_HEAD_DIM ** -0.5)).astype(BF16)

    vo = _dot(h, wmvo_ref[...])
    mv_ref[0] = vo[:, :M_W].astype(BF16)
    mo_ref[0] = vo[:, M_W:].astype(BF16)

    gc_ref[0] = _dot(h, wg_ref[...]) + gb_ref[...]
    gr_ref[0] = _dot_nt(wgt_ref[...], h) + gbt_ref[...]


def _inproj(x, norm_w, wqk, wav, wmqk, wmvo, wg, wgt, gb, gbt, conv_w, rc, rs1, rs2, tm):
    b, s, d = x.shape
    nt = s // tm
    x8 = x.reshape(b, s // SUBLANES, SUBLANES, d)
    r8 = tm // SUBLANES
    nb8 = s // SUBLANES
    full = lambda shape: pl.BlockSpec(shape, lambda bi, i: (0,) * len(shape))
    tok = lambda w: pl.BlockSpec((1, tm, w), lambda bi, i: (bi, i, 0))
    out_shapes = (
        [jax.ShapeDtypeStruct((b, s, ATT_W), BF16)] * 3
        + [jax.ShapeDtypeStruct((b, s, M_W), BF16)] * 4
        + [jax.ShapeDtypeStruct((b, s, N_GATES), F32), jax.ShapeDtypeStruct((b, N_GATES, s), F32)]
    )
    return pl.pallas_call(
        _inproj_kernel,
        out_shape=out_shapes,
        grid=(b, nt),
        in_specs=[
            tok(d),
            pl.BlockSpec((1, 1, SUBLANES, d), lambda bi, i: (bi, jnp.maximum(i * r8 - 1, 0), 0, 0)),
            pl.BlockSpec((1, 1, SUBLANES, d), lambda bi, i: (bi, jnp.minimum((i + 1) * r8, nb8 - 1), 0, 0)),
            full((1, d)),
            full(wqk.shape), full(wav.shape), full(wmqk.shape), full(wmvo.shape),
            full(wg.shape), full(wgt.shape), full(gb.shape), full(gbt.shape), full(conv_w.shape),
            pl.BlockSpec((tm, LANES), lambda bi, i: (i, 0)),
            pl.BlockSpec((tm, LANES), lambda bi, i: (i, 0)),
            pl.BlockSpec((tm, LANES), lambda bi, i: (i, 0)),
        ],
        out_specs=[tok(ATT_W)] * 3 + [tok(M_W)] * 4
        + [tok(N_GATES), pl.BlockSpec((1, N_GATES, tm), lambda bi, i: (bi, 0, i))],
        compiler_params=_cparams(("parallel", "parallel")),
        name="inproj",
    )(x, x8, x8, norm_w, wqk, wav, wmqk, wmvo, wg, wgt, gb, gbt, conv_w, rc, rs1, rs2)


def _attn_kernel(lq_ref, q_ref, k_ref, v_ref, gain_ref, o_ref):
    lq = lq_ref[...]
    lam = (jnp.exp(jnp.sum(lq[0:1] * lq[1:2], axis=1, keepdims=True))
           - jnp.exp(jnp.sum(lq[2:3] * lq[3:4], axis=1, keepdims=True)) + LAM_INIT)
    q = q_ref[0]
    k = k_ref[0]
    v = v_ref[0]
    lane = lax.broadcasted_iota(jnp.int32, q.shape, 1)
    zero = jnp.zeros_like(q)

    def soft(qh):
        sc = _dot_nt(qh, k)
        m = jnp.max(sc, axis=1, keepdims=True)
        e = jnp.exp(sc - m)
        return e, jnp.sum(e, axis=1, keepdims=True)

    e0, l0 = soft(jnp.where(lane < ATT_QK_DIM, q, zero))
    e1, l1 = soft(jnp.where(lane >= ATT_QK_DIM, q, zero))
    a = e0 * (1.0 / l0) - e1 * (lam / l1)
    o = _dot(a.astype(BF16), v)
    o_ref[0] = (_rms(o, gain_ref[...]) * (1.0 - LAM_INIT)).astype(BF16)


def _attn(lq, q, k, v, gain, tq):
    b, s, _ = q.shape
    return pl.pallas_call(
        _attn_kernel,
        out_shape=jax.ShapeDtypeStruct((b, s, ATT_W), BF16),
        grid=(b, ATT_HEADS, s // tq),
        in_specs=[
            pl.BlockSpec(lq.shape, lambda bi, h, i: (0, 0)),
            pl.BlockSpec((1, tq, ATT_V_DIM), lambda bi, h, i: (bi, i, h)),
            pl.BlockSpec((1, s, ATT_V_DIM), lambda bi, h, i: (bi, 0, h)),
            pl.BlockSpec((1, s, ATT_V_DIM), lambda bi, h, i: (bi, 0, h)),
            pl.BlockSpec((1, ATT_V_DIM), lambda bi, h, i: (0, 0)),
        ],
        out_specs=pl.BlockSpec((1, tq, ATT_V_DIM), lambda bi, h, i: (bi, i, h)),
        compiler_params=_cparams(("parallel", "parallel", "parallel")),
        name="attn",
    )(lq, q, k, v, gain)


def _mlstm_chunk(q, k, v, ic_col, lf_col, ic_row, lf_row, cst, n, m, fwd):
    ln = q.shape[0]
    row = lax.broadcasted_iota(jnp.int32, (ln, ln), 0)
    col = lax.broadcasted_iota(jnp.int32, (ln, ln), 1)
    lower = col <= row
    upper = col >= row
    mask, mask_t = (lower, upper) if fwd else (upper, lower)
    bcum_col = jnp.sum(jnp.where(mask, lf_row, 0.0), axis=1, keepdims=True)
    bcum_row = jnp.sum(jnp.where(mask_t, lf_col, 0.0), axis=0, keepdims=True)
    dlog = jnp.where(mask, bcum_col - bcum_row + ic_row, NEG_INF)
    inter = bcum_col + m
    m_t = jnp.maximum(inter, jnp.max(dlog, axis=1, keepdims=True))
    w = jnp.exp(dlog - m_t)
    sc = jnp.exp(inter - m_t)
    wqk = w * _dot_nt(q, k)
    num = sc * _dot(q, cst.astype(BF16)) + _dot(wqk.astype(BF16), v)
    qn = jnp.sum(q.astype(F32) * n, axis=1, keepdims=True)
    den = sc * qn + jnp.sum(wqk, axis=1, keepdims=True)
    h = num / jnp.maximum(jnp.abs(den), jnp.exp(-m_t))
    b_last = jnp.sum(lf_row, axis=1, keepdims=True)
    rel_row = b_last - bcum_row + ic_row
    rel_col = b_last - bcum_col + ic_col
    m_new = jnp.maximum(b_last + m, jnp.max(rel_row, axis=1, keepdims=True))
    dec_col = jnp.exp(rel_col - m_new)
    keep = jnp.exp(b_last + m - m_new)
    kd = k.astype(F32) * dec_col
    c_new = keep * cst + _dot(kd.T.astype(BF16), v)
    n_new = keep * n + jnp.sum(kd, axis=0, keepdims=True)
    return h, c_new, n_new, m_new


def _mlstm_kernel(q_ref, k_ref, v_ref, o_ref, gc_ref, gr_ref, gain_ref, out_ref, hf_ref):
    s = q_ref.shape[1]
    nc = s // M_CHUNK
    d = M_HEAD_DIM
    gain = gain_ref[...]

    def step(c, carry, fwd):
        cst, n, m = carry
        sl = pl.ds(pl.multiple_of(c * M_CHUNK, M_CHUNK), M_CHUNK)
        gc = gc_ref[0, 0, c]
        gr = gr_ref[0, 0, c]
        t = 0 if fwd else 2
        h, cst, n, m = _mlstm_chunk(
            q_ref[0, sl, :], k_ref[0, sl, :], v_ref[0, sl, :],
            gc[:, t:t + 1], _log_sigmoid(gc[:, t + 1:t + 2]),
            gr[t:t + 1, :], _log_sigmoid(gr[t + 1:t + 2, :]),
            cst, n, m, fwd)
        return sl, h, (cst, n, m)

    init = (jnp.zeros((d, d), F32), jnp.zeros((1, d), F32), jnp.zeros((1, 1), F32))

    def fwd_body(c, carry):
        sl, h, carry = step(c, carry, True)
        hf_ref[sl, :] = h
        return carry

    lax.fori_loop(0, nc, fwd_body, init)

    def bwd_body(j, carry):
        sl, h, carry = step(nc - 1 - j, carry, False)
        hs = hf_ref[sl, :] + h
        y = _sigmoid(o_ref[0, sl, :].astype(F32)) * hs
        out_ref[0, sl, :] = _rms(y, gain).astype(BF16)
        return carry

    lax.fori_loop(0, nc, bwd_body, init)


def _mlstm(q, k, v, o, gc, gr, gain):
    b, s, _ = q.shape
    nc = s // M_CHUNK
    head = pl.BlockSpec((1, s, M_HEAD_DIM), lambda bi, h: (bi, 0, h))
    return pl.pallas_call(
        _mlstm_kernel,
        out_shape=jax.ShapeDtypeStruct((b, s, M_W), BF16),
        grid=(b, M_HEADS),
        in_specs=[
            head, head, head, head,
            pl.BlockSpec((1, 1, nc, M_CHUNK, 4), lambda bi, h: (bi, h, 0, 0, 0)),
            pl.BlockSpec((1, 1, nc, 4, M_CHUNK), lambda bi, h: (bi, h, 0, 0, 0)),
            pl.BlockSpec((1, M_HEAD_DIM), lambda bi, h: (0, h)),
        ],
        out_specs=head,
        scratch_shapes=[pltpu.VMEM((s, M_HEAD_DIM), F32)],
        compiler_params=_cparams(("parallel", "parallel")),
        name="mlstm",
    )(q, k, v, o, gc, gr, gain)


def _outproj_kernel(x_ref, att_ref, mem_ref, wa_ref, wm_ref, nw_ref, wq_ref, x1_ref, xn_ref, qp_ref):
    x1 = x_ref[...] + _dot(att_ref[...], wa_ref[...]) + _dot(mem_ref[...], wm_ref[...])
    x1_ref[...] = x1
    xn = _rms(x1, nw_ref[...])
    xn_ref[...] = xn
    qp = _dot(xn.astype(BF16), wq_ref[...])
    for c in range(qp_ref.shape[0]):
        qp_ref[c] = qp[:, c * LANES:(c + 1) * LANES].astype(BF16)


def _outproj(x, att, mem, wa, wm, nw, wq, tm):
    t, d = x.shape
    nq = wq.shape[1] // LANES
    full = lambda shape: pl.BlockSpec(shape, lambda i: (0,) * len(shape))
    tok = lambda w: pl.BlockSpec((tm, w), lambda i: (i, 0))
    return pl.pallas_call(
        _outproj_kernel,
        out_shape=[jax.ShapeDtypeStruct((t, d), F32), jax.ShapeDtypeStruct((t, d), F32),
                   jax.ShapeDtypeStruct((nq, t, LANES), BF16)],
        grid=(t // tm,),
        in_specs=[tok(d), tok(ATT_W), tok(M_W), full(wa.shape), full(wm.shape), full(nw.shape),
                  full(wq.shape)],
        out_specs=[tok(d), tok(d), pl.BlockSpec((nq, tm, LANES), lambda i: (0, i, 0))],
        compiler_params=_cparams(("parallel",)),
        name="outproj",
    )(x, att, mem, wa, wm, nw, wq)


def _topk_rows(sc, n):
    pos = lax.broadcasted_iota(jnp.int32, sc.shape, 0)
    vals, poss = [], []
    for _ in range(PEER_TOPK):
        m = jnp.max(sc, axis=0, keepdims=True)
        p = jnp.min(jnp.where(sc == m, pos, n), axis=0, keepdims=True)
        vals.append(m)
        poss.append(p)
        sc = jnp.where(pos == p, NEG_INF, sc)
    return jnp.concatenate(vals, axis=0), jnp.concatenate(poss, axis=0)


def _route_kernel(qp_ref, sk_ref, idx_ref, g_ref, idx_t, g_t):
    def head(h, carry):
        s1 = _dot_nt(sk_ref[2 * h], qp_ref[2 * h])
        s2 = _dot_nt(sk_ref[2 * h + 1], qp_ref[2 * h + 1])
        v1, i1 = _topk_rows(s1, N_KEYS)
        v2, i2 = _topk_rows(s2, N_KEYS)
        cand = jnp.concatenate([v1[a:a + 1, :] + v2 for a in range(PEER_TOPK)], axis=0)
        sc, ci = _topk_rows(cand, PEER_TOPK * PEER_TOPK)
        ca = ci // PEER_TOPK
        cb = ci % PEER_TOPK
        e1 = jnp.zeros_like(ci)
        e2 = jnp.zeros_like(ci)
        for a in range(PEER_TOPK):
            e1 = jnp.where(ca == a, i1[a:a + 1, :], e1)
            e2 = jnp.where(cb == a, i2[a:a + 1, :], e2)
        ex = jnp.exp(sc - sc[0:1, :])
        rows = pl.ds(pl.multiple_of(h * PEER_TOPK, PEER_TOPK), PEER_TOPK)
        idx_t[rows, :] = e1 * N_KEYS + e2
        g_t[rows, :] = ex / jnp.sum(ex, axis=0, keepdims=True)
        return carry

    lax.fori_loop(0, PEER_HEADS, head, 0)
    idx_ref[...] = idx_t[...].T
    g_ref[...] = g_t[...].T


def _route(qp, sk, tm):
    nq, t, _ = qp.shape
    return pl.pallas_call(
        _route_kernel,
        out_shape=[jax.ShapeDtypeStruct((t, PEER_SEL), jnp.int32),
                   jax.ShapeDtypeStruct((t, PEER_SEL), F32)],
        grid=(t // tm,),
        in_specs=[pl.BlockSpec((nq, tm, LANES), lambda i: (0, i, 0)),
                  pl.BlockSpec(sk.shape, lambda i: (0, 0, 0))],
        out_specs=[pl.BlockSpec((tm, PEER_SEL), lambda i: (i, 0))] * 2,
        scratch_shapes=[pltpu.VMEM((PEER_SEL, tm), jnp.int32), pltpu.VMEM((PEER_SEL, tm), F32)],
        compiler_params=_cparams(("parallel",)),
        name="route",
    )(qp, sk)


PEER_TB = 256
PEER_NB = SUBLANES
PEER_NG = 4
PEER_LA = 2
HI16 = -65536


def _gelu(a):
    return 0.5 * a * (1.0 + lax.erf(a * (2.0 ** -0.5)))


def _peer_token(xr, gr, tile):
    nch = D_MODEL // LANES
    nkt = PEER_SEL // SUBLANES
    xb = [jnp.broadcast_to(xr[:, c * LANES:(c + 1) * LANES], (SUBLANES, LANES)) for c in range(nch)]
    parts = []
    for kt in range(nkt):
        acc = None
        for c in range(nch):
            u = lax.bitcast_convert_type(tile(kt, c) & HI16, F32)
            acc = u * xb[c] if acc is None else acc + u * xb[c]
        parts.append(acc)
    part = jnp.concatenate(parts, axis=0)
    a_row = jnp.sum(part.T, axis=0, keepdims=True)
    act = _gelu(a_row) * gr
    act_b = jnp.broadcast_to(act, (LANES, PEER_SEL)).T
    outs = []
    for c in range(nch):
        acc = None
        for kt in range(nkt):
            v = lax.bitcast_convert_type(tile(kt, c) << 16, F32)
            t = act_b[kt * SUBLANES:(kt + 1) * SUBLANES, :] * v
            acc = t if acc is None else acc + t
        outs.append(jnp.sum(acc, axis=0, keepdims=True))
    return jnp.concatenate(outs, axis=1)


def _peer_staged_kernel(st_ref, xn_ref, g_ref, out_ref):
    xg = xn_ref[...]
    gg = g_ref[...]
    rows = []
    for tb in range(xn_ref.shape[0]):
        def tile(kt, c, tb=tb):
            return st_ref[tb, kt * SUBLANES:(kt + 1) * SUBLANES, c * LANES:(c + 1) * LANES]
        rows.append(_peer_token(xg[tb:tb + 1, :], gg[tb:tb + 1, :], tile))
    out_ref[...] = jnp.concatenate(rows, axis=0)


def _peer_staged(staged, xn, g):
    t, d = xn.shape
    tb = PEER_NB
    return pl.pallas_call(
        _peer_staged_kernel,
        out_shape=jax.ShapeDtypeStruct((t, d), F32),
        grid=(t // tb,),
        in_specs=[pl.BlockSpec((tb, PEER_SEL, d), lambda i: (i, 0, 0)),
                  pl.BlockSpec((tb, d), lambda i: (i, 0)),
                  pl.BlockSpec((tb, PEER_SEL), lambda i: (i, 0))],
        out_specs=pl.BlockSpec((tb, d), lambda i: (i, 0)),
        compiler_params=_cparams(("parallel",)),
        name="peer_staged",
    )(staged, xn, g)


SC_CORES = 2
SC_SUBCORES = 16
SC_WORKERS = SC_CORES * SC_SUBCORES
SC_ROWS = 32
SC_IDX_BLK = 128
PEER_SC_SHARE_NUM = 1
PEER_SC_SHARE_DEN = 2
PEER_SC_QUANTUM = SC_WORKERS * SC_ROWS * SC_IDX_BLK // PEER_SEL


def _stage_rows(uv2, idx):
    n_tok, nsel = idx.shape
    n_rows = n_tok * nsel
    rows_per_w = n_rows // SC_WORKERS
    n_blk = rows_per_w // (SC_ROWS * SC_IDX_BLK)
    idx4 = idx.reshape(SC_WORKERS, n_blk, SC_IDX_BLK, SC_ROWS)
    mesh = plsc.VectorSubcoreMesh(core_axis_name="c", subcore_axis_name="s")

    @functools.partial(
        pl.kernel, mesh=mesh,
        out_type=jax.ShapeDtypeStruct((n_rows, uv2.shape[1]), uv2.dtype),
        scratch_types=[
            pltpu.VMEM((SC_IDX_BLK, SC_ROWS), jnp.int32),
            pltpu.VMEM((2, SC_ROWS, uv2.shape[1]), uv2.dtype),
            pltpu.SemaphoreType.DMA((2,)),
            pltpu.SemaphoreType.DMA((2,)),
        ],
    )
    def stage(uv_hbm, idx_hbm, out_hbm, idx_v, rows_v, gsem, wsem):
        wid = lax.axis_index("s") * SC_CORES + lax.axis_index("c")
        base = wid * rows_per_w

        def gather(j, b):
            return pltpu.make_async_copy(uv_hbm.at[idx_v.at[j]], rows_v.at[b], gsem.at[b])

        def write(row0, b):
            return pltpu.make_async_copy(rows_v.at[b], out_hbm.at[pl.ds(row0, SC_ROWS)], wsem.at[b])

        @pl.loop(0, n_blk)
        def _(blk):
            pltpu.sync_copy(idx_hbm.at[wid, blk], idx_v)
            blk_row0 = base + blk * (SC_IDX_BLK * SC_ROWS)
            gather(0, 0).start()

            @pl.loop(0, SC_IDX_BLK, step=2)
            def _(j):
                for b in range(2):
                    jj = j + b
                    row0 = pl.multiple_of(blk_row0 + jj * SC_ROWS, SC_ROWS)
                    gather(jj, b).wait()

                    @pl.when(jj > 0)
                    def _():
                        write(row0 - SC_ROWS, 1 - b).wait()

                    @pl.when(jj + 1 < SC_IDX_BLK)
                    def _():
                        gather(jj + 1, 1 - b).start()

                    write(row0, b).start()

            write(blk_row0 + (SC_IDX_BLK - 1) * SC_ROWS, 1).wait()

    return stage(uv2, idx4)


def _peer_kernel(idx_hbm, xn_ref, g_ref, uv_hbm, out_ref, idx_s, buf, sem_i, sem):
    step = pl.program_id(0)
    nbatch = xn_ref.shape[0] // PEER_NB
    nch = D_MODEL // LANES
    nkt = PEER_SEL // SUBLANES

    cp = pltpu.make_async_copy(idx_hbm.at[step], idx_s, sem_i.at[0])
    cp.start()
    cp.wait()

    def issue(b):
        grp = b & (PEER_NG - 1)

        def tok_body(tb, carry):
            base = (b * PEER_NB + tb) * PEER_SEL
            for k in range(PEER_SEL):
                e = idx_s[base + k]
                pltpu.make_async_copy(
                    uv_hbm.at[e], buf.at[grp, tb, k // SUBLANES, :, k % SUBLANES, :], sem.at[grp]
                ).start(priority=k % 2)
            return carry

        lax.fori_loop(0, PEER_NB, tok_body, 0)

    def wait(b):
        grp = b & (PEER_NG - 1)
        pltpu.make_async_copy(buf.at[grp], buf.at[grp], sem.at[grp]).wait()

    def compute(b):
        grp = b & (PEER_NG - 1)
        r0 = pl.multiple_of(b * PEER_NB, PEER_NB)
        xg = xn_ref[pl.ds(r0, PEER_NB), :]
        gg = g_ref[pl.ds(r0, PEER_NB), :]
        rows = [_peer_token(xg[tb:tb + 1, :], gg[tb:tb + 1, :],
                            functools.partial(lambda tb, kt, c: buf[grp, tb, kt, c], tb))
                for tb in range(PEER_NB)]
        out_ref[pl.ds(r0, PEER_NB), :] = jnp.concatenate(rows, axis=0)

    for b in range(PEER_LA):
        issue(b)

    def body(b, carry):
        wait(b)

        @pl.when(b + PEER_LA < nbatch)
        def _():
            issue(b + PEER_LA)

        compute(b)
        return carry

    lax.fori_loop(0, nbatch, body, 0)


def _peer(idx, xn, g, uv):
    t, d = xn.shape
    tb = PEER_TB
    idx2 = idx.reshape(t // tb, tb * PEER_SEL)
    return pl.pallas_call(
        _peer_kernel,
        out_shape=jax.ShapeDtypeStruct((t, d), F32),
        grid=(t // tb,),
        in_specs=[pl.BlockSpec(memory_space=pl.ANY),
                  pl.BlockSpec((tb, d), lambda i: (i, 0)),
                  pl.BlockSpec((tb, PEER_SEL), lambda i: (i, 0)),
                  pl.BlockSpec(memory_space=pl.ANY)],
        out_specs=pl.BlockSpec((tb, d), lambda i: (i, 0)),
        scratch_shapes=[pltpu.SMEM((tb * PEER_SEL,), jnp.int32),
                        pltpu.VMEM((PEER_NG, PEER_NB, PEER_SEL // SUBLANES, d // LANES, SUBLANES, LANES),
                                   jnp.int32),
                        pltpu.SemaphoreType.DMA((1,)),
                        pltpu.SemaphoreType.DMA((PEER_NG,))],
        compiler_params=_cparams(("arbitrary",)),
        name="peer",
    )(idx2, xn, g, uv)


def _ple_kernel(x1_ref, pe_ref, p_ref, nw_ref, wg_ref, wp_ref, nf_ref, y_ref):
    x2 = x1_ref[...] + pe_ref[...]
    gate = _sigmoid(_dot(_rms(x2, nw_ref[...]).astype(BF16), wg_ref[...]))
    x3 = x2 + _dot(p_ref[...].astype(BF16), wp_ref[...]) * gate
    y_ref[...] = _rms(x3, nf_ref[...])


def _ple(x1, pe, p, nw, wg, wp, nf, tm):
    t, d = x1.shape
    full = lambda shape: pl.BlockSpec(shape, lambda i: (0,) * len(shape))
    tok = lambda w: pl.BlockSpec((tm, w), lambda i: (i, 0))
    return pl.pallas_call(
        _ple_kernel,
        out_shape=jax.ShapeDtypeStruct((t, d), F32),
        grid=(t // tm,),
        in_specs=[tok(d), tok(d), tok(PLE_DIM), full(nw.shape), full(wg.shape), full(wp.shape),
                  full(nf.shape)],
        out_specs=tok(d),
        compiler_params=_cparams(("parallel",)),
        name="ple",
    )(x1, pe, p, nw, wg, wp, nf)


def _rope_tables(s):
    half = ROT_DIM // 2
    inv = ROPE_THETA ** (-jnp.arange(0, ROT_DIM, 2, dtype=F32) / ROT_DIM)
    ang = jnp.arange(s, dtype=F32)[:, None] * inv[None, :]
    cos, sin = jnp.cos(ang), jnp.sin(ang)
    ones = jnp.ones((s, ATT_QK_DIM - ROT_DIM), F32)
    zeros = jnp.zeros((s, ATT_QK_DIM - ROT_DIM), F32)
    zh = jnp.zeros((s, half), F32)
    c64 = jnp.concatenate([cos, cos, ones], axis=1)
    s1_64 = jnp.concatenate([zh, sin, zeros], axis=1)
    s2_64 = jnp.concatenate([-sin, zh, zeros], axis=1)
    rep = LANES // ATT_QK_DIM
    return jnp.tile(c64, (1, rep)), jnp.tile(s1_64, (1, rep)), jnp.tile(s2_64, (1, rep))


def _pack_uv(u, v):
    ub = lax.bitcast_convert_type(u.astype(BF16), jnp.uint16).astype(jnp.uint32)
    vb = lax.bitcast_convert_type(v.astype(BF16), jnp.uint16).astype(jnp.uint32)
    return lax.bitcast_convert_type((ub << 16) | vb, jnp.int32)


def _trunk(x, p, w):
    b, s, d = x.shape
    t = b * s
    nc = s // M_CHUNK
    rc, rs1, rs2 = _rope_tables(s)
    aq, ak, av, mq, mk, mv, mo, gc, gr = _inproj(
        x, w["norm_mix"], w["wqk"], w["wav"], w["wmqk"], w["wmvo"], w["wg"], w["wgt"], w["gb"],
        w["gbt"], w["conv"], rc, rs1, rs2, tm=512)
    att = _attn(w["lq"], aq, ak, av, w["attn_gain"], tq=128)
    gc5 = gc.reshape(b, nc, M_CHUNK, M_HEADS, 4).transpose(0, 3, 1, 2, 4)
    gr5 = gr.reshape(b, M_HEADS, 4, nc, M_CHUNK).transpose(0, 1, 3, 2, 4)
    mem = _mlstm(mq, mk, mv, mo, gc5, gr5, w["mlstm_gain"])
    x1, xn, qp = _outproj(x.reshape(t, d), att.reshape(t, ATT_W), mem.reshape(t, M_W),
                          w["wo_a"], w["wo_m"], w["norm_ffn"], w["wq"], tm=512)
    idx, g = _route(qp, w["sk"], tm=128)
    ts = (t * PEER_SC_SHARE_NUM // PEER_SC_SHARE_DEN) // PEER_SC_QUANTUM * PEER_SC_QUANTUM
    staged = _stage_rows(w["uv"], idx[:ts])
    pe_t = _peer(idx[ts:], xn[ts:], g[ts:], w["uv"].reshape(-1, D_MODEL // LANES, LANES))
    pe_s = _peer_staged(staged.reshape(ts, PEER_SEL, d), xn[:ts], g[:ts])
    pe = jnp.concatenate([pe_s, pe_t], axis=0)
    y = _ple(x1, pe, p.reshape(t, PLE_DIM), w["norm_ple"], w["ple_wg"], w["ple_wp"],
             w["norm_final"], tm=512)
    return y.reshape(b, s, d)


def kernel(x_prompt, x_sample, p_prompt, p_sample, norm_mix, w_in, gate_bias, conv_qk, lambda_qk,
           attn_gain, mlstm_gain, w_out, norm_ffn, peer_wq, peer_subkeys, peer_u, peer_v, norm_ple,
           ple_w_gate, ple_w_proj, norm_final):
    li = 0
    win = w_in[li]
    c0 = 0
    cols = {}
    for name, width in (("aq", ATT_W), ("ak", ATT_W), ("av", ATT_W), ("mq", M_W), ("mk", M_W),
                        ("mv", M_W), ("mo", M_W), ("gt", N_GATES)):
        cols[name] = win[:, c0:c0 + width]
        c0 += width
    perm = jnp.array([ty * M_HEADS + h for h in range(M_HEADS) for ty in range(4)], jnp.int32)
    wg = cols["gt"][:, perm].astype(BF16)
    gb = gate_bias[li][perm]
    w = {
        "norm_mix": norm_mix[li][None, :],
        "wqk": jnp.concatenate([cols["aq"], cols["ak"]], axis=1).astype(BF16),
        "wav": cols["av"].astype(BF16),
        "wmqk": jnp.concatenate([cols["mq"], cols["mk"]], axis=1).astype(BF16),
        "wmvo": jnp.concatenate([cols["mv"], cols["mo"]], axis=1).astype(BF16),
        "wg": wg,
        "wgt": wg.T,
        "gb": gb[None, :],
        "gbt": gb[:, None],
        "conv": conv_qk[li],
        "lq": lambda_qk[li],
        "attn_gain": attn_gain[li][None, :],
        "mlstm_gain": mlstm_gain[li][None, :],
        "wo_a": w_out[li][:ATT_W].astype(BF16),
        "wo_m": w_out[li][ATT_W:].astype(BF16),
        "norm_ffn": norm_ffn[li][None, :],
        "wq": peer_wq[li].astype(BF16),
        "sk": peer_subkeys[li].reshape(PEER_HEADS * 2, N_KEYS, PEER_HALF).astype(BF16),
        "uv": _pack_uv(peer_u[li], peer_v[li]),
        "norm_ple": norm_ple[li][None, :],
        "ple_wg": ple_w_gate[li].astype(BF16),
        "ple_wp": ple_w_proj[li].astype(BF16),
        "norm_final": norm_final[None, :],
    }
    y_prompt = _trunk(x_prompt, p_prompt[li], w)
    y_sample = _trunk(x_sample, p_sample[li], w)
    return (y_prompt, y_sample)
```

```python
import functools
import math

import jax
import jax.numpy as jnp
from jax import lax
from jax.experimental import pallas as pl
from jax.experimental.pallas import tpu as pltpu
from jax.experimental.pallas import tpu_sc as plsc

F32 = jnp.float32
BF16 = jnp.bfloat16

D_MODEL = 1024
ATT_W = 512
ATT_HEADS = 4
ATT_V_DIM = 128
ATT_QK_DIM = 64
ROT_DIM = 16
ROPE_THETA = 500000.0
M_W = 512
M_HEADS = 4
M_HEAD_DIM = 128
M_CHUNK = 128
CONV_K = 5
N_GATES = 16
PEER_HEADS = 8
PEER_QDIM = 256
PEER_HALF = 128
N_KEYS = 128
PEER_TOPK = 16
PEER_SEL = PEER_HEADS * PEER_TOPK
PLE_DIM = 256
EPS = 1e-6
LAM_INIT = 0.8 - 0.6 * math.exp(-0.3 * 0)
Q_SCALE = ATT_QK_DIM ** -0.5 * math.log2(math.e)

LANES = 128
SUBLANES = 8
VMEM_LIMIT = 56 * 1024 * 1024

NEG_INF = float("-inf")


def _cparams(sem):
    return pltpu.CompilerParams(dimension_semantics=sem, vmem_limit_bytes=VMEM_LIMIT)


def _dot(a, b):
    return jnp.dot(a, b, preferred_element_type=F32)


def _dot_nt(a, b):
    return lax.dot_general(a, b, (((1,), (1,)), ((), ())), preferred_element_type=F32)


def _rms(x, w):
    ms = jnp.mean(x * x, axis=-1, keepdims=True)
    return (x * lax.rsqrt(ms + EPS)) * w


def _sigmoid(x):
    return 1.0 / (1.0 + jnp.exp(-x))


def _log_sigmoid(x):
    return jnp.minimum(x, 0.0) - jnp.log(1.0 + jnp.exp(-jnp.abs(x)))


def _inproj_kernel(x_ref, xp_ref, xn_ref, nw_ref, wqk_ref, wav_ref, wmqk_ref, wmvo_ref,
                   wg_ref, wgt_ref, gb_ref, gbt_ref, cw_ref, rc_ref, rs1_ref, rs2_ref,
                   aq_ref, ak_ref, av_ref, mq_ref, mk_ref, mv_ref, mo_ref, gc_ref, gr_ref):
    i = pl.program_id(1)
    last = pl.num_programs(1) - 1
    nw = nw_ref[...]
    tm = x_ref.shape[1]

    h = _rms(x_ref[0], nw).astype(BF16)
    hp = _rms(xp_ref[0, 0], nw).astype(BF16)
    hn = _rms(xn_ref[0, 0], nw).astype(BF16)

    qk = _dot(h, wqk_ref[...])
    rc = rc_ref[...]
    rs1 = rs1_ref[...]
    rs2 = rs2_ref[...]
    half = ROT_DIM // 2
    for c in range(2 * ATT_W // LANES):
        xc = qk[:, c * LANES:(c + 1) * LANES]
        r = xc * rc + pltpu.roll(xc, half, 1) * rs1 + pltpu.roll(xc, LANES - half, 1) * rs2
        if c < ATT_W // LANES:
            aq_ref[0, :, c * LANES:(c + 1) * LANES] = (r * Q_SCALE).astype(BF16)
        else:
            cc = c - ATT_W // LANES
            ak_ref[0, :, cc * LANES:(cc + 1) * LANES] = r.astype(BF16)

    av_ref[0] = _dot(h, wav_ref[...]).astype(BF16)

    wm = wmqk_ref[...]
    pm = _dot(h, wm)
    pp = _dot(hp, wm) * (i > 0).astype(F32)
    pn = _dot(hn, wm) * (i < last).astype(F32)
    ext = jnp.concatenate([pp, pm, pn], axis=0)
    cw = cw_ref[...]
    off = SUBLANES - CONV_K // 2
    conv = cw[0:1, :] * ext[off:off + tm, :]
    for j in range(1, CONV_K):
        conv = conv + cw[j:j + 1, :] * ext[off + j:off + j + tm, :]
    act = conv * _sigmoid(conv)
    mq_ref[0] = act[:, :M_W].astype(BF16)
    mk_ref[0] = (act[:, M_W:] * (M_HEAD_DIM ** -0.5)).astype(BF16)

    vo = _dot(h, wmvo_ref[...])
    mv_ref[0] = vo[:, :M_W].astype(BF16)
    mo_ref[0] = vo[:, M_W:].astype(BF16)

    gc_ref[0] = _dot(h, wg_ref[...]) + gb_ref[...]
    gr_ref[0] = _dot_nt(wgt_ref[...], h) + gbt_ref[...]


def _inproj(x, norm_w, wqk, wav, wmqk, wmvo, wg, wgt, gb, gbt, conv_w, rc, rs1, rs2, tm):
    b, s, d = x.shape
    nt = s // tm
    x8 = x.reshape(b, s // SUBLANES, SUBLANES, d)
    r8 = tm // SUBLANES
    nb8 = s // SUBLANES
    full = lambda shape: pl.BlockSpec(shape, lambda bi, i: (0,) * len(shape))
    tok = lambda w: pl.BlockSpec((1, tm, w), lambda bi, i: (bi, i, 0))
    out_shapes = (
        [jax.ShapeDtypeStruct((b, s, ATT_W), BF16)] * 3
        + [jax.ShapeDtypeStruct((b, s, M_W), BF16)] * 4
        + [jax.ShapeDtypeStruct((b, s, N_GATES), F32), jax.ShapeDtypeStruct((b, N_GATES, s), F32)]
    )
    return pl.pallas_call(
        _inproj_kernel,
        out_shape=out_shapes,
        grid=(b, nt),
        in_specs=[
            tok(d),
            pl.BlockSpec((1, 1, SUBLANES, d), lambda bi, i: (bi, jnp.maximum(i * r8 - 1, 0), 0, 0)),
            pl.BlockSpec((1, 1, SUBLANES, d), lambda bi, i: (bi, jnp.minimum((i + 1) * r8, nb8 - 1), 0, 0)),
            full((1, d)),
            full(wqk.shape), full(wav.shape), full(wmqk.shape), full(wmvo.shape),
            full(wg.shape), full(wgt.shape), full(gb.shape), full(gbt.shape), full(conv_w.shape),
            pl.BlockSpec((tm, LANES), lambda bi, i: (i, 0)),
            pl.BlockSpec((tm, LANES), lambda bi, i: (i, 0)),
            pl.BlockSpec((tm, LANES), lambda bi, i: (i, 0)),
        ],
        out_specs=[tok(ATT_W)] * 3 + [tok(M_W)] * 4
        + [tok(N_GATES), pl.BlockSpec((1, N_GATES, tm), lambda bi, i: (bi, 0, i))],
        compiler_params=_cparams(("parallel", "parallel")),
        name="inproj",
    )(x, x8, x8, norm_w, wqk, wav, wmqk, wmvo, wg, wgt, gb, gbt, conv_w, rc, rs1, rs2)


def _attn_kernel(lq_ref, q_ref, k_ref, v_ref, gain_ref, o_ref):
    lq = lq_ref[...]
    lam = (jnp.exp(jnp.sum(lq[0:1] * lq[1:2], axis=1, keepdims=True))
           - jnp.exp(jnp.sum(lq[2:3] * lq[3:4], axis=1, keepdims=True)) + LAM_INIT)
    q = q_ref[0]
    k = k_ref[0]
    v = v_ref[0]
    lane = lax.broadcasted_iota(jnp.int32, q.shape, 1)
    zero = jnp.zeros_like(q)

    def soft(qh):
        sc = _dot_nt(qh, k)
        m = jnp.max(sc, axis=1, keepdims=True)
        e = jnp.exp2(sc - m)
        return _dot(e.astype(BF16), v), jnp.sum(e, axis=1, keepdims=True)

    o0, l0 = soft(jnp.where(lane < ATT_QK_DIM, q, zero))
    o1, l1 = soft(jnp.where(lane >= ATT_QK_DIM, q, zero))
    o = o0 * (1.0 / l0) - o1 * (lam / l1)
    o_ref[0] = (_rms(o, gain_ref[...]) * (1.0 - LAM_INIT)).astype(BF16)


def _attn(lq, q, k, v, gain, tq):
    b, s, _ = q.shape
    return pl.pallas_call(
        _attn_kernel,
        out_shape=jax.ShapeDtypeStruct((b, s, ATT_W), BF16),
        grid=(b, ATT_HEADS, s // tq),
        in_specs=[
            pl.BlockSpec(lq.shape, lambda bi, h, i: (0, 0)),
            pl.BlockSpec((1, tq, ATT_V_DIM), lambda bi, h, i: (bi, i, h)),
            pl.BlockSpec((1, s, ATT_V_DIM), lambda bi, h, i: (bi, 0, h)),
            pl.BlockSpec((1, s, ATT_V_DIM), lambda bi, h, i: (bi, 0, h)),
            pl.BlockSpec((1, ATT_V_DIM), lambda bi, h, i: (0, 0)),
        ],
        out_specs=pl.BlockSpec((1, tq, ATT_V_DIM), lambda bi, h, i: (bi, i, h)),
        compiler_params=_cparams(("parallel", "parallel", "parallel")),
        name="attn",
    )(lq, q, k, v, gain)


def _mlstm_chunk(q, k, v, ic_col, lf_col, ic_row, lf_row, cst, n, m, fwd):
    ln = q.shape[0]
    row = lax.broadcasted_iota(jnp.int32, (ln, ln), 0)
    col = lax.broadcasted_iota(jnp.int32, (ln, ln), 1)
    lower = col <= row
    upper = col >= row
    mask, mask_t = (lower, upper) if fwd else (upper, lower)
    bcum_col = jnp.sum(jnp.where(mask, lf_row, 0.0), axis=1, keepdims=True)
    bcum_row = jnp.sum(jnp.where(mask_t, lf_col, 0.0), axis=0, keepdims=True)
    dlog = jnp.where(mask, bcum_col - bcum_row + ic_row, NEG_INF)
    inter = bcum_col + m
    m_t = jnp.maximum(inter, jnp.max(dlog, axis=1, keepdims=True))
    w = jnp.exp(dlog - m_t)
    sc = jnp.exp(inter - m_t)
    wqk = w * _dot_nt(q, k)
    num = sc * _dot(q, cst.astype(BF16)) + _dot(wqk.astype(BF16), v)
    qn = jnp.sum(q.astype(F32) * n, axis=1, keepdims=True)
    den = sc * qn + jnp.sum(wqk, axis=1, keepdims=True)
    h = num / jnp.maximum(jnp.abs(den), jnp.exp(-m_t))
    b_last = jnp.sum(lf_row, axis=1, keepdims=True)
    rel_row = b_last - bcum_row + ic_row
    rel_col = b_last - bcum_col + ic_col
    m_new = jnp.maximum(b_last + m, jnp.max(rel_row, axis=1, keepdims=True))
    dec_col = jnp.exp(rel_col - m_new)
    keep = jnp.exp(b_last + m - m_new)
    kd = k.astype(F32) * dec_col
    c_new = keep * cst + _dot(kd.T.astype(BF16), v)
    n_new = keep * n + jnp.sum(kd, axis=0, keepdims=True)
    return h, c_new, n_new, m_new


def _mlstm_kernel(q_ref, k_ref, v_ref, o_ref, gc_ref, gr_ref, gain_ref, out_ref, hf_ref):
    s = q_ref.shape[1]
    nc = s // M_CHUNK
    d = M_HEAD_DIM
    gain = gain_ref[...]

    def step(c, carry, fwd):
        cst, n, m = carry
        sl = pl.ds(pl.multiple_of(c * M_CHUNK, M_CHUNK), M_CHUNK)
        gc = gc_ref[0, 0, c]
        gr = gr_ref[0, 0, c]
        t = 0 if fwd else 2
        h, cst, n, m = _mlstm_chunk(
            q_ref[0, sl, :], k_ref[0, sl, :], v_ref[0, sl, :],
            gc[:, t:t + 1], _log_sigmoid(gc[:, t + 1:t + 2]),
            gr[t:t + 1, :], _log_sigmoid(gr[t + 1:t + 2, :]),
            cst, n, m, fwd)
        return sl, h, (cst, n, m)

    init = (jnp.zeros((d, d), F32), jnp.zeros((1, d), F32), jnp.zeros((1, 1), F32))

    def fwd_body(c, carry):
        sl, h, carry = step(c, carry, True)
        hf_ref[sl, :] = h
        return carry

    lax.fori_loop(0, nc, fwd_body, init)

    def bwd_body(j, carry):
        sl, h, carry = step(nc - 1 - j, carry, False)
        hs = hf_ref[sl, :] + h
        y = _sigmoid(o_ref[0, sl, :].astype(F32)) * hs
        out_ref[0, sl, :] = _rms(y, gain).astype(BF16)
        return carry

    lax.fori_loop(0, nc, bwd_body, init)


def _mlstm(q, k, v, o, gc, gr, gain):
    b, s, _ = q.shape
    nc = s // M_CHUNK
    head = pl.BlockSpec((1, s, M_HEAD_DIM), lambda bi, h: (bi, 0, h))
    return pl.pallas_call(
        _mlstm_kernel,
        out_shape=jax.ShapeDtypeStruct((b, s, M_W), BF16),
        grid=(b, M_HEADS),
        in_specs=[
            head, head, head, head,
            pl.BlockSpec((1, 1, nc, M_CHUNK, 4), lambda bi, h: (bi, h, 0, 0, 0)),
            pl.BlockSpec((1, 1, nc, 4, M_CHUNK), lambda bi, h: (bi, h, 0, 0, 0)),
            pl.BlockSpec((1, M_HEAD_DIM), lambda bi, h: (0, h)),
        ],
        out_specs=head,
        scratch_shapes=[pltpu.VMEM((s, M_HEAD_DIM), F32)],
        compiler_params=_cparams(("parallel", "parallel")),
        name="mlstm",
    )(q, k, v, o, gc, gr, gain)


def _outproj_kernel(x_ref, att_ref, mem_ref, wa_ref, wm_ref, nw_ref, wq_ref, x1_ref, xn_ref, qp_ref):
    x1 = x_ref[...] + _dot(att_ref[...], wa_ref[...]) + _dot(mem_ref[...], wm_ref[...])
    x1_ref[...] = x1
    xn = _rms(x1, nw_ref[...])
    xn_ref[...] = xn
    qp = _dot(xn.astype(BF16), wq_ref[...])
    for c in range(qp_ref.shape[0]):
        qp_ref[c] = qp[:, c * LANES:(c + 1) * LANES].astype(BF16)


def _outproj(x, att, mem, wa, wm, nw, wq, tm):
    t, d = x.shape
    nq = wq.shape[1] // LANES
    full = lambda shape: pl.BlockSpec(shape, lambda i: (0,) * len(shape))
    tok = lambda w: pl.BlockSpec((tm, w), lambda i: (i, 0))
    return pl.pallas_call(
        _outproj_kernel,
        out_shape=[jax.ShapeDtypeStruct((t, d), F32), jax.ShapeDtypeStruct((t, d), F32),
                   jax.ShapeDtypeStruct((nq, t, LANES), BF16)],
        grid=(t // tm,),
        in_specs=[tok(d), tok(ATT_W), tok(M_W), full(wa.shape), full(wm.shape), full(nw.shape),
                  full(wq.shape)],
        out_specs=[tok(d), tok(d), pl.BlockSpec((nq, tm, LANES), lambda i: (0, i, 0))],
        compiler_params=_cparams(("parallel",)),
        name="outproj",
    )(x, att, mem, wa, wm, nw, wq)


def _topk_rows(sc, n):
    pos = lax.broadcasted_iota(jnp.int32, sc.shape, 0).astype(F32)
    vals, poss = [], []
    for _ in range(PEER_TOPK):
        m = jnp.max(sc, axis=0, keepdims=True)
        p = jnp.min(jnp.where(sc == m, pos, float(n)), axis=0, keepdims=True)
        vals.append(m)
        poss.append(p)
        sc = jnp.where(pos == p, NEG_INF, sc)
    return jnp.concatenate(vals, axis=0), jnp.concatenate(poss, axis=0).astype(jnp.int32)


def _route_kernel(qp_ref, sk_ref, idx_ref, g_ref, idx_t, g_t):
    def head(h, carry):
        s1 = _dot_nt(sk_ref[2 * h], qp_ref[2 * h])
        s2 = _dot_nt(sk_ref[2 * h + 1], qp_ref[2 * h + 1])
        v1, i1 = _topk_rows(s1, N_KEYS)
        v2, i2 = _topk_rows(s2, N_KEYS)
        cand = jnp.concatenate([v1[a:a + 1, :] + v2 for a in range(PEER_TOPK)], axis=0)
        sc, ci = _topk_rows(cand, PEER_TOPK * PEER_TOPK)
        ca = lax.shift_right_logical(ci, int(math.log2(PEER_TOPK)))
        cb = ci & (PEER_TOPK - 1)
        e1 = jnp.zeros_like(ci)
        e2 = jnp.zeros_like(ci)
        for a in range(PEER_TOPK):
            e1 = jnp.where(ca == a, i1[a:a + 1, :], e1)
            e2 = jnp.where(cb == a, i2[a:a + 1, :], e2)
        ex = jnp.exp(sc - sc[0:1, :])
        rows = pl.ds(pl.multiple_of(h * PEER_TOPK, PEER_TOPK), PEER_TOPK)
        idx_t[rows, :] = e1 * N_KEYS + e2
        g_t[rows, :] = ex / jnp.sum(ex, axis=0, keepdims=True)
        return carry

    lax.fori_loop(0, PEER_HEADS, head, 0)
    idx_ref[...] = idx_t[...].T
    g_ref[...] = g_t[...].T


def _route(qp, sk, tm):
    nq, t, _ = qp.shape
    return pl.pallas_call(
        _route_kernel,
        out_shape=[jax.ShapeDtypeStruct((t, PEER_SEL), jnp.int32),
                   jax.ShapeDtypeStruct((t, PEER_SEL), F32)],
        grid=(t // tm,),
        in_specs=[pl.BlockSpec((nq, tm, LANES), lambda i: (0, i, 0)),
                  pl.BlockSpec(sk.shape, lambda i: (0, 0, 0))],
        out_specs=[pl.BlockSpec((tm, PEER_SEL), lambda i: (i, 0))] * 2,
        scratch_shapes=[pltpu.VMEM((PEER_SEL, tm), jnp.int32), pltpu.VMEM((PEER_SEL, tm), F32)],
        compiler_params=_cparams(("parallel",)),
        name="route",
    )(qp, sk)


PEER_TB = 256
PEER_NB = SUBLANES
PEER_NG = 4
PEER_LA = 2
HI16 = -65536


def _gelu(a):
    return 0.5 * a * (1.0 + lax.erf(a * (2.0 ** -0.5)))


def _peer_token(xr, gr, tile):
    nch = D_MODEL // LANES
    nkt = PEER_SEL // SUBLANES
    xb = [jnp.broadcast_to(xr[:, c * LANES:(c + 1) * LANES], (SUBLANES, LANES)) for c in range(nch)]
    parts = []
    for kt in range(nkt):
        acc = None
        for c in range(nch):
            u = lax.bitcast_convert_type(tile(kt, c) & HI16, F32)
            acc = u * xb[c] if acc is None else acc + u * xb[c]
        parts.append(acc)
    part = jnp.concatenate(parts, axis=0)
    a_row = jnp.sum(part.T, axis=0, keepdims=True)
    act = _gelu(a_row) * gr
    act_b = jnp.broadcast_to(act, (LANES, PEER_SEL)).T
    outs = []
    for c in range(nch):
        acc = None
        for kt in range(nkt):
            v = lax.bitcast_convert_type(tile(kt, c) << 16, F32)
            t = act_b[kt * SUBLANES:(kt + 1) * SUBLANES, :] * v
            acc = t if acc is None else acc + t
        outs.append(jnp.sum(acc, axis=0, keepdims=True))
    return jnp.concatenate(outs, axis=1)


def _peer_staged_kernel(st_ref, xn_ref, g_ref, rest_ref, out_ref):
    del rest_ref
    xg = xn_ref[...]
    gg = g_ref[...]
    rows = []
    for tb in range(xn_ref.shape[0]):
        def tile(kt, c, tb=tb):
            return st_ref[tb, kt * SUBLANES:(kt + 1) * SUBLANES, c * LANES:(c + 1) * LANES]
        rows.append(_peer_token(xg[tb:tb + 1, :], gg[tb:tb + 1, :], tile))
    out_ref[...] = jnp.concatenate(rows, axis=0)


def _peer_staged(staged, xn, g, rest):
    ts = staged.shape[0]
    t, d = xn.shape
    tb = PEER_NB
    return pl.pallas_call(
        _peer_staged_kernel,
        out_shape=jax.ShapeDtypeStruct((t, d), F32),
        grid=(ts // tb,),
        in_specs=[pl.BlockSpec((tb, PEER_SEL, d), lambda i: (i, 0, 0)),
                  pl.BlockSpec((tb, d), lambda i: (i, 0)),
                  pl.BlockSpec((tb, PEER_SEL), lambda i: (i, 0)),
                  pl.BlockSpec(memory_space=pl.ANY)],
        out_specs=pl.BlockSpec((tb, d), lambda i: (i, 0)),
        input_output_aliases={3: 0},
        compiler_params=_cparams(("parallel",)),
        name="peer_staged",
    )(staged, xn, g, rest)


SC_CORES = 2
SC_SUBCORES = 16
SC_WORKERS = SC_CORES * SC_SUBCORES
SC_ROWS = 32
SC_IDX_BLK = 128
PEER_SC_SHARE_NUM = 5
PEER_SC_SHARE_DEN = 8
PEER_SC_QUANTUM = SC_WORKERS * SC_ROWS * SC_IDX_BLK // PEER_SEL


def _stage_rows(uv2, idx):
    n_tok, nsel = idx.shape
    n_rows = n_tok * nsel
    rows_per_w = n_rows // SC_WORKERS
    n_blk = rows_per_w // (SC_ROWS * SC_IDX_BLK)
    idx4 = idx.reshape(SC_WORKERS, n_blk, SC_IDX_BLK, SC_ROWS)
    mesh = plsc.VectorSubcoreMesh(core_axis_name="c", subcore_axis_name="s")

    @functools.partial(
        pl.kernel, mesh=mesh,
        out_type=jax.ShapeDtypeStruct((n_rows, uv2.shape[1]), uv2.dtype),
        scratch_types=[
            pltpu.VMEM((SC_IDX_BLK, SC_ROWS), jnp.int32),
            pltpu.VMEM((2, SC_ROWS, uv2.shape[1]), uv2.dtype),
            pltpu.SemaphoreType.DMA((2,)),
            pltpu.SemaphoreType.DMA((2,)),
        ],
    )
    def stage(uv_hbm, idx_hbm, out_hbm, idx_v, rows_v, gsem, wsem):
        wid = lax.axis_index("s") * SC_CORES + lax.axis_index("c")
        base = wid * rows_per_w

        def gather(j, b):
            return pltpu.make_async_copy(uv_hbm.at[idx_v.at[j]], rows_v.at[b], gsem.at[b])

        def write(row0, b):
            return pltpu.make_async_copy(rows_v.at[b], out_hbm.at[pl.ds(row0, SC_ROWS)], wsem.at[b])

        @pl.loop(0, n_blk)
        def _(blk):
            pltpu.sync_copy(idx_hbm.at[wid, blk], idx_v)
            blk_row0 = base + blk * (SC_IDX_BLK * SC_ROWS)
            gather(0, 0).start()

            @pl.loop(0, SC_IDX_BLK, step=2)
            def _(j):
                for b in range(2):
                    jj = j + b
                    row0 = pl.multiple_of(blk_row0 + jj * SC_ROWS, SC_ROWS)
                    gather(jj, b).wait()

                    @pl.when(jj > 0)
                    def _():
                        write(row0 - SC_ROWS, 1 - b).wait()

                    @pl.when(jj + 1 < SC_IDX_BLK)
                    def _():
                        gather(jj + 1, 1 - b).start()

                    write(row0, b).start()

            write(blk_row0 + (SC_IDX_BLK - 1) * SC_ROWS, 1).wait()

    return stage(uv2, idx4)


def _peer_kernel(step0, idx_hbm, xn_ref, g_ref, uv_hbm, out_ref, idx_s, buf, sem_i, sem):
    step = pl.program_id(0) + step0
    nbatch = xn_ref.shape[0] // PEER_NB
    nch = D_MODEL // LANES
    nkt = PEER_SEL // SUBLANES

    cp = pltpu.make_async_copy(idx_hbm.at[step], idx_s, sem_i.at[0])
    cp.start()
    cp.wait()

    def issue(b):
        grp = b & (PEER_NG - 1)

        def tok_body(tb, carry):
            base = (b * PEER_NB + tb) * PEER_SEL
            for k in range(PEER_SEL):
                e = idx_s[base + k]
                pltpu.make_async_copy(
                    uv_hbm.at[e], buf.at[grp, tb, k // SUBLANES, :, k % SUBLANES, :], sem.at[grp]
                ).start(priority=k % 2)
            return carry

        lax.fori_loop(0, PEER_NB, tok_body, 0)

    def wait(b):
        grp = b & (PEER_NG - 1)
        pltpu.make_async_copy(buf.at[grp], buf.at[grp], sem.at[grp]).wait()

    def compute(b):
        grp = b & (PEER_NG - 1)
        r0 = pl.multiple_of(b * PEER_NB, PEER_NB)
        xg = xn_ref[pl.ds(r0, PEER_NB), :]
        gg = g_ref[pl.ds(r0, PEER_NB), :]
        rows = [_peer_token(xg[tb:tb + 1, :], gg[tb:tb + 1, :],
                            functools.partial(lambda tb, kt, c: buf[grp, tb, kt, c], tb))
                for tb in range(PEER_NB)]
        out_ref[pl.ds(r0, PEER_NB), :] = jnp.concatenate(rows, axis=0)

    for b in range(PEER_LA):
        issue(b)

    def body(b, carry):
        wait(b)

        @pl.when(b + PEER_LA < nbatch)
        def _():
            issue(b + PEER_LA)

        compute(b)
        return carry

    lax.fori_loop(0, nbatch, body, 0)


def _peer(idx, xn, g, uv, ts):
    t, d = xn.shape
    tb = PEER_TB
    idx2 = idx.reshape(t // tb, tb * PEER_SEL)
    s0 = ts // tb
    return pl.pallas_call(
        functools.partial(_peer_kernel, s0),
        out_shape=jax.ShapeDtypeStruct((t, d), F32),
        grid=((t - ts) // tb,),
        in_specs=[pl.BlockSpec(memory_space=pl.ANY),
                  pl.BlockSpec((tb, d), lambda i: (i + s0, 0)),
                  pl.BlockSpec((tb, PEER_SEL), lambda i: (i + s0, 0)),
                  pl.BlockSpec(memory_space=pl.ANY)],
        out_specs=pl.BlockSpec((tb, d), lambda i: (i + s0, 0)),
        scratch_shapes=[pltpu.SMEM((tb * PEER_SEL,), jnp.int32),
                        pltpu.VMEM((PEER_NG, PEER_NB, PEER_SEL // SUBLANES, d // LANES, SUBLANES, LANES),
                                   jnp.int32),
                        pltpu.SemaphoreType.DMA((1,)),
                        pltpu.SemaphoreType.DMA((PEER_NG,))],
        compiler_params=_cparams(("arbitrary",)),
        name="peer",
    )(idx2, xn, g, uv)


def _ple_kernel(x1_ref, pe_ref, p_ref, nw_ref, wg_ref, wp_ref, nf_ref, y_ref):
    x2 = x1_ref[...] + pe_ref[...]
    gate = _sigmoid(_dot(_rms(x2, nw_ref[...]).astype(BF16), wg_ref[...]))
    x3 = x2 + _dot(p_ref[...].astype(BF16), wp_ref[...]) * gate
    y_ref[...] = _rms(x3, nf_ref[...])


def _ple(x1, pe, p, nw, wg, wp, nf, tm):
    t, d = x1.shape
    full = lambda shape: pl.BlockSpec(shape, lambda i: (0,) * len(shape))
    tok = lambda w: pl.BlockSpec((tm, w), lambda i: (i, 0))
    return pl.pallas_call(
        _ple_kernel,
        out_shape=jax.ShapeDtypeStruct((t, d), F32),
        grid=(t // tm,),
        in_specs=[tok(d), tok(d), tok(PLE_DIM), full(nw.shape), full(wg.shape), full(wp.shape),
                  full(nf.shape)],
        out_specs=tok(d),
        compiler_params=_cparams(("parallel",)),
        name="ple",
    )(x1, pe, p, nw, wg, wp, nf)


def _rope_tables(s):
    half = ROT_DIM // 2
    inv = ROPE_THETA ** (-jnp.arange(0, ROT_DIM, 2, dtype=F32) / ROT_DIM)
    ang = jnp.arange(s, dtype=F32)[:, None] * inv[None, :]
    cos, sin = jnp.cos(ang), jnp.sin(ang)
    ones = jnp.ones((s, ATT_QK_DIM - ROT_DIM), F32)
    zeros = jnp.zeros((s, ATT_QK_DIM - ROT_DIM), F32)
    zh = jnp.zeros((s, half), F32)
    c64 = jnp.concatenate([cos, cos, ones], axis=1)
    s1_64 = jnp.concatenate([zh, sin, zeros], axis=1)
    s2_64 = jnp.concatenate([-sin, zh, zeros], axis=1)
    rep = LANES // ATT_QK_DIM
    return jnp.tile(c64, (1, rep)), jnp.tile(s1_64, (1, rep)), jnp.tile(s2_64, (1, rep))


def _pack_uv(u, v):
    ub = lax.bitcast_convert_type(u.astype(BF16), jnp.uint16).astype(jnp.uint32)
    vb = lax.bitcast_convert_type(v.astype(BF16), jnp.uint16).astype(jnp.uint32)
    return lax.bitcast_convert_type((ub << 16) | vb, jnp.int32)


def _trunk(x, p, w):
    b, s, d = x.shape
    t = b * s
    nc = s // M_CHUNK
    rc, rs1, rs2 = _rope_tables(s)
    aq, ak, av, mq, mk, mv, mo, gc, gr = _inproj(
        x, w["norm_mix"], w["wqk"], w["wav"], w["wmqk"], w["wmvo"], w["wg"], w["wgt"], w["gb"],
        w["gbt"], w["conv"], rc, rs1, rs2, tm=512)
    att = _attn(w["lq"], aq, ak, av, w["attn_gain"], tq=256)
    gc5 = gc.reshape(b, nc, M_CHUNK, M_HEADS, 4).transpose(0, 3, 1, 2, 4)
    gr5 = gr.reshape(b, M_HEADS, 4, nc, M_CHUNK).transpose(0, 1, 3, 2, 4)
    mem = _mlstm(mq, mk, mv, mo, gc5, gr5, w["mlstm_gain"])
    x1, xn, qp = _outproj(x.reshape(t, d), att.reshape(t, ATT_W), mem.reshape(t, M_W),
                          w["wo_a"], w["wo_m"], w["norm_ffn"], w["wq"], tm=512)
    idx, g = _route(qp, w["sk"], tm=128)
    ts = (t * PEER_SC_SHARE_NUM // PEER_SC_SHARE_DEN) // PEER_SC_QUANTUM * PEER_SC_QUANTUM
    staged = _stage_rows(w["uv"], idx[:ts])
    pe = _peer(idx, xn, g, w["uv"].reshape(-1, D_MODEL // LANES, LANES), ts)
    pe = _peer_staged(staged.reshape(ts, PEER_SEL, d), xn, g, pe)
    y = _ple(x1, pe, p.reshape(t, PLE_DIM), w["norm_ple"], w["ple_wg"], w["ple_wp"],
             w["norm_final"], tm=512)
    return y.reshape(b, s, d)


def kernel(x_prompt, x_sample, p_prompt, p_sample, norm_mix, w_in, gate_bias, conv_qk, lambda_qk,
           attn_gain, mlstm_gain, w_out, norm_ffn, peer_wq, peer_subkeys, peer_u, peer_v, norm_ple,
           ple_w_gate, ple_w_proj, norm_final):
    li = 0
    win = w_in[li]
    c0 = 0
    cols = {}
    for name, width in (("aq", ATT_W), ("ak", ATT_W), ("av", ATT_W), ("mq", M_W), ("mk", M_W),
                        ("mv", M_W), ("mo", M_W), ("gt", N_GATES)):
        cols[name] = win[:, c0:c0 + width]
        c0 += width
    perm = jnp.array([ty * M_HEADS + h for h in range(M_HEADS) for ty in range(4)], jnp.int32)
    wg = cols["gt"][:, perm].astype(BF16)
    gb = gate_bias[li][perm]
    w = {
        "norm_mix": norm_mix[li][None, :],
        "wqk": jnp.concatenate([cols["aq"], cols["ak"]], axis=1).astype(BF16),
        "wav": cols["av"].astype(BF16),
        "wmqk": jnp.concatenate([cols["mq"], cols["mk"]], axis=1).astype(BF16),
        "wmvo": jnp.concatenate([cols["mv"], cols["mo"]], axis=1).astype(BF16),
        "wg": wg,
        "wgt": wg.T,
        "gb": gb[None, :],
        "gbt": gb[:, None],
        "conv": conv_qk[li],
        "lq": lambda_qk[li],
        "attn_gain": attn_gain[li][None, :],
        "mlstm_gain": mlstm_gain[li][None, :],
        "wo_a": w_out[li][:ATT_W].astype(BF16),
        "wo_m": w_out[li][ATT_W:].astype(BF16),
        "norm_ffn": norm_ffn[li][None, :],
        "wq": peer_wq[li].astype(BF16),
        "sk": peer_subkeys[li].reshape(PEER_HEADS * 2, N_KEYS, PEER_HALF).astype(BF16),
        "uv": _pack_uv(peer_u[li], peer_v[li]),
        "norm_ple": norm_ple[li][None, :],
        "ple_wg": ple_w_gate[li].astype(BF16),
        "ple_wp": ple_w_proj[li].astype(BF16),
        "norm_final": norm_final[None, :],
    }
    y_prompt = _trunk(x_prompt, p_prompt[li], w)
    y_sample = _trunk(x_sample, p_sample[li], w)
    return (y_prompt, y_sample)
```

```python
import functools
import math

import jax
import jax.numpy as jnp
from jax import lax
from jax.experimental import pallas as pl
from jax.experimental.pallas import tpu as pltpu
from jax.experimental.pallas import tpu_sc as plsc

F32 = jnp.float32
BF16 = jnp.bfloat16

D_MODEL = 1024
ATT_W = 512
ATT_HEADS = 4
ATT_V_DIM = 128
ATT_QK_DIM = 64
ROT_DIM = 16
ROPE_THETA = 500000.0
M_W = 512
M_HEADS = 4
M_HEAD_DIM = 128
M_CHUNK = 128
CONV_K = 5
N_GATES = 16
PEER_HEADS = 8
PEER_QDIM = 256
PEER_HALF = 128
N_KEYS = 128
PEER_TOPK = 16
PEER_SEL = PEER_HEADS * PEER_TOPK
PLE_DIM = 256
EPS = 1e-6
LAM_INIT = 0.8 - 0.6 * math.exp(-0.3 * 0)
Q_SCALE = ATT_QK_DIM ** -0.5 * math.log2(math.e)

LANES = 128
SUBLANES = 8
VMEM_LIMIT = 56 * 1024 * 1024

NEG_INF = float("-inf")


def _cparams(sem):
    return pltpu.CompilerParams(dimension_semantics=sem, vmem_limit_bytes=VMEM_LIMIT)


def _dot(a, b):
    return jnp.dot(a, b, preferred_element_type=F32)


def _dot_nt(a, b):
    return lax.dot_general(a, b, (((1,), (1,)), ((), ())), preferred_element_type=F32)


def _rms(x, w):
    ms = jnp.mean(x * x, axis=-1, keepdims=True)
    return (x * lax.rsqrt(ms + EPS)) * w


def _sigmoid(x):
    return 1.0 / (1.0 + jnp.exp(-x))


def _log_sigmoid(x):
    return jnp.minimum(x, 0.0) - jnp.log(1.0 + jnp.exp(-jnp.abs(x)))


def _inproj_kernel(x_ref, xp_ref, xn_ref, nw_ref, wqk_ref, wav_ref, wmqk_ref, wmvo_ref,
                   wg_ref, wgt_ref, gb_ref, gbt_ref, cw_ref, rc_ref, rs1_ref, rs2_ref,
                   aq_ref, ak_ref, av_ref, mq_ref, mk_ref, mv_ref, mo_ref, gc_ref, gr_ref):
    i = pl.program_id(1)
    last = pl.num_programs(1) - 1
    nw = nw_ref[...]
    tm = x_ref.shape[1]

    h = _rms(x_ref[0], nw).astype(BF16)
    hp = _rms(xp_ref[0, 0], nw).astype(BF16)
    hn = _rms(xn_ref[0, 0], nw).astype(BF16)

    qk = _dot(h, wqk_ref[...])
    rc = rc_ref[...]
    rs1 = rs1_ref[...]
    rs2 = rs2_ref[...]
    half = ROT_DIM // 2
    for c in range(2 * ATT_W // LANES):
        xc = qk[:, c * LANES:(c + 1) * LANES]
        r = xc * rc + pltpu.roll(xc, half, 1) * rs1 + pltpu.roll(xc, LANES - half, 1) * rs2
        if c < ATT_W // LANES:
            aq_ref[0, :, c * LANES:(c + 1) * LANES] = (r * Q_SCALE).astype(BF16)
        else:
            cc = c - ATT_W // LANES
            ak_ref[0, :, cc * LANES:(cc + 1) * LANES] = r.astype(BF16)

    av_ref[0] = _dot(h, wav_ref[...]).astype(BF16)

    wm = wmqk_ref[...]
    pm = _dot(h, wm)
    pp = _dot(hp, wm) * (i > 0).astype(F32)
    pn = _dot(hn, wm) * (i < last).astype(F32)
    ext = jnp.concatenate([pp, pm, pn], axis=0)
    cw = cw_ref[...]
    off = SUBLANES - CONV_K // 2
    conv = cw[0:1, :] * ext[off:off + tm, :]
    for j in range(1, CONV_K):
        conv = conv + cw[j:j + 1, :] * ext[off + j:off + j + tm, :]
    act = conv * _sigmoid(conv)
    mq_ref[0] = act[:, :M_W].astype(BF16)
    mk_ref[0] = (act[:, M_W:] * (M_HEAD_DIM ** -0.5)).astype(BF16)

    vo = _dot(h, wmvo_ref[...])
    mv_ref[0] = vo[:, :M_W].astype(BF16)
    mo_ref[0] = vo[:, M_W:].astype(BF16)

    gc_ref[0] = _dot(h, wg_ref[...]) + gb_ref[...]
    gr_ref[0] = _dot_nt(wgt_ref[...], h) + gbt_ref[...]


def _inproj(x, norm_w, wqk, wav, wmqk, wmvo, wg, wgt, gb, gbt, conv_w, rc, rs1, rs2, tm):
    b, s, d = x.shape
    nt = s // tm
    x8 = x.reshape(b, s // SUBLANES, SUBLANES, d)
    r8 = tm // SUBLANES
    nb8 = s // SUBLANES
    full = lambda shape: pl.BlockSpec(shape, lambda bi, i: (0,) * len(shape))
    tok = lambda w: pl.BlockSpec((1, tm, w), lambda bi, i: (bi, i, 0))
    out_shapes = (
        [jax.ShapeDtypeStruct((b, s, ATT_W), BF16)] * 3
        + [jax.ShapeDtypeStruct((b, s, M_W), BF16)] * 4
        + [jax.ShapeDtypeStruct((b, s, N_GATES), F32), jax.ShapeDtypeStruct((b, N_GATES, s), F32)]
    )
    return pl.pallas_call(
        _inproj_kernel,
        out_shape=out_shapes,
        grid=(b, nt),
        in_specs=[
            tok(d),
            pl.BlockSpec((1, 1, SUBLANES, d), lambda bi, i: (bi, jnp.maximum(i * r8 - 1, 0), 0, 0)),
            pl.BlockSpec((1, 1, SUBLANES, d), lambda bi, i: (bi, jnp.minimum((i + 1) * r8, nb8 - 1), 0, 0)),
            full((1, d)),
            full(wqk.shape), full(wav.shape), full(wmqk.shape), full(wmvo.shape),
            full(wg.shape), full(wgt.shape), full(gb.shape), full(gbt.shape), full(conv_w.shape),
            pl.BlockSpec((tm, LANES), lambda bi, i: (i, 0)),
            pl.BlockSpec((tm, LANES), lambda bi, i: (i, 0)),
            pl.BlockSpec((tm, LANES), lambda bi, i: (i, 0)),
        ],
        out_specs=[tok(ATT_W)] * 3 + [tok(M_W)] * 4
        + [tok(N_GATES), pl.BlockSpec((1, N_GATES, tm), lambda bi, i: (bi, 0, i))],
        compiler_params=_cparams(("parallel", "parallel")),
        name="inproj",
    )(x, x8, x8, norm_w, wqk, wav, wmqk, wmvo, wg, wgt, gb, gbt, conv_w, rc, rs1, rs2)


def _attn_kernel(lq_ref, q_ref, k_ref, v_ref, gain_ref, o_ref):
    lq = lq_ref[...]
    lam = (jnp.exp(jnp.sum(lq[0:1] * lq[1:2], axis=1, keepdims=True))
           - jnp.exp(jnp.sum(lq[2:3] * lq[3:4], axis=1, keepdims=True)) + LAM_INIT)
    q = q_ref[0]
    k = k_ref[0]
    v = v_ref[0]
    lane = lax.broadcasted_iota(jnp.int32, q.shape, 1)
    zero = jnp.zeros_like(q)

    def soft(qh):
        sc = _dot_nt(qh, k)
        m = jnp.max(sc, axis=1, keepdims=True)
        e = jnp.exp2(sc - m)
        return _dot(e.astype(BF16), v), jnp.sum(e, axis=1, keepdims=True)

    o0, l0 = soft(jnp.where(lane < ATT_QK_DIM, q, zero))
    o1, l1 = soft(jnp.where(lane >= ATT_QK_DIM, q, zero))
    o = o0 * (1.0 / l0) - o1 * (lam / l1)
    o_ref[0] = (_rms(o, gain_ref[...]) * (1.0 - LAM_INIT)).astype(BF16)


def _attn(lq, q, k, v, gain, tq):
    b, s, _ = q.shape
    return pl.pallas_call(
        _attn_kernel,
        out_shape=jax.ShapeDtypeStruct((b, s, ATT_W), BF16),
        grid=(b, ATT_HEADS, s // tq),
        in_specs=[
            pl.BlockSpec(lq.shape, lambda bi, h, i: (0, 0)),
            pl.BlockSpec((1, tq, ATT_V_DIM), lambda bi, h, i: (bi, i, h)),
            pl.BlockSpec((1, s, ATT_V_DIM), lambda bi, h, i: (bi, 0, h)),
            pl.BlockSpec((1, s, ATT_V_DIM), lambda bi, h, i: (bi, 0, h)),
            pl.BlockSpec((1, ATT_V_DIM), lambda bi, h, i: (0, 0)),
        ],
        out_specs=pl.BlockSpec((1, tq, ATT_V_DIM), lambda bi, h, i: (bi, i, h)),
        compiler_params=_cparams(("parallel", "parallel", "parallel")),
        name="attn",
    )(lq, q, k, v, gain)


def _mlstm_chunk(q, k, v, ic_col, lf_col, ic_row, lf_row, cst, n, m, fwd):
    ln = q.shape[0]
    row = lax.broadcasted_iota(jnp.int32, (ln, ln), 0)
    col = lax.broadcasted_iota(jnp.int32, (ln, ln), 1)
    lower = col <= row
    upper = col >= row
    mask, mask_t = (lower, upper) if fwd else (upper, lower)
    bcum_col = jnp.sum(jnp.where(mask, lf_row, 0.0), axis=1, keepdims=True)
    bcum_row = jnp.sum(jnp.where(mask_t, lf_col, 0.0), axis=0, keepdims=True)
    dlog = jnp.where(mask, bcum_col - bcum_row + ic_row, NEG_INF)
    inter = bcum_col + m
    m_t = jnp.maximum(inter, jnp.max(dlog, axis=1, keepdims=True))
    w = jnp.exp(dlog - m_t)
    sc = jnp.exp(inter - m_t)
    wqk = w * _dot_nt(q, k)
    num = sc * _dot(q, cst.astype(BF16)) + _dot(wqk.astype(BF16), v)
    qn = jnp.sum(q.astype(F32) * n, axis=1, keepdims=True)
    den = sc * qn + jnp.sum(wqk, axis=1, keepdims=True)
    h = num / jnp.maximum(jnp.abs(den), jnp.exp(-m_t))
    b_last = jnp.sum(lf_row, axis=1, keepdims=True)
    rel_row = b_last - bcum_row + ic_row
    rel_col = b_last - bcum_col + ic_col
    m_new = jnp.maximum(b_last + m, jnp.max(rel_row, axis=1, keepdims=True))
    dec_col = jnp.exp(rel_col - m_new)
    keep = jnp.exp(b_last + m - m_new)
    kd = k.astype(F32) * dec_col
    c_new = keep * cst + _dot(kd.T.astype(BF16), v)
    n_new = keep * n + jnp.sum(kd, axis=0, keepdims=True)
    return h, c_new, n_new, m_new


def _mlstm_kernel(q_ref, k_ref, v_ref, o_ref, gc_ref, gr_ref, gain_ref, out_ref, hf_ref):
    s = q_ref.shape[1]
    nc = s // M_CHUNK
    d = M_HEAD_DIM
    gain = gain_ref[...]

    def step(c, carry, fwd):
        cst, n, m = carry
        sl = pl.ds(pl.multiple_of(c * M_CHUNK, M_CHUNK), M_CHUNK)
        gc = gc_ref[0, 0, c]
        gr = gr_ref[0, 0, c]
        t = 0 if fwd else 2
        h, cst, n, m = _mlstm_chunk(
            q_ref[0, sl, :], k_ref[0, sl, :], v_ref[0, sl, :],
            gc[:, t:t + 1], _log_sigmoid(gc[:, t + 1:t + 2]),
            gr[t:t + 1, :], _log_sigmoid(gr[t + 1:t + 2, :]),
            cst, n, m, fwd)
        return sl, h, (cst, n, m)

    init = (jnp.zeros((d, d), F32), jnp.zeros((1, d), F32), jnp.zeros((1, 1), F32))

    def fwd_body(c, carry):
        sl, h, carry = step(c, carry, True)
        hf_ref[sl, :] = h
        return carry

    lax.fori_loop(0, nc, fwd_body, init)

    def bwd_body(j, carry):
        sl, h, carry = step(nc - 1 - j, carry, False)
        hs = hf_ref[sl, :] + h
        y = _sigmoid(o_ref[0, sl, :].astype(F32)) * hs
        out_ref[0, sl, :] = _rms(y, gain).astype(BF16)
        return carry

    lax.fori_loop(0, nc, bwd_body, init)


def _mlstm(q, k, v, o, gc, gr, gain):
    b, s, _ = q.shape
    nc = s // M_CHUNK
    head = pl.BlockSpec((1, s, M_HEAD_DIM), lambda bi, h: (bi, 0, h))
    return pl.pallas_call(
        _mlstm_kernel,
        out_shape=jax.ShapeDtypeStruct((b, s, M_W), BF16),
        grid=(b, M_HEADS),
        in_specs=[
            head, head, head, head,
            pl.BlockSpec((1, 1, nc, M_CHUNK, 4), lambda bi, h: (bi, h, 0, 0, 0)),
            pl.BlockSpec((1, 1, nc, 4, M_CHUNK), lambda bi, h: (bi, h, 0, 0, 0)),
            pl.BlockSpec((1, M_HEAD_DIM), lambda bi, h: (0, h)),
        ],
        out_specs=head,
        scratch_shapes=[pltpu.VMEM((s, M_HEAD_DIM), F32)],
        compiler_params=_cparams(("parallel", "parallel")),
        name="mlstm",
    )(q, k, v, o, gc, gr, gain)


def _outproj_kernel(x_ref, att_ref, mem_ref, wa_ref, wm_ref, nw_ref, wq_ref, x1_ref, xn_ref, qp_ref):
    x1 = x_ref[...] + _dot(att_ref[...], wa_ref[...]) + _dot(mem_ref[...], wm_ref[...])
    x1_ref[...] = x1
    xn = _rms(x1, nw_ref[...])
    xn_ref[...] = xn
    qp = _dot(xn.astype(BF16), wq_ref[...])
    for c in range(qp_ref.shape[0]):
        qp_ref[c] = qp[:, c * LANES:(c + 1) * LANES].astype(BF16)


def _outproj(x, att, mem, wa, wm, nw, wq, tm):
    t, d = x.shape
    nq = wq.shape[1] // LANES
    full = lambda shape: pl.BlockSpec(shape, lambda i: (0,) * len(shape))
    tok = lambda w: pl.BlockSpec((tm, w), lambda i: (i, 0))
    return pl.pallas_call(
        _outproj_kernel,
        out_shape=[jax.ShapeDtypeStruct((t, d), F32), jax.ShapeDtypeStruct((t, d), F32),
                   jax.ShapeDtypeStruct((nq, t, LANES), BF16)],
        grid=(t // tm,),
        in_specs=[tok(d), tok(ATT_W), tok(M_W), full(wa.shape), full(wm.shape), full(nw.shape),
                  full(wq.shape)],
        out_specs=[tok(d), tok(d), pl.BlockSpec((nq, tm, LANES), lambda i: (0, i, 0))],
        compiler_params=_cparams(("parallel",)),
        name="outproj",
    )(x, att, mem, wa, wm, nw, wq)


def _topk_rows(sc, n):
    pos = lax.broadcasted_iota(jnp.int32, sc.shape, 0).astype(F32)
    vals, poss = [], []
    for _ in range(PEER_TOPK):
        m = jnp.max(sc, axis=0, keepdims=True)
        p = jnp.min(jnp.where(sc == m, pos, float(n)), axis=0, keepdims=True)
        vals.append(m)
        poss.append(p)
        sc = jnp.where(pos == p, NEG_INF, sc)
    return jnp.concatenate(vals, axis=0), jnp.concatenate(poss, axis=0).astype(jnp.int32)


def _route_kernel(qp_ref, sk_ref, idx_ref, g_ref, idx_t, g_t):
    def head(h, carry):
        s1 = _dot_nt(sk_ref[2 * h], qp_ref[2 * h])
        s2 = _dot_nt(sk_ref[2 * h + 1], qp_ref[2 * h + 1])
        v1, i1 = _topk_rows(s1, N_KEYS)
        v2, i2 = _topk_rows(s2, N_KEYS)
        k = PEER_TOPK
        sub = SUBLANES
        brow = lax.broadcasted_iota(jnp.int32, (sub, v2.shape[1]), 0)
        slabs = [v1[0:1, :] + v2]
        for a in range(1, sub):
            slabs.append(jnp.where(brow < k // (a + 1), v1[a:a + 1, :] + v2[0:sub, :], NEG_INF))
        slabs.append(v1[sub:k, :] + v2[0:1, :])
        cand = jnp.concatenate(slabs, axis=0)
        sc, cp = _topk_rows(cand, cand.shape[0])
        mid = cp < k + sub * (sub - 1)
        ca = jnp.where(cp < k, 0, jnp.where(mid, lax.shift_right_logical(cp, 3) - 1, cp - (k + sub * (sub - 1)) + sub))
        cb = jnp.where(cp < k, cp, jnp.where(mid, cp & (sub - 1), 0))
        e1 = jnp.zeros_like(cp)
        e2 = jnp.zeros_like(cp)
        for a in range(PEER_TOPK):
            e1 = jnp.where(ca == a, i1[a:a + 1, :], e1)
            e2 = jnp.where(cb == a, i2[a:a + 1, :], e2)
        ex = jnp.exp(sc - sc[0:1, :])
        rows = pl.ds(pl.multiple_of(h * PEER_TOPK, PEER_TOPK), PEER_TOPK)
        idx_t[rows, :] = e1 * N_KEYS + e2
        g_t[rows, :] = ex / jnp.sum(ex, axis=0, keepdims=True)
        return carry

    lax.fori_loop(0, PEER_HEADS, head, 0)
    idx_ref[...] = idx_t[...].T
    g_ref[...] = g_t[...].T


def _route(qp, sk, tm):
    nq, t, _ = qp.shape
    return pl.pallas_call(
        _route_kernel,
        out_shape=[jax.ShapeDtypeStruct((t, PEER_SEL), jnp.int32),
                   jax.ShapeDtypeStruct((t, PEER_SEL), F32)],
        grid=(t // tm,),
        in_specs=[pl.BlockSpec((nq, tm, LANES), lambda i: (0, i, 0)),
                  pl.BlockSpec(sk.shape, lambda i: (0, 0, 0))],
        out_specs=[pl.BlockSpec((tm, PEER_SEL), lambda i: (i, 0))] * 2,
        scratch_shapes=[pltpu.VMEM((PEER_SEL, tm), jnp.int32), pltpu.VMEM((PEER_SEL, tm), F32)],
        compiler_params=_cparams(("parallel",)),
        name="route",
    )(qp, sk)


PEER_TB = 256
PEER_NB = SUBLANES
PEER_NG = 4
PEER_LA = 2
HI16 = -65536


def _gelu(a):
    return 0.5 * a * (1.0 + lax.erf(a * (2.0 ** -0.5)))


def _peer_token(xr, gr, tile):
    nch = D_MODEL // LANES
    nkt = PEER_SEL // SUBLANES
    xb = [jnp.broadcast_to(xr[:, c * LANES:(c + 1) * LANES], (SUBLANES, LANES)) for c in range(nch)]
    parts = []
    for kt in range(nkt):
        acc = None
        for c in range(nch):
            u = lax.bitcast_convert_type(tile(kt, c) & HI16, F32)
            acc = u * xb[c] if acc is None else acc + u * xb[c]
        parts.append(acc)
    part = jnp.concatenate(parts, axis=0)
    a_row = jnp.sum(part.T, axis=0, keepdims=True)
    act = _gelu(a_row) * gr
    act_b = jnp.broadcast_to(act, (LANES, PEER_SEL)).T
    outs = []
    for c in range(nch):
        acc = None
        for kt in range(nkt):
            v = lax.bitcast_convert_type(tile(kt, c) << 16, F32)
            t = act_b[kt * SUBLANES:(kt + 1) * SUBLANES, :] * v
            acc = t if acc is None else acc + t
        outs.append(jnp.sum(acc, axis=0, keepdims=True))
    return jnp.concatenate(outs, axis=1)


def _peer_staged_kernel(st_ref, xn_ref, g_ref, rest_ref, after_ref, out_ref):
    del rest_ref, after_ref
    xg = xn_ref[...]
    gg = g_ref[...]
    rows = []
    for tb in range(xn_ref.shape[0]):
        def tile(kt, c, tb=tb):
            return st_ref[tb, kt * SUBLANES:(kt + 1) * SUBLANES, c * LANES:(c + 1) * LANES]
        rows.append(_peer_token(xg[tb:tb + 1, :], gg[tb:tb + 1, :], tile))
    out_ref[...] = jnp.concatenate(rows, axis=0)


def _peer_staged(staged, xn, g, rest, after=None):
    ts = staged.shape[0]
    t, d = xn.shape
    tb = PEER_NB
    return pl.pallas_call(
        _peer_staged_kernel,
        out_shape=jax.ShapeDtypeStruct((t, d), F32),
        grid=(ts // tb,),
        in_specs=[pl.BlockSpec((tb, PEER_SEL, d), lambda i: (i, 0, 0)),
                  pl.BlockSpec((tb, d), lambda i: (i, 0)),
                  pl.BlockSpec((tb, PEER_SEL), lambda i: (i, 0)),
                  pl.BlockSpec(memory_space=pl.ANY),
                  pl.BlockSpec(memory_space=pl.ANY)],
        out_specs=pl.BlockSpec((tb, d), lambda i: (i, 0)),
        input_output_aliases={3: 0},
        compiler_params=_cparams(("parallel",)),
        name="peer_staged",
    )(staged, xn, g, rest, _order_after(after))


SC_CORES = 2
SC_SUBCORES = 16
SC_WORKERS = SC_CORES * SC_SUBCORES
SC_ROWS = 32
SC_IDX_BLK = 128
PEER_SC_SHARE_NUM = 11
PEER_SC_SHARE_DEN = 16
PEER_SC_QUANTUM = SC_WORKERS * SC_ROWS * SC_IDX_BLK // PEER_SEL


def _stage_rows(uv2, idx):
    n_tok, nsel = idx.shape
    n_rows = n_tok * nsel
    rows_per_w = n_rows // SC_WORKERS
    n_blk = rows_per_w // (SC_ROWS * SC_IDX_BLK)
    idx4 = idx.reshape(SC_WORKERS, n_blk, SC_IDX_BLK, SC_ROWS)
    mesh = plsc.VectorSubcoreMesh(core_axis_name="c", subcore_axis_name="s")

    @functools.partial(
        pl.kernel, mesh=mesh,
        out_type=jax.ShapeDtypeStruct((n_rows, uv2.shape[1]), uv2.dtype),
        scratch_types=[
            pltpu.VMEM((SC_IDX_BLK, SC_ROWS), jnp.int32),
            pltpu.VMEM((2, SC_ROWS, uv2.shape[1]), uv2.dtype),
            pltpu.SemaphoreType.DMA((2,)),
            pltpu.SemaphoreType.DMA((2,)),
        ],
    )
    def stage(uv_hbm, idx_hbm, out_hbm, idx_v, rows_v, gsem, wsem):
        wid = lax.axis_index("s") * SC_CORES + lax.axis_index("c")
        base = wid * rows_per_w

        def gather(j, b):
            return pltpu.make_async_copy(uv_hbm.at[idx_v.at[j]], rows_v.at[b], gsem.at[b])

        def write(row0, b):
            return pltpu.make_async_copy(rows_v.at[b], out_hbm.at[pl.ds(row0, SC_ROWS)], wsem.at[b])

        @pl.loop(0, n_blk)
        def _(blk):
            pltpu.sync_copy(idx_hbm.at[wid, blk], idx_v)
            blk_row0 = base + blk * (SC_IDX_BLK * SC_ROWS)
            gather(0, 0).start()

            @pl.loop(0, SC_IDX_BLK, step=2)
            def _(j):
                for b in range(2):
                    jj = j + b
                    row0 = pl.multiple_of(blk_row0 + jj * SC_ROWS, SC_ROWS)
                    gather(jj, b).wait()

                    @pl.when(jj > 0)
                    def _():
                        write(row0 - SC_ROWS, 1 - b).wait()

                    @pl.when(jj + 1 < SC_IDX_BLK)
                    def _():
                        gather(jj + 1, 1 - b).start()

                    write(row0, b).start()

            write(blk_row0 + (SC_IDX_BLK - 1) * SC_ROWS, 1).wait()

    return stage(uv2, idx4)


def _peer_kernel(step0, idx_hbm, xn_ref, g_ref, uv_hbm, after_ref, out_ref, idx_s, buf, sem_i, sem):
    del after_ref
    step = pl.program_id(0) + step0
    nbatch = xn_ref.shape[0] // PEER_NB
    nch = D_MODEL // LANES
    nkt = PEER_SEL // SUBLANES

    cp = pltpu.make_async_copy(idx_hbm.at[step], idx_s, sem_i.at[0])
    cp.start()
    cp.wait()

    def issue(b):
        grp = b & (PEER_NG - 1)

        def tok_body(tb, carry):
            base = (b * PEER_NB + tb) * PEER_SEL
            for k in range(PEER_SEL):
                e = idx_s[base + k]
                pltpu.make_async_copy(
                    uv_hbm.at[e], buf.at[grp, tb, k // SUBLANES, :, k % SUBLANES, :], sem.at[grp]
                ).start(priority=k % 2)
            return carry

        lax.fori_loop(0, PEER_NB, tok_body, 0)

    def wait(b):
        grp = b & (PEER_NG - 1)
        pltpu.make_async_copy(buf.at[grp], buf.at[grp], sem.at[grp]).wait()

    def compute(b):
        grp = b & (PEER_NG - 1)
        r0 = pl.multiple_of(b * PEER_NB, PEER_NB)
        xg = xn_ref[pl.ds(r0, PEER_NB), :]
        gg = g_ref[pl.ds(r0, PEER_NB), :]
        rows = [_peer_token(xg[tb:tb + 1, :], gg[tb:tb + 1, :],
                            functools.partial(lambda tb, kt, c: buf[grp, tb, kt, c], tb))
                for tb in range(PEER_NB)]
        out_ref[pl.ds(r0, PEER_NB), :] = jnp.concatenate(rows, axis=0)

    for b in range(PEER_LA):
        issue(b)

    def body(b, carry):
        wait(b)

        @pl.when(b + PEER_LA < nbatch)
        def _():
            issue(b + PEER_LA)

        compute(b)
        return carry

    lax.fori_loop(0, nbatch, body, 0)


def _order_after(after):
    return jnp.zeros((SUBLANES, LANES), F32) if after is None else after


def _peer(idx, xn, g, uv, ts, after=None):
    t, d = xn.shape
    tb = PEER_TB
    idx2 = idx.reshape(t // tb, tb * PEER_SEL)
    s0 = ts // tb
    return pl.pallas_call(
        functools.partial(_peer_kernel, s0),
        out_shape=jax.ShapeDtypeStruct((t, d), F32),
        grid=((t - ts) // tb,),
        in_specs=[pl.BlockSpec(memory_space=pl.ANY),
                  pl.BlockSpec((tb, d), lambda i: (i + s0, 0)),
                  pl.BlockSpec((tb, PEER_SEL), lambda i: (i + s0, 0)),
                  pl.BlockSpec(memory_space=pl.ANY),
                  pl.BlockSpec(memory_space=pl.ANY)],
        out_specs=pl.BlockSpec((tb, d), lambda i: (i + s0, 0)),
        scratch_shapes=[pltpu.SMEM((tb * PEER_SEL,), jnp.int32),
                        pltpu.VMEM((PEER_NG, PEER_NB, PEER_SEL // SUBLANES, d // LANES, SUBLANES, LANES),
                                   jnp.int32),
                        pltpu.SemaphoreType.DMA((1,)),
                        pltpu.SemaphoreType.DMA((PEER_NG,))],
        compiler_params=_cparams(("arbitrary",)),
        name="peer",
    )(idx2, xn, g, uv, _order_after(after))


def _ple_kernel(x1_ref, pe_ref, p_ref, nw_ref, wg_ref, wp_ref, nf_ref, y_ref):
    x2 = x1_ref[...] + pe_ref[...]
    gate = _sigmoid(_dot(_rms(x2, nw_ref[...]).astype(BF16), wg_ref[...]))
    x3 = x2 + _dot(p_ref[...].astype(BF16), wp_ref[...]) * gate
    y_ref[...] = _rms(x3, nf_ref[...])


def _ple(x1, pe, p, nw, wg, wp, nf, tm):
    t, d = x1.shape
    full = lambda shape: pl.BlockSpec(shape, lambda i: (0,) * len(shape))
    tok = lambda w: pl.BlockSpec((tm, w), lambda i: (i, 0))
    return pl.pallas_call(
        _ple_kernel,
        out_shape=jax.ShapeDtypeStruct((t, d), F32),
        grid=(t // tm,),
        in_specs=[tok(d), tok(d), tok(PLE_DIM), full(nw.shape), full(wg.shape), full(wp.shape),
                  full(nf.shape)],
        out_specs=tok(d),
        compiler_params=_cparams(("parallel",)),
        name="ple",
    )(x1, pe, p, nw, wg, wp, nf)


def _rope_tables(s):
    half = ROT_DIM // 2
    inv = ROPE_THETA ** (-jnp.arange(0, ROT_DIM, 2, dtype=F32) / ROT_DIM)
    ang = jnp.arange(s, dtype=F32)[:, None] * inv[None, :]
    cos, sin = jnp.cos(ang), jnp.sin(ang)
    ones = jnp.ones((s, ATT_QK_DIM - ROT_DIM), F32)
    zeros = jnp.zeros((s, ATT_QK_DIM - ROT_DIM), F32)
    zh = jnp.zeros((s, half), F32)
    c64 = jnp.concatenate([cos, cos, ones], axis=1)
    s1_64 = jnp.concatenate([zh, sin, zeros], axis=1)
    s2_64 = jnp.concatenate([-sin, zh, zeros], axis=1)
    rep = LANES // ATT_QK_DIM
    return jnp.tile(c64, (1, rep)), jnp.tile(s1_64, (1, rep)), jnp.tile(s2_64, (1, rep))


def _pack_uv(u, v):
    ub = lax.bitcast_convert_type(u.astype(BF16), jnp.uint16).astype(jnp.uint32)
    vb = lax.bitcast_convert_type(v.astype(BF16), jnp.uint16).astype(jnp.uint32)
    return lax.bitcast_convert_type((ub << 16) | vb, jnp.int32)


def _mix_route(x, w):
    b, s, d = x.shape
    t = b * s
    nc = s // M_CHUNK
    rc, rs1, rs2 = _rope_tables(s)
    aq, ak, av, mq, mk, mv, mo, gc, gr = _inproj(
        x, w["norm_mix"], w["wqk"], w["wav"], w["wmqk"], w["wmvo"], w["wg"], w["wgt"], w["gb"],
        w["gbt"], w["conv"], rc, rs1, rs2, tm=512)
    att = _attn(w["lq"], aq, ak, av, w["attn_gain"], tq=256)
    gc5 = gc.reshape(b, nc, M_CHUNK, M_HEADS, 4).transpose(0, 3, 1, 2, 4)
    gr5 = gr.reshape(b, M_HEADS, 4, nc, M_CHUNK).transpose(0, 1, 3, 2, 4)
    mem = _mlstm(mq, mk, mv, mo, gc5, gr5, w["mlstm_gain"])
    x1, xn, qp = _outproj(x.reshape(t, d), att.reshape(t, ATT_W), mem.reshape(t, M_W),
                          w["wo_a"], w["wo_m"], w["norm_ffn"], w["wq"], tm=512)
    idx, g = _route(qp, w["sk"], tm=128)
    return x1, xn, idx, g


def _peer_all(groups, uv2):
    uv3 = uv2.reshape(-1, D_MODEL // LANES, LANES)
    staged, outs = [], []
    prev = None
    for xn, idx, g in groups:
        t, d = xn.shape
        ts = (t * PEER_SC_SHARE_NUM // PEER_SC_SHARE_DEN) // PEER_SC_QUANTUM * PEER_SC_QUANTUM
        assert 0 < ts < t and (t - ts) % PEER_TB == 0, (t, ts)
        staged.append(_stage_rows(uv2, idx[:ts]).reshape(ts, PEER_SEL, d))
        prev = _peer(idx, xn, g, uv3, ts, after=prev)
        outs.append(prev)
    for i, (xn, idx, g) in enumerate(groups):
        prev = _peer_staged(staged[i], xn, g, outs[i], after=prev)
        outs[i] = prev
    return outs


def kernel(x_prompt, x_sample, p_prompt, p_sample, norm_mix, w_in, gate_bias, conv_qk, lambda_qk,
           attn_gain, mlstm_gain, w_out, norm_ffn, peer_wq, peer_subkeys, peer_u, peer_v, norm_ple,
           ple_w_gate, ple_w_proj, norm_final):
    li = 0
    win = w_in[li]
    c0 = 0
    cols = {}
    for name, width in (("aq", ATT_W), ("ak", ATT_W), ("av", ATT_W), ("mq", M_W), ("mk", M_W),
                        ("mv", M_W), ("mo", M_W), ("gt", N_GATES)):
        cols[name] = win[:, c0:c0 + width]
        c0 += width
    perm = jnp.array([ty * M_HEADS + h for h in range(M_HEADS) for ty in range(4)], jnp.int32)
    wg = cols["gt"][:, perm].astype(BF16)
    gb = gate_bias[li][perm]
    w = {
        "norm_mix": norm_mix[li][None, :],
        "wqk": jnp.concatenate([cols["aq"], cols["ak"]], axis=1).astype(BF16),
        "wav": cols["av"].astype(BF16),
        "wmqk": jnp.concatenate([cols["mq"], cols["mk"]], axis=1).astype(BF16),
        "wmvo": jnp.concatenate([cols["mv"], cols["mo"]], axis=1).astype(BF16),
        "wg": wg,
        "wgt": wg.T,
        "gb": gb[None, :],
        "gbt": gb[:, None],
        "conv": conv_qk[li],
        "lq": lambda_qk[li],
        "attn_gain": attn_gain[li][None, :],
        "mlstm_gain": mlstm_gain[li][None, :],
        "wo_a": w_out[li][:ATT_W].astype(BF16),
        "wo_m": w_out[li][ATT_W:].astype(BF16),
        "norm_ffn": norm_ffn[li][None, :],
        "wq": peer_wq[li].astype(BF16),
        "sk": peer_subkeys[li].reshape(PEER_HEADS * 2, N_KEYS, PEER_HALF).astype(BF16),
        "uv": _pack_uv(peer_u[li], peer_v[li]),
        "norm_ple": norm_ple[li][None, :],
        "ple_wg": ple_w_gate[li].astype(BF16),
        "ple_wp": ple_w_proj[li].astype(BF16),
        "norm_final": norm_final[None, :],
    }
    xs = (x_prompt, x_sample)
    ps = (p_prompt[li], p_sample[li])
    mixed = [_mix_route(x, w) for x in xs]
    pes = _peer_all([(xn, idx, g) for _, xn, idx, g in mixed], w["uv"])
    ys = []
    for x, p, (x1, _, _, _), pe in zip(xs, ps, mixed, pes):
        y = _ple(x1, pe, p.reshape(-1, PLE_DIM), w["norm_ple"], w["ple_wg"], w["ple_wp"],
                 w["norm_final"], tm=512)
        ys.append(y.reshape(x.shape))
    return tuple(ys)
```

```python
import functools
import math

import jax
import jax.numpy as jnp
from jax import lax
from jax.experimental import pallas as pl
from jax.experimental.pallas import tpu as pltpu
from jax.experimental.pallas import tpu_sc as plsc

F32 = jnp.float32
BF16 = jnp.bfloat16

D_MODEL = 1024
ATT_W = 512
ATT_HEADS = 4
ATT_V_DIM = 128
ATT_QK_DIM = 64
ROT_DIM = 16
ROPE_THETA = 500000.0
M_W = 512
M_HEADS = 4
M_HEAD_DIM = 128
M_CHUNK = 128
CONV_K = 5
N_GATES = 16
PEER_HEADS = 8
PEER_QDIM = 256
PEER_HALF = 128
N_KEYS = 128
PEER_TOPK = 16
PEER_SEL = PEER_HEADS * PEER_TOPK
PLE_DIM = 256
EPS = 1e-6
LAM_INIT = 0.8 - 0.6 * math.exp(-0.3 * 0)
Q_SCALE = ATT_QK_DIM ** -0.5 * math.log2(math.e)

LANES = 128
SUBLANES = 8
VMEM_LIMIT = 56 * 1024 * 1024

NEG_INF = float("-inf")


def _cparams(sem):
    return pltpu.CompilerParams(dimension_semantics=sem, vmem_limit_bytes=VMEM_LIMIT)


def _dot(a, b):
    return jnp.dot(a, b, preferred_element_type=F32)


def _dot_nt(a, b):
    return lax.dot_general(a, b, (((1,), (1,)), ((), ())), preferred_element_type=F32)


def _rms(x, w):
    ms = jnp.mean(x * x, axis=-1, keepdims=True)
    return (x * lax.rsqrt(ms + EPS)) * w


def _sigmoid(x):
    return 1.0 / (1.0 + jnp.exp(-x))


def _log_sigmoid(x):
    return jnp.minimum(x, 0.0) - jnp.log(1.0 + jnp.exp(-jnp.abs(x)))


def _inproj_kernel(x_ref, xp_ref, xn_ref, nw_ref, wqk_ref, wav_ref, wmqk_ref, wmvo_ref,
                   wg_ref, wgt_ref, gb_ref, gbt_ref, cw_ref, rc_ref, rs1_ref, rs2_ref,
                   aq_ref, ak_ref, av_ref, mq_ref, mk_ref, mv_ref, mo_ref, gc_ref, gr_ref):
    i = pl.program_id(1)
    last = pl.num_programs(1) - 1
    nw = nw_ref[...]
    tm = x_ref.shape[1]

    h = _rms(x_ref[0], nw).astype(BF16)
    hp = _rms(xp_ref[0, 0], nw).astype(BF16)
    hn = _rms(xn_ref[0, 0], nw).astype(BF16)

    qk = _dot(h, wqk_ref[...])
    rc = rc_ref[...]
    rs1 = rs1_ref[...]
    rs2 = rs2_ref[...]
    half = ROT_DIM // 2
    for c in range(2 * ATT_W // LANES):
        xc = qk[:, c * LANES:(c + 1) * LANES]
        r = xc * rc + pltpu.roll(xc, half, 1) * rs1 + pltpu.roll(xc, LANES - half, 1) * rs2
        if c < ATT_W // LANES:
            aq_ref[0, :, c * LANES:(c + 1) * LANES] = (r * Q_SCALE).astype(BF16)
        else:
            cc = c - ATT_W // LANES
            ak_ref[0, :, cc * LANES:(cc + 1) * LANES] = r.astype(BF16)

    av_ref[0] = _dot(h, wav_ref[...]).astype(BF16)

    wm = wmqk_ref[...]
    pm = _dot(h, wm)
    pp = _dot(hp, wm) * (i > 0).astype(F32)
    pn = _dot(hn, wm) * (i < last).astype(F32)
    ext = jnp.concatenate([pp, pm, pn], axis=0)
    cw = cw_ref[...]
    off = SUBLANES - CONV_K // 2
    conv = cw[0:1, :] * ext[off:off + tm, :]
    for j in range(1, CONV_K):
        conv = conv + cw[j:j + 1, :] * ext[off + j:off + j + tm, :]
    act = conv * _sigmoid(conv)
    mq_ref[0] = act[:, :M_W].astype(BF16)
    mk_ref[0] = (act[:, M_W:] * (M_HEAD_DIM ** -0.5)).astype(BF16)

    vo = _dot(h, wmvo_ref[...])
    mv_ref[0] = vo[:, :M_W].astype(BF16)
    mo_ref[0] = vo[:, M_W:].astype(BF16)

    gc_ref[0] = _dot(h, wg_ref[...]) + gb_ref[...]
    gr_ref[0] = _dot_nt(wgt_ref[...], h) + gbt_ref[...]


def _inproj(x, norm_w, wqk, wav, wmqk, wmvo, wg, wgt, gb, gbt, conv_w, rc, rs1, rs2, tm):
    b, s, d = x.shape
    nt = s // tm
    x8 = x.reshape(b, s // SUBLANES, SUBLANES, d)
    r8 = tm // SUBLANES
    nb8 = s // SUBLANES
    full = lambda shape: pl.BlockSpec(shape, lambda bi, i: (0,) * len(shape))
    tok = lambda w: pl.BlockSpec((1, tm, w), lambda bi, i: (bi, i, 0))
    out_shapes = (
        [jax.ShapeDtypeStruct((b, s, ATT_W), BF16)] * 3
        + [jax.ShapeDtypeStruct((b, s, M_W), BF16)] * 4
        + [jax.ShapeDtypeStruct((b, s, N_GATES), F32), jax.ShapeDtypeStruct((b, N_GATES, s), F32)]
    )
    return pl.pallas_call(
        _inproj_kernel,
        out_shape=out_shapes,
        grid=(b, nt),
        in_specs=[
            tok(d),
            pl.BlockSpec((1, 1, SUBLANES, d), lambda bi, i: (bi, jnp.maximum(i * r8 - 1, 0), 0, 0)),
            pl.BlockSpec((1, 1, SUBLANES, d), lambda bi, i: (bi, jnp.minimum((i + 1) * r8, nb8 - 1), 0, 0)),
            full((1, d)),
            full(wqk.shape), full(wav.shape), full(wmqk.shape), full(wmvo.shape),
            full(wg.shape), full(wgt.shape), full(gb.shape), full(gbt.shape), full(conv_w.shape),
            pl.BlockSpec((tm, LANES), lambda bi, i: (i, 0)),
            pl.BlockSpec((tm, LANES), lambda bi, i: (i, 0)),
            pl.BlockSpec((tm, LANES), lambda bi, i: (i, 0)),
        ],
        out_specs=[tok(ATT_W)] * 3 + [tok(M_W)] * 4
        + [tok(N_GATES), pl.BlockSpec((1, N_GATES, tm), lambda bi, i: (bi, 0, i))],
        compiler_params=_cparams(("parallel", "parallel")),
        name="inproj",
    )(x, x8, x8, norm_w, wqk, wav, wmqk, wmvo, wg, wgt, gb, gbt, conv_w, rc, rs1, rs2)


def _attn_kernel(lq_ref, q_ref, k_ref, v_ref, gain_ref, o_ref):
    lq = lq_ref[...]
    lam = (jnp.exp(jnp.sum(lq[0:1] * lq[1:2], axis=1, keepdims=True))
           - jnp.exp(jnp.sum(lq[2:3] * lq[3:4], axis=1, keepdims=True)) + LAM_INIT)
    q = q_ref[0]
    k = k_ref[0]
    v = v_ref[0]
    lane = lax.broadcasted_iota(jnp.int32, q.shape, 1)
    zero = jnp.zeros_like(q)

    def soft(qh):
        sc = _dot_nt(qh, k)
        m = jnp.max(sc, axis=1, keepdims=True)
        e = jnp.exp2(sc - m)
        return _dot(e.astype(BF16), v), jnp.sum(e, axis=1, keepdims=True)

    o0, l0 = soft(jnp.where(lane < ATT_QK_DIM, q, zero))
    o1, l1 = soft(jnp.where(lane >= ATT_QK_DIM, q, zero))
    o = o0 * (1.0 / l0) - o1 * (lam / l1)
    o_ref[0] = (_rms(o, gain_ref[...]) * (1.0 - LAM_INIT)).astype(BF16)


def _attn(lq, q, k, v, gain, tq):
    b, s, _ = q.shape
    return pl.pallas_call(
        _attn_kernel,
        out_shape=jax.ShapeDtypeStruct((b, s, ATT_W), BF16),
        grid=(b, ATT_HEADS, s // tq),
        in_specs=[
            pl.BlockSpec(lq.shape, lambda bi, h, i: (0, 0)),
            pl.BlockSpec((1, tq, ATT_V_DIM), lambda bi, h, i: (bi, i, h)),
            pl.BlockSpec((1, s, ATT_V_DIM), lambda bi, h, i: (bi, 0, h)),
            pl.BlockSpec((1, s, ATT_V_DIM), lambda bi, h, i: (bi, 0, h)),
            pl.BlockSpec((1, ATT_V_DIM), lambda bi, h, i: (0, 0)),
        ],
        out_specs=pl.BlockSpec((1, tq, ATT_V_DIM), lambda bi, h, i: (bi, i, h)),
        compiler_params=_cparams(("parallel", "parallel", "parallel")),
        name="attn",
    )(lq, q, k, v, gain)


def _mlstm_chunk(q, k, v, ic_col, lf_col, ic_row, lf_row, cst, n, m, fwd):
    ln = q.shape[0]
    row = lax.broadcasted_iota(jnp.int32, (ln, ln), 0)
    col = lax.broadcasted_iota(jnp.int32, (ln, ln), 1)
    lower = col <= row
    upper = col >= row
    mask, mask_t = (lower, upper) if fwd else (upper, lower)
    bcum_col = jnp.sum(jnp.where(mask, lf_row, 0.0), axis=1, keepdims=True)
    bcum_row = jnp.sum(jnp.where(mask_t, lf_col, 0.0), axis=0, keepdims=True)
    dlog = jnp.where(mask, bcum_col - bcum_row + ic_row, NEG_INF)
    inter = bcum_col + m
    m_t = jnp.maximum(inter, jnp.max(dlog, axis=1, keepdims=True))
    w = jnp.exp(dlog - m_t)
    sc = jnp.exp(inter - m_t)
    wqk = w * _dot_nt(q, k)
    num = sc * _dot(q, cst.astype(BF16)) + _dot(wqk.astype(BF16), v)
    qn = jnp.sum(q.astype(F32) * n, axis=1, keepdims=True)
    den = sc * qn + jnp.sum(wqk, axis=1, keepdims=True)
    h = num / jnp.maximum(jnp.abs(den), jnp.exp(-m_t))
    b_last = jnp.sum(lf_row, axis=1, keepdims=True)
    rel_row = b_last - bcum_row + ic_row
    rel_col = b_last - bcum_col + ic_col
    m_new = jnp.maximum(b_last + m, jnp.max(rel_row, axis=1, keepdims=True))
    dec_col = jnp.exp(rel_col - m_new)
    keep = jnp.exp(b_last + m - m_new)
    kd = k.astype(F32) * dec_col
    c_new = keep * cst + _dot(kd.T.astype(BF16), v)
    n_new = keep * n + jnp.sum(kd, axis=0, keepdims=True)
    return h, c_new, n_new, m_new


def _mlstm_kernel(q_ref, k_ref, v_ref, o_ref, gc_ref, gr_ref, gain_ref, out_ref, hf_ref, hb_ref):
    s = q_ref.shape[1]
    nc = s // M_CHUNK
    d = M_HEAD_DIM
    gain = gain_ref[...]

    def step(c, carry, fwd):
        cst, n, m = carry
        sl = pl.ds(pl.multiple_of(c * M_CHUNK, M_CHUNK), M_CHUNK)
        gc = gc_ref[0, 0, c]
        gr = gr_ref[0, 0, c]
        t = 0 if fwd else 2
        h, cst, n, m = _mlstm_chunk(
            q_ref[0, sl, :], k_ref[0, sl, :], v_ref[0, sl, :],
            gc[:, t:t + 1], _log_sigmoid(gc[:, t + 1:t + 2]),
            gr[t:t + 1, :], _log_sigmoid(gr[t + 1:t + 2, :]),
            cst, n, m, fwd)
        return sl, h, (cst, n, m)

    init = (jnp.zeros((d, d), F32), jnp.zeros((1, d), F32), jnp.zeros((1, 1), F32))

    def scan_body(c, carry):
        cf, cb = carry
        sl, h, cf = step(c, cf, True)
        hf_ref[sl, :] = h
        sl, h, cb = step(nc - 1 - c, cb, False)
        hb_ref[sl, :] = h
        return cf, cb

    lax.fori_loop(0, nc, scan_body, (init, init))

    def gate_body(c, carry):
        sl = pl.ds(pl.multiple_of(c * M_CHUNK, M_CHUNK), M_CHUNK)
        y = _sigmoid(o_ref[0, sl, :].astype(F32)) * (hf_ref[sl, :] + hb_ref[sl, :])
        out_ref[0, sl, :] = _rms(y, gain).astype(BF16)
        return carry

    lax.fori_loop(0, nc, gate_body, 0)


def _mlstm(q, k, v, o, gc, gr, gain):
    b, s, _ = q.shape
    nc = s // M_CHUNK
    head = pl.BlockSpec((1, s, M_HEAD_DIM), lambda bi, h: (bi, 0, h))
    return pl.pallas_call(
        _mlstm_kernel,
        out_shape=jax.ShapeDtypeStruct((b, s, M_W), BF16),
        grid=(b, M_HEADS),
        in_specs=[
            head, head, head, head,
            pl.BlockSpec((1, 1, nc, M_CHUNK, 4), lambda bi, h: (bi, h, 0, 0, 0)),
            pl.BlockSpec((1, 1, nc, 4, M_CHUNK), lambda bi, h: (bi, h, 0, 0, 0)),
            pl.BlockSpec((1, M_HEAD_DIM), lambda bi, h: (0, h)),
        ],
        out_specs=head,
        scratch_shapes=[pltpu.VMEM((s, M_HEAD_DIM), F32), pltpu.VMEM((s, M_HEAD_DIM), F32)],
        compiler_params=_cparams(("parallel", "parallel")),
        name="mlstm",
    )(q, k, v, o, gc, gr, gain)


def _outproj_kernel(x_ref, att_ref, mem_ref, wa_ref, wm_ref, nw_ref, wq_ref, x1_ref, xn_ref, qp_ref):
    x1 = x_ref[...] + _dot(att_ref[...], wa_ref[...]) + _dot(mem_ref[...], wm_ref[...])
    x1_ref[...] = x1
    xn = _rms(x1, nw_ref[...])
    xn_ref[...] = xn
    qp = _dot(xn.astype(BF16), wq_ref[...])
    for c in range(qp_ref.shape[0]):
        qp_ref[c] = qp[:, c * LANES:(c + 1) * LANES].astype(BF16)


def _outproj(x, att, mem, wa, wm, nw, wq, tm):
    t, d = x.shape
    nq = wq.shape[1] // LANES
    full = lambda shape: pl.BlockSpec(shape, lambda i: (0,) * len(shape))
    tok = lambda w: pl.BlockSpec((tm, w), lambda i: (i, 0))
    return pl.pallas_call(
        _outproj_kernel,
        out_shape=[jax.ShapeDtypeStruct((t, d), F32), jax.ShapeDtypeStruct((t, d), F32),
                   jax.ShapeDtypeStruct((nq, t, LANES), BF16)],
        grid=(t // tm,),
        in_specs=[tok(d), tok(ATT_W), tok(M_W), full(wa.shape), full(wm.shape), full(nw.shape),
                  full(wq.shape)],
        out_specs=[tok(d), tok(d), pl.BlockSpec((nq, tm, LANES), lambda i: (0, i, 0))],
        compiler_params=_cparams(("parallel",)),
        name="outproj",
    )(x, att, mem, wa, wm, nw, wq)


ROUTE_HEADS_PER_ITER = 4


def _topk_rows(sc, n):
    pos = lax.broadcasted_iota(jnp.int32, sc.shape, 0).astype(F32)
    vals, poss = [], []
    for _ in range(PEER_TOPK):
        m = jnp.max(sc, axis=0, keepdims=True)
        p = jnp.min(jnp.where(sc == m, pos, float(n)), axis=0, keepdims=True)
        vals.append(m)
        poss.append(p)
        sc = jnp.where(pos == p, NEG_INF, sc)
    return jnp.concatenate(vals, axis=0), jnp.concatenate(poss, axis=0).astype(jnp.int32)


def _route_kernel(qp_ref, sk_ref, idx_ref, g_ref, idx_t, g_t):
    def head(h, carry):
        s1 = _dot_nt(sk_ref[2 * h], qp_ref[2 * h])
        s2 = _dot_nt(sk_ref[2 * h + 1], qp_ref[2 * h + 1])
        v1, i1 = _topk_rows(s1, N_KEYS)
        v2, i2 = _topk_rows(s2, N_KEYS)
        k = PEER_TOPK
        sub = SUBLANES
        brow = lax.broadcasted_iota(jnp.int32, (sub, v2.shape[1]), 0)
        slabs = [v1[0:1, :] + v2]
        for a in range(1, sub):
            slabs.append(jnp.where(brow < k // (a + 1), v1[a:a + 1, :] + v2[0:sub, :], NEG_INF))
        slabs.append(v1[sub:k, :] + v2[0:1, :])
        cand = jnp.concatenate(slabs, axis=0)
        sc, cp = _topk_rows(cand, cand.shape[0])
        mid = cp < k + sub * (sub - 1)
        ca = jnp.where(cp < k, 0, jnp.where(mid, lax.shift_right_logical(cp, 3) - 1, cp - (k + sub * (sub - 1)) + sub))
        cb = jnp.where(cp < k, cp, jnp.where(mid, cp & (sub - 1), 0))
        e1 = jnp.zeros_like(cp)
        e2 = jnp.zeros_like(cp)
        for a in range(PEER_TOPK):
            e1 = jnp.where(ca == a, i1[a:a + 1, :], e1)
            e2 = jnp.where(cb == a, i2[a:a + 1, :], e2)
        ex = jnp.exp(sc - sc[0:1, :])
        rows = pl.ds(pl.multiple_of(h * PEER_TOPK, PEER_TOPK), PEER_TOPK)
        idx_t[rows, :] = e1 * N_KEYS + e2
        g_t[rows, :] = ex / jnp.sum(ex, axis=0, keepdims=True)
        return carry

    def heads(hh, carry):
        for j in range(ROUTE_HEADS_PER_ITER):
            head(hh * ROUTE_HEADS_PER_ITER + j, carry)
        return carry

    lax.fori_loop(0, PEER_HEADS // ROUTE_HEADS_PER_ITER, heads, 0)
    idx_ref[...] = idx_t[...].T
    g_ref[...] = g_t[...].T


def _route(qp, sk, tm):
    nq, t, _ = qp.shape
    return pl.pallas_call(
        _route_kernel,
        out_shape=[jax.ShapeDtypeStruct((t, PEER_SEL), jnp.int32),
                   jax.ShapeDtypeStruct((t, PEER_SEL), F32)],
        grid=(t // tm,),
        in_specs=[pl.BlockSpec((nq, tm, LANES), lambda i: (0, i, 0)),
                  pl.BlockSpec(sk.shape, lambda i: (0, 0, 0))],
        out_specs=[pl.BlockSpec((tm, PEER_SEL), lambda i: (i, 0))] * 2,
        scratch_shapes=[pltpu.VMEM((PEER_SEL, tm), jnp.int32), pltpu.VMEM((PEER_SEL, tm), F32)],
        compiler_params=_cparams(("parallel",)),
        name="route",
    )(qp, sk)


PEER_TB = 256
PEER_NB = SUBLANES
PEER_NG = 4
PEER_LA = 2
HI16 = -65536


def _gelu(a):
    return 0.5 * a * (1.0 + lax.erf(a * (2.0 ** -0.5)))


def _peer_token(xr, gr, tile):
    nch = D_MODEL // LANES
    nkt = PEER_SEL // SUBLANES
    xb = [jnp.broadcast_to(xr[:, c * LANES:(c + 1) * LANES], (SUBLANES, LANES)) for c in range(nch)]
    parts = []
    for kt in range(nkt):
        acc = None
        for c in range(nch):
            u = lax.bitcast_convert_type(tile(kt, c) & HI16, F32)
            acc = u * xb[c] if acc is None else acc + u * xb[c]
        parts.append(acc)
    part = jnp.concatenate(parts, axis=0)
    a_row = jnp.sum(part.T, axis=0, keepdims=True)
    act = _gelu(a_row) * gr
    act_b = jnp.broadcast_to(act, (LANES, PEER_SEL)).T
    outs = []
    for c in range(nch):
        acc = None
        for kt in range(nkt):
            v = lax.bitcast_convert_type(tile(kt, c) << 16, F32)
            t = act_b[kt * SUBLANES:(kt + 1) * SUBLANES, :] * v
            acc = t if acc is None else acc + t
        outs.append(jnp.sum(acc, axis=0, keepdims=True))
    return jnp.concatenate(outs, axis=1)


def _peer_staged_kernel(st_ref, xn_ref, g_ref, rest_ref, after_ref, out_ref):
    del rest_ref, after_ref
    xg = xn_ref[...]
    gg = g_ref[...]
    rows = []
    for tb in range(xn_ref.shape[0]):
        def tile(kt, c, tb=tb):
            return st_ref[tb, kt * SUBLANES:(kt + 1) * SUBLANES, c * LANES:(c + 1) * LANES]
        rows.append(_peer_token(xg[tb:tb + 1, :], gg[tb:tb + 1, :], tile))
    out_ref[...] = jnp.concatenate(rows, axis=0)


def _peer_staged(staged, xn, g, rest, after=None):
    ts = staged.shape[0]
    t, d = xn.shape
    tb = PEER_NB
    return pl.pallas_call(
        _peer_staged_kernel,
        out_shape=jax.ShapeDtypeStruct((t, d), F32),
        grid=(ts // tb,),
        in_specs=[pl.BlockSpec((tb, PEER_SEL, d), lambda i: (i, 0, 0)),
                  pl.BlockSpec((tb, d), lambda i: (i, 0)),
                  pl.BlockSpec((tb, PEER_SEL), lambda i: (i, 0)),
                  pl.BlockSpec(memory_space=pl.ANY),
                  pl.BlockSpec(memory_space=pl.ANY)],
        out_specs=pl.BlockSpec((tb, d), lambda i: (i, 0)),
        input_output_aliases={3: 0},
        compiler_params=_cparams(("parallel",)),
        name="peer_staged",
    )(staged, xn, g, rest, _order_after(after))


SC_CORES = 2
SC_SUBCORES = 16
SC_WORKERS = SC_CORES * SC_SUBCORES
SC_ROWS = 32
SC_IDX_BLK = 128
PEER_SC_SHARE_NUM = 11
PEER_SC_SHARE_DEN = 16
PEER_SC_QUANTUM = SC_WORKERS * SC_ROWS * SC_IDX_BLK // PEER_SEL


def _stage_rows(uv2, idx):
    n_tok, nsel = idx.shape
    n_rows = n_tok * nsel
    rows_per_w = n_rows // SC_WORKERS
    n_blk = rows_per_w // (SC_ROWS * SC_IDX_BLK)
    idx4 = idx.reshape(SC_WORKERS, n_blk, SC_IDX_BLK, SC_ROWS)
    mesh = plsc.VectorSubcoreMesh(core_axis_name="c", subcore_axis_name="s")

    @functools.partial(
        pl.kernel, mesh=mesh,
        out_type=jax.ShapeDtypeStruct((n_rows, uv2.shape[1]), uv2.dtype),
        scratch_types=[
            pltpu.VMEM((SC_IDX_BLK, SC_ROWS), jnp.int32),
            pltpu.VMEM((2, SC_ROWS, uv2.shape[1]), uv2.dtype),
            pltpu.SemaphoreType.DMA((2,)),
            pltpu.SemaphoreType.DMA((2,)),
        ],
    )
    def stage(uv_hbm, idx_hbm, out_hbm, idx_v, rows_v, gsem, wsem):
        wid = lax.axis_index("s") * SC_CORES + lax.axis_index("c")
        base = wid * rows_per_w

        def gather(j, b):
            return pltpu.make_async_copy(uv_hbm.at[idx_v.at[j]], rows_v.at[b], gsem.at[b])

        def write(row0, b):
            return pltpu.make_async_copy(rows_v.at[b], out_hbm.at[pl.ds(row0, SC_ROWS)], wsem.at[b])

        @pl.loop(0, n_blk)
        def _(blk):
            pltpu.sync_copy(idx_hbm.at[wid, blk], idx_v)
            blk_row0 = base + blk * (SC_IDX_BLK * SC_ROWS)
            gather(0, 0).start()

            @pl.loop(0, SC_IDX_BLK, step=2)
            def _(j):
                for b in range(2):
                    jj = j + b
                    row0 = pl.multiple_of(blk_row0 + jj * SC_ROWS, SC_ROWS)
                    gather(jj, b).wait()

                    @pl.when(jj > 0)
                    def _():
                        write(row0 - SC_ROWS, 1 - b).wait()

                    @pl.when(jj + 1 < SC_IDX_BLK)
                    def _():
                        gather(jj + 1, 1 - b).start()

                    write(row0, b).start()

            write(blk_row0 + (SC_IDX_BLK - 1) * SC_ROWS, 1).wait()

    return stage(uv2, idx4)


def _peer_kernel(step0, idx_hbm, xn_ref, g_ref, uv_hbm, after_ref, out_ref, idx_s, buf, sem_i, sem):
    del after_ref
    step = pl.program_id(0) + step0
    nbatch = xn_ref.shape[0] // PEER_NB
    nch = D_MODEL // LANES
    nkt = PEER_SEL // SUBLANES

    cp = pltpu.make_async_copy(idx_hbm.at[step], idx_s, sem_i.at[0])
    cp.start()
    cp.wait()

    def issue(b):
        grp = b & (PEER_NG - 1)

        def tok_body(tb, carry):
            base = (b * PEER_NB + tb) * PEER_SEL
            for k in range(PEER_SEL):
                e = idx_s[base + k]
                pltpu.make_async_copy(
                    uv_hbm.at[e], buf.at[grp, tb, k // SUBLANES, :, k % SUBLANES, :], sem.at[grp]
                ).start(priority=k % 2)
            return carry

        lax.fori_loop(0, PEER_NB, tok_body, 0)

    def wait(b):
        grp = b & (PEER_NG - 1)
        pltpu.make_async_copy(buf.at[grp], buf.at[grp], sem.at[grp]).wait()

    def compute(b):
        grp = b & (PEER_NG - 1)
        r0 = pl.multiple_of(b * PEER_NB, PEER_NB)
        xg = xn_ref[pl.ds(r0, PEER_NB), :]
        gg = g_ref[pl.ds(r0, PEER_NB), :]
        rows = [_peer_token(xg[tb:tb + 1, :], gg[tb:tb + 1, :],
                            functools.partial(lambda tb, kt, c: buf[grp, tb, kt, c], tb))
                for tb in range(PEER_NB)]
        out_ref[pl.ds(r0, PEER_NB), :] = jnp.concatenate(rows, axis=0)

    for b in range(PEER_LA):
        issue(b)

    def body(b, carry):
        wait(b)

        @pl.when(b + PEER_LA < nbatch)
        def _():
            issue(b + PEER_LA)

        compute(b)
        return carry

    lax.fori_loop(0, nbatch, body, 0)


def _order_after(after):
    return jnp.zeros((SUBLANES, LANES), F32) if after is None else after


def _peer(idx, xn, g, uv, ts, after=None):
    t, d = xn.shape
    tb = PEER_TB
    idx2 = idx.reshape(t // tb, tb * PEER_SEL)
    s0 = ts // tb
    return pl.pallas_call(
        functools.partial(_peer_kernel, s0),
        out_shape=jax.ShapeDtypeStruct((t, d), F32),
        grid=((t - ts) // tb,),
        in_specs=[pl.BlockSpec(memory_space=pl.ANY),
                  pl.BlockSpec((tb, d), lambda i: (i + s0, 0)),
                  pl.BlockSpec((tb, PEER_SEL), lambda i: (i + s0, 0)),
                  pl.BlockSpec(memory_space=pl.ANY),
                  pl.BlockSpec(memory_space=pl.ANY)],
        out_specs=pl.BlockSpec((tb, d), lambda i: (i + s0, 0)),
        scratch_shapes=[pltpu.SMEM((tb * PEER_SEL,), jnp.int32),
                        pltpu.VMEM((PEER_NG, PEER_NB, PEER_SEL // SUBLANES, d // LANES, SUBLANES, LANES),
                                   jnp.int32),
                        pltpu.SemaphoreType.DMA((1,)),
                        pltpu.SemaphoreType.DMA((PEER_NG,))],
        compiler_params=_cparams(("arbitrary",)),
        name="peer",
    )(idx2, xn, g, uv, _order_after(after))


def _ple_kernel(x1_ref, pe_ref, p_ref, nw_ref, wg_ref, wp_ref, nf_ref, y_ref):
    x2 = x1_ref[...] + pe_ref[...]
    gate = _sigmoid(_dot(_rms(x2, nw_ref[...]).astype(BF16), wg_ref[...]))
    x3 = x2 + _dot(p_ref[...].astype(BF16), wp_ref[...]) * gate
    y_ref[...] = _rms(x3, nf_ref[...])


def _ple(x1, pe, p, nw, wg, wp, nf, tm):
    t, d = x1.shape
    full = lambda shape: pl.BlockSpec(shape, lambda i: (0,) * len(shape))
    tok = lambda w: pl.BlockSpec((tm, w), lambda i: (i, 0))
    return pl.pallas_call(
        _ple_kernel,
        out_shape=jax.ShapeDtypeStruct((t, d), F32),
        grid=(t // tm,),
        in_specs=[tok(d), tok(d), tok(PLE_DIM), full(nw.shape), full(wg.shape), full(wp.shape),
                  full(nf.shape)],
        out_specs=tok(d),
        compiler_params=_cparams(("parallel",)),
        name="ple",
    )(x1, pe, p, nw, wg, wp, nf)


def _rope_tables(s):
    half = ROT_DIM // 2
    inv = ROPE_THETA ** (-jnp.arange(0, ROT_DIM, 2, dtype=F32) / ROT_DIM)
    ang = jnp.arange(s, dtype=F32)[:, None] * inv[None, :]
    cos, sin = jnp.cos(ang), jnp.sin(ang)
    ones = jnp.ones((s, ATT_QK_DIM - ROT_DIM), F32)
    zeros = jnp.zeros((s, ATT_QK_DIM - ROT_DIM), F32)
    zh = jnp.zeros((s, half), F32)
    c64 = jnp.concatenate([cos, cos, ones], axis=1)
    s1_64 = jnp.concatenate([zh, sin, zeros], axis=1)
    s2_64 = jnp.concatenate([-sin, zh, zeros], axis=1)
    rep = LANES // ATT_QK_DIM
    return jnp.tile(c64, (1, rep)), jnp.tile(s1_64, (1, rep)), jnp.tile(s2_64, (1, rep))


def _pack_uv(u, v):
    ub = lax.bitcast_convert_type(u.astype(BF16), jnp.uint16).astype(jnp.uint32)
    vb = lax.bitcast_convert_type(v.astype(BF16), jnp.uint16).astype(jnp.uint32)
    return lax.bitcast_convert_type((ub << 16) | vb, jnp.int32)


def _mix_route(x, w):
    b, s, d = x.shape
    t = b * s
    nc = s // M_CHUNK
    rc, rs1, rs2 = _rope_tables(s)
    aq, ak, av, mq, mk, mv, mo, gc, gr = _inproj(
        x, w["norm_mix"], w["wqk"], w["wav"], w["wmqk"], w["wmvo"], w["wg"], w["wgt"], w["gb"],
        w["gbt"], w["conv"], rc, rs1, rs2, tm=512)
    att = _attn(w["lq"], aq, ak, av, w["attn_gain"], tq=256)
    gc5 = gc.reshape(b, nc, M_CHUNK, M_HEADS, 4).transpose(0, 3, 1, 2, 4)
    gr5 = gr.reshape(b, M_HEADS, 4, nc, M_CHUNK).transpose(0, 1, 3, 2, 4)
    mem = _mlstm(mq, mk, mv, mo, gc5, gr5, w["mlstm_gain"])
    x1, xn, qp = _outproj(x.reshape(t, d), att.reshape(t, ATT_W), mem.reshape(t, M_W),
                          w["wo_a"], w["wo_m"], w["norm_ffn"], w["wq"], tm=512)
    idx, g = _route(qp, w["sk"], tm=128)
    return x1, xn, idx, g


def _peer_all(groups, uv2):
    uv3 = uv2.reshape(-1, D_MODEL // LANES, LANES)
    staged, outs = [], []
    prev = None
    for xn, idx, g in groups:
        t, d = xn.shape
        ts = (t * PEER_SC_SHARE_NUM // PEER_SC_SHARE_DEN) // PEER_SC_QUANTUM * PEER_SC_QUANTUM
        assert 0 < ts < t and (t - ts) % PEER_TB == 0, (t, ts)
        staged.append(_stage_rows(uv2, idx[:ts]).reshape(ts, PEER_SEL, d))
        prev = _peer(idx, xn, g, uv3, ts, after=prev)
        outs.append(prev)
    for i, (xn, idx, g) in enumerate(groups):
        prev = _peer_staged(staged[i], xn, g, outs[i], after=prev)
        outs[i] = prev
    return outs


def kernel(x_prompt, x_sample, p_prompt, p_sample, norm_mix, w_in, gate_bias, conv_qk, lambda_qk,
           attn_gain, mlstm_gain, w_out, norm_ffn, peer_wq, peer_subkeys, peer_u, peer_v, norm_ple,
           ple_w_gate, ple_w_proj, norm_final):
    li = 0
    win = w_in[li]
    c0 = 0
    cols = {}
    for name, width in (("aq", ATT_W), ("ak", ATT_W), ("av", ATT_W), ("mq", M_W), ("mk", M_W),
                        ("mv", M_W), ("mo", M_W), ("gt", N_GATES)):
        cols[name] = win[:, c0:c0 + width]
        c0 += width
    perm = jnp.array([ty * M_HEADS + h for h in range(M_HEADS) for ty in range(4)], jnp.int32)
    wg = cols["gt"][:, perm].astype(BF16)
    gb = gate_bias[li][perm]
    w = {
        "norm_mix": norm_mix[li][None, :],
        "wqk": jnp.concatenate([cols["aq"], cols["ak"]], axis=1).astype(BF16),
        "wav": cols["av"].astype(BF16),
        "wmqk": jnp.concatenate([cols["mq"], cols["mk"]], axis=1).astype(BF16),
        "wmvo": jnp.concatenate([cols["mv"], cols["mo"]], axis=1).astype(BF16),
        "wg": wg,
        "wgt": wg.T,
        "gb": gb[None, :],
        "gbt": gb[:, None],
        "conv": conv_qk[li],
        "lq": lambda_qk[li],
        "attn_gain": attn_gain[li][None, :],
        "mlstm_gain": mlstm_gain[li][None, :],
        "wo_a": w_out[li][:ATT_W].astype(BF16),
        "wo_m": w_out[li][ATT_W:].astype(BF16),
        "norm_ffn": norm_ffn[li][None, :],
        "wq": peer_wq[li].astype(BF16),
        "sk": peer_subkeys[li].reshape(PEER_HEADS * 2, N_KEYS, PEER_HALF).astype(BF16),
        "uv": _pack_uv(peer_u[li], peer_v[li]),
        "norm_ple": norm_ple[li][None, :],
        "ple_wg": ple_w_gate[li].astype(BF16),
        "ple_wp": ple_w_proj[li].astype(BF16),
        "norm_final": norm_final[None, :],
    }
    xs = (x_prompt, x_sample)
    ps = (p_prompt[li], p_sample[li])
    mixed = [_mix_route(x, w) for x in xs]
    pes = _peer_all([(xn, idx, g) for _, xn, idx, g in mixed], w["uv"])
    ys = []
    for x, p, (x1, _, _, _), pe in zip(xs, ps, mixed, pes):
        y = _ple(x1, pe, p.reshape(-1, PLE_DIM), w["norm_ple"], w["ple_wg"], w["ple_wp"],
                 w["norm_final"], tm=512)
        ys.append(y.reshape(x.shape))
    return tuple(ys)
```

```python
import functools
import math

import jax
import jax.numpy as jnp
from jax import lax
from jax.experimental import pallas as pl
from jax.experimental.pallas import tpu as pltpu
from jax.experimental.pallas import tpu_sc as plsc

F32 = jnp.float32
BF16 = jnp.bfloat16

D_MODEL = 1024
ATT_W = 512
ATT_HEADS = 4
ATT_V_DIM = 128
ATT_QK_DIM = 64
ROT_DIM = 16
ROPE_THETA = 500000.0
M_W = 512
M_HEADS = 4
M_HEAD_DIM = 128
M_CHUNK = 128
CONV_K = 5
N_GATES = 16
PEER_HEADS = 8
PEER_QDIM = 256
PEER_HALF = 128
N_KEYS = 128
PEER_TOPK = 16
PEER_SEL = PEER_HEADS * PEER_TOPK
PLE_DIM = 256
EPS = 1e-6
LAM_INIT = 0.8 - 0.6 * math.exp(-0.3 * 0)
Q_SCALE = ATT_QK_DIM ** -0.5 * math.log2(math.e)

LANES = 128
SUBLANES = 8
VMEM_LIMIT = 56 * 1024 * 1024

NEG_INF = float("-inf")


def _cparams(sem):
    return pltpu.CompilerParams(dimension_semantics=sem, vmem_limit_bytes=VMEM_LIMIT)


def _dot(a, b):
    return jnp.dot(a, b, preferred_element_type=F32)


def _dot_nt(a, b):
    return lax.dot_general(a, b, (((1,), (1,)), ((), ())), preferred_element_type=F32)


def _rms(x, w):
    ms = jnp.mean(x * x, axis=-1, keepdims=True)
    return (x * lax.rsqrt(ms + EPS)) * w


def _sigmoid(x):
    return 1.0 / (1.0 + jnp.exp(-x))


def _log_sigmoid(x):
    return jnp.minimum(x, 0.0) - jnp.log(1.0 + jnp.exp(-jnp.abs(x)))


def _inproj_kernel(x_ref, xp_ref, xn_ref, nw_ref, wqk_ref, wav_ref, wmqk_ref, wmvo_ref,
                   wg_ref, wgt_ref, gb_ref, gbt_ref, cw_ref, rc_ref, rs1_ref, rs2_ref,
                   aq_ref, ak_ref, av_ref, mq_ref, mk_ref, mv_ref, mo_ref, gc_ref, gr_ref):
    i = pl.program_id(1)
    last = pl.num_programs(1) - 1
    nw = nw_ref[...]
    tm = x_ref.shape[1]

    h = _rms(x_ref[0], nw).astype(BF16)
    hp = _rms(xp_ref[0, 0], nw).astype(BF16)
    hn = _rms(xn_ref[0, 0], nw).astype(BF16)

    qk = _dot(h, wqk_ref[...])
    rc = rc_ref[...]
    rs1 = rs1_ref[...]
    rs2 = rs2_ref[...]
    half = ROT_DIM // 2
    for c in range(2 * ATT_W // LANES):
        xc = qk[:, c * LANES:(c + 1) * LANES]
        r = xc * rc + pltpu.roll(xc, half, 1) * rs1 + pltpu.roll(xc, LANES - half, 1) * rs2
        if c < ATT_W // LANES:
            aq_ref[0, :, c * LANES:(c + 1) * LANES] = (r * Q_SCALE).astype(BF16)
        else:
            cc = c - ATT_W // LANES
            ak_ref[0, :, cc * LANES:(cc + 1) * LANES] = r.astype(BF16)

    av_ref[0] = _dot(h, wav_ref[...]).astype(BF16)

    wm = wmqk_ref[...]
    pm = _dot(h, wm)
    pp = _dot(hp, wm) * (i > 0).astype(F32)
    pn = _dot(hn, wm) * (i < last).astype(F32)
    ext = jnp.concatenate([pp, pm, pn], axis=0)
    cw = cw_ref[...]
    off = SUBLANES - CONV_K // 2
    conv = cw[0:1, :] * ext[off:off + tm, :]
    for j in range(1, CONV_K):
        conv = conv + cw[j:j + 1, :] * ext[off + j:off + j + tm, :]
    act = conv * _sigmoid(conv)
    mq_ref[0] = act[:, :M_W].astype(BF16)
    mk_ref[0] = (act[:, M_W:] * (M_HEAD_DIM ** -0.5)).astype(BF16)

    vo = _dot(h, wmvo_ref[...])
    mv_ref[0] = vo[:, :M_W].astype(BF16)
    mo_ref[0] = vo[:, M_W:].astype(BF16)

    gc_ref[0] = _dot(h, wg_ref[...]) + gb_ref[...]
    gr_ref[0] = _dot_nt(wgt_ref[...], h) + gbt_ref[...]


def _inproj(x, norm_w, wqk, wav, wmqk, wmvo, wg, wgt, gb, gbt, conv_w, rc, rs1, rs2, tm):
    b, s, d = x.shape
    nt = s // tm
    x8 = x.reshape(b, s // SUBLANES, SUBLANES, d)
    r8 = tm // SUBLANES
    nb8 = s // SUBLANES
    full = lambda shape: pl.BlockSpec(shape, lambda bi, i: (0,) * len(shape))
    tok = lambda w: pl.BlockSpec((1, tm, w), lambda bi, i: (bi, i, 0))
    out_shapes = (
        [jax.ShapeDtypeStruct((b, s, ATT_W), BF16)] * 3
        + [jax.ShapeDtypeStruct((b, s, M_W), BF16)] * 4
        + [jax.ShapeDtypeStruct((b, s, N_GATES), F32), jax.ShapeDtypeStruct((b, N_GATES, s), F32)]
    )
    return pl.pallas_call(
        _inproj_kernel,
        out_shape=out_shapes,
        grid=(b, nt),
        in_specs=[
            tok(d),
            pl.BlockSpec((1, 1, SUBLANES, d), lambda bi, i: (bi, jnp.maximum(i * r8 - 1, 0), 0, 0)),
            pl.BlockSpec((1, 1, SUBLANES, d), lambda bi, i: (bi, jnp.minimum((i + 1) * r8, nb8 - 1), 0, 0)),
            full((1, d)),
            full(wqk.shape), full(wav.shape), full(wmqk.shape), full(wmvo.shape),
            full(wg.shape), full(wgt.shape), full(gb.shape), full(gbt.shape), full(conv_w.shape),
            pl.BlockSpec((tm, LANES), lambda bi, i: (i, 0)),
            pl.BlockSpec((tm, LANES), lambda bi, i: (i, 0)),
            pl.BlockSpec((tm, LANES), lambda bi, i: (i, 0)),
        ],
        out_specs=[tok(ATT_W)] * 3 + [tok(M_W)] * 4
        + [tok(N_GATES), pl.BlockSpec((1, N_GATES, tm), lambda bi, i: (bi, 0, i))],
        compiler_params=_cparams(("parallel", "parallel")),
        name="inproj",
    )(x, x8, x8, norm_w, wqk, wav, wmqk, wmvo, wg, wgt, gb, gbt, conv_w, rc, rs1, rs2)


ATT_KEY_BLOCK = 1024


def _attn_kernel(lq_ref, q_ref, k_ref, v_ref, gain_ref, o_ref):
    lq = lq_ref[...]
    lam = (jnp.exp(jnp.sum(lq[0:1] * lq[1:2], axis=1, keepdims=True))
           - jnp.exp(jnp.sum(lq[2:3] * lq[3:4], axis=1, keepdims=True)) + LAM_INIT)
    q = q_ref[0]
    tq = q.shape[0]
    lane = lax.broadcasted_iota(jnp.int32, q.shape, 1)
    zero = jnp.zeros_like(q)
    qq = jnp.concatenate([jnp.where(lane < ATT_QK_DIM, q, zero),
                          jnp.where(lane >= ATT_QK_DIM, q, zero)], axis=0)
    m = jnp.full((2 * tq, 1), NEG_INF, F32)
    ln = jnp.zeros((2 * tq, 1), F32)
    acc = jnp.zeros((2 * tq, ATT_V_DIM), F32)
    kb = min(ATT_KEY_BLOCK, k_ref.shape[1])
    assert k_ref.shape[1] % kb == 0
    for c in range(k_ref.shape[1] // kb):
        ks = slice(c * kb, (c + 1) * kb)
        sc = _dot_nt(qq, k_ref[0, ks, :])
        m_new = jnp.maximum(m, jnp.max(sc, axis=1, keepdims=True))
        alpha = jnp.exp2(m - m_new)
        e = jnp.exp2(sc - m_new)
        ln = alpha * ln + jnp.sum(e, axis=1, keepdims=True)
        acc = alpha * acc + _dot(e.astype(BF16), v_ref[0, ks, :])
        m = m_new
    o = acc[:tq] * (1.0 / ln[:tq]) - acc[tq:] * (lam / ln[tq:])
    o_ref[0] = (_rms(o, gain_ref[...]) * (1.0 - LAM_INIT)).astype(BF16)


def _attn(lq, q, k, v, gain, tq):
    b, s, _ = q.shape
    return pl.pallas_call(
        _attn_kernel,
        out_shape=jax.ShapeDtypeStruct((b, s, ATT_W), BF16),
        grid=(b, ATT_HEADS, s // tq),
        in_specs=[
            pl.BlockSpec(lq.shape, lambda bi, h, i: (0, 0)),
            pl.BlockSpec((1, tq, ATT_V_DIM), lambda bi, h, i: (bi, i, h)),
            pl.BlockSpec((1, s, ATT_V_DIM), lambda bi, h, i: (bi, 0, h)),
            pl.BlockSpec((1, s, ATT_V_DIM), lambda bi, h, i: (bi, 0, h)),
            pl.BlockSpec((1, ATT_V_DIM), lambda bi, h, i: (0, 0)),
        ],
        out_specs=pl.BlockSpec((1, tq, ATT_V_DIM), lambda bi, h, i: (bi, i, h)),
        compiler_params=_cparams(("parallel", "parallel", "parallel")),
        name="attn",
    )(lq, q, k, v, gain)


def _mlstm_chunk(q, k, v, ic_col, lf_col, ic_row, lf_row, cst, n, m, fwd):
    ln = q.shape[0]
    row = lax.broadcasted_iota(jnp.int32, (ln, ln), 0)
    col = lax.broadcasted_iota(jnp.int32, (ln, ln), 1)
    lower = col <= row
    upper = col >= row
    mask, mask_t = (lower, upper) if fwd else (upper, lower)
    bcum_col = jnp.sum(jnp.where(mask, lf_row, 0.0), axis=1, keepdims=True)
    bcum_row = jnp.sum(jnp.where(mask_t, lf_col, 0.0), axis=0, keepdims=True)
    dlog = jnp.where(mask, bcum_col - bcum_row + ic_row, NEG_INF)
    inter = bcum_col + m
    m_t = jnp.maximum(inter, jnp.max(dlog, axis=1, keepdims=True))
    w = jnp.exp(dlog - m_t)
    sc = jnp.exp(inter - m_t)
    wqk = w * _dot_nt(q, k)
    num = sc * _dot(q, cst.astype(BF16)) + _dot(wqk.astype(BF16), v)
    qn = jnp.sum(q.astype(F32) * n, axis=1, keepdims=True)
    den = sc * qn + jnp.sum(wqk, axis=1, keepdims=True)
    h = num / jnp.maximum(jnp.abs(den), jnp.exp(-m_t))
    b_last = jnp.sum(lf_row, axis=1, keepdims=True)
    rel_row = b_last - bcum_row + ic_row
    rel_col = b_last - bcum_col + ic_col
    m_new = jnp.maximum(b_last + m, jnp.max(rel_row, axis=1, keepdims=True))
    dec_col = jnp.exp(rel_col - m_new)
    keep = jnp.exp(b_last + m - m_new)
    kd = k.astype(F32) * dec_col
    c_new = keep * cst + _dot(kd.T.astype(BF16), v)
    n_new = keep * n + jnp.sum(kd, axis=0, keepdims=True)
    return h, c_new, n_new, m_new


M_HEADS_PER_STEP = 1


def _mlstm_kernel(q_ref, k_ref, v_ref, o_ref, gc_ref, gr_ref, gain_ref, out_ref, hf_ref, hb_ref):
    s = q_ref.shape[1]
    nc = s // M_CHUNK
    d = M_HEAD_DIM
    heads = range(M_HEADS_PER_STEP)

    def step(j, c, carry, fwd):
        cst, n, m = carry
        sl = pl.ds(pl.multiple_of(c * M_CHUNK, M_CHUNK), M_CHUNK)
        ln = slice(j * d, (j + 1) * d)
        gc = gc_ref[0, j, c]
        gr = gr_ref[0, j, c]
        t = 0 if fwd else 2
        h, cst, n, m = _mlstm_chunk(
            q_ref[0, sl, ln], k_ref[0, sl, ln], v_ref[0, sl, ln],
            gc[:, t:t + 1], _log_sigmoid(gc[:, t + 1:t + 2]),
            gr[t:t + 1, :], _log_sigmoid(gr[t + 1:t + 2, :]),
            cst, n, m, fwd)
        return sl, ln, h, (cst, n, m)

    init = (jnp.zeros((d, d), F32), jnp.zeros((1, d), F32), jnp.zeros((1, 1), F32))

    def scan_body(c, carry):
        new = []
        for j in heads:
            cf, cb = carry[j]
            sl, ln, h, cf = step(j, c, cf, True)
            hf_ref[sl, ln] = h
            sl, ln, h, cb = step(j, nc - 1 - c, cb, False)
            hb_ref[sl, ln] = h
            new.append((cf, cb))
        return tuple(new)

    lax.fori_loop(0, nc, scan_body, tuple((init, init) for _ in heads))

    def gate_body(c, carry):
        sl = pl.ds(pl.multiple_of(c * M_CHUNK, M_CHUNK), M_CHUNK)
        for j in heads:
            ln = slice(j * d, (j + 1) * d)
            y = _sigmoid(o_ref[0, sl, ln].astype(F32)) * (hf_ref[sl, ln] + hb_ref[sl, ln])
            out_ref[0, sl, ln] = _rms(y, gain_ref[:, ln]).astype(BF16)
        return carry

    lax.fori_loop(0, nc, gate_body, 0)


def _mlstm(q, k, v, o, gc, gr, gain):
    b, s, _ = q.shape
    nc = s // M_CHUNK
    hp = M_HEADS_PER_STEP
    w = hp * M_HEAD_DIM
    head = pl.BlockSpec((1, s, w), lambda bi, h: (bi, 0, h))
    return pl.pallas_call(
        _mlstm_kernel,
        out_shape=jax.ShapeDtypeStruct((b, s, M_W), BF16),
        grid=(b, M_HEADS // hp),
        in_specs=[
            head, head, head, head,
            pl.BlockSpec((1, hp, nc, M_CHUNK, 4), lambda bi, h: (bi, h, 0, 0, 0)),
            pl.BlockSpec((1, hp, nc, 4, M_CHUNK), lambda bi, h: (bi, h, 0, 0, 0)),
            pl.BlockSpec((1, w), lambda bi, h: (0, h)),
        ],
        out_specs=head,
        scratch_shapes=[pltpu.VMEM((s, w), F32), pltpu.VMEM((s, w), F32)],
        compiler_params=_cparams(("parallel", "parallel")),
        name="mlstm",
    )(q, k, v, o, gc, gr, gain)


def _outproj_kernel(x_ref, att_ref, mem_ref, wa_ref, wm_ref, nw_ref, wq_ref, x1_ref, xn_ref, qp_ref):
    x1 = x_ref[...] + _dot(att_ref[...], wa_ref[...]) + _dot(mem_ref[...], wm_ref[...])
    x1_ref[...] = x1
    xn = _rms(x1, nw_ref[...])
    xn_ref[...] = xn
    qp = _dot(xn.astype(BF16), wq_ref[...])
    for c in range(qp_ref.shape[0]):
        qp_ref[c] = qp[:, c * LANES:(c + 1) * LANES].astype(BF16)


def _outproj(x, att, mem, wa, wm, nw, wq, tm):
    t, d = x.shape
    nq = wq.shape[1] // LANES
    full = lambda shape: pl.BlockSpec(shape, lambda i: (0,) * len(shape))
    tok = lambda w: pl.BlockSpec((tm, w), lambda i: (i, 0))
    return pl.pallas_call(
        _outproj_kernel,
        out_shape=[jax.ShapeDtypeStruct((t, d), F32), jax.ShapeDtypeStruct((t, d), F32),
                   jax.ShapeDtypeStruct((nq, t, LANES), BF16)],
        grid=(t // tm,),
        in_specs=[tok(d), tok(ATT_W), tok(M_W), full(wa.shape), full(wm.shape), full(nw.shape),
                  full(wq.shape)],
        out_specs=[tok(d), tok(d), pl.BlockSpec((nq, tm, LANES), lambda i: (0, i, 0))],
        compiler_params=_cparams(("parallel",)),
        name="outproj",
    )(x, att, mem, wa, wm, nw, wq)


ROUTE_HEADS_PER_ITER = 4


def _topk_rows(sc, n):
    pos = lax.broadcasted_iota(jnp.int32, sc.shape, 0).astype(F32)
    vals, poss = [], []
    for _ in range(PEER_TOPK):
        m = jnp.max(sc, axis=0, keepdims=True)
        p = jnp.min(jnp.where(sc == m, pos, float(n)), axis=0, keepdims=True)
        vals.append(m)
        poss.append(p)
        sc = jnp.where(pos == p, NEG_INF, sc)
    return jnp.concatenate(vals, axis=0), jnp.concatenate(poss, axis=0).astype(jnp.int32)


def _route_kernel(qp_ref, sk_ref, idx_ref, g_ref, idx_t, g_t):
    def head(h, carry):
        s1 = _dot_nt(sk_ref[2 * h], qp_ref[2 * h])
        s2 = _dot_nt(sk_ref[2 * h + 1], qp_ref[2 * h + 1])
        v1, i1 = _topk_rows(s1, N_KEYS)
        v2, i2 = _topk_rows(s2, N_KEYS)
        k = PEER_TOPK
        sub = SUBLANES
        brow = lax.broadcasted_iota(jnp.int32, (sub, v2.shape[1]), 0)
        slabs = [v1[0:1, :] + v2]
        for a in range(1, sub):
            slabs.append(jnp.where(brow < k // (a + 1), v1[a:a + 1, :] + v2[0:sub, :], NEG_INF))
        slabs.append(v1[sub:k, :] + v2[0:1, :])
        cand = jnp.concatenate(slabs, axis=0)
        sc, cp = _topk_rows(cand, cand.shape[0])
        mid = cp < k + sub * (sub - 1)
        ca = jnp.where(cp < k, 0, jnp.where(mid, lax.shift_right_logical(cp, 3) - 1, cp - (k + sub * (sub - 1)) + sub))
        cb = jnp.where(cp < k, cp, jnp.where(mid, cp & (sub - 1), 0))
        e1 = jnp.zeros_like(cp)
        e2 = jnp.zeros_like(cp)
        for a in range(PEER_TOPK):
            e1 = jnp.where(ca == a, i1[a:a + 1, :], e1)
            e2 = jnp.where(cb == a, i2[a:a + 1, :], e2)
        ex = jnp.exp(sc - sc[0:1, :])
        rows = pl.ds(pl.multiple_of(h * PEER_TOPK, PEER_TOPK), PEER_TOPK)
        idx_t[rows, :] = e1 * N_KEYS + e2
        g_t[rows, :] = ex / jnp.sum(ex, axis=0, keepdims=True)
        return carry

    def heads(hh, carry):
        for j in range(ROUTE_HEADS_PER_ITER):
            head(hh * ROUTE_HEADS_PER_ITER + j, carry)
        return carry

    lax.fori_loop(0, PEER_HEADS // ROUTE_HEADS_PER_ITER, heads, 0)
    idx_ref[...] = idx_t[...].T
    g_ref[...] = g_t[...].T


def _route(qp, sk, tm):
    nq, t, _ = qp.shape
    return pl.pallas_call(
        _route_kernel,
        out_shape=[jax.ShapeDtypeStruct((t, PEER_SEL), jnp.int32),
                   jax.ShapeDtypeStruct((t, PEER_SEL), F32)],
        grid=(t // tm,),
        in_specs=[pl.BlockSpec((nq, tm, LANES), lambda i: (0, i, 0)),
                  pl.BlockSpec(sk.shape, lambda i: (0, 0, 0))],
        out_specs=[pl.BlockSpec((tm, PEER_SEL), lambda i: (i, 0))] * 2,
        scratch_shapes=[pltpu.VMEM((PEER_SEL, tm), jnp.int32), pltpu.VMEM((PEER_SEL, tm), F32)],
        compiler_params=_cparams(("parallel",)),
        name="route",
    )(qp, sk)


PEER_TB = 256
PEER_NB = SUBLANES
PEER_STAGED_TB = 16
PEER_NG = 4
PEER_LA = 2
HI16 = -65536


def _gelu(a):
    return 0.5 * a * (1.0 + lax.erf(a * (2.0 ** -0.5)))


def _peer_token(xr, gr, tile):
    nch = D_MODEL // LANES
    nkt = PEER_SEL // SUBLANES
    xb = [jnp.broadcast_to(xr[:, c * LANES:(c + 1) * LANES], (SUBLANES, LANES)) for c in range(nch)]
    parts = []
    for kt in range(nkt):
        acc = None
        for c in range(nch):
            u = lax.bitcast_convert_type(tile(kt, c) & HI16, F32)
            acc = u * xb[c] if acc is None else acc + u * xb[c]
        parts.append(acc)
    part = jnp.concatenate(parts, axis=0)
    a_row = jnp.sum(part.T, axis=0, keepdims=True)
    act = _gelu(a_row) * gr
    act_b = jnp.broadcast_to(act, (LANES, PEER_SEL)).T
    outs = []
    for c in range(nch):
        acc = None
        for kt in range(nkt):
            v = lax.bitcast_convert_type(tile(kt, c) << 16, F32)
            t = act_b[kt * SUBLANES:(kt + 1) * SUBLANES, :] * v
            acc = t if acc is None else acc + t
        outs.append(jnp.sum(acc, axis=0, keepdims=True))
    return jnp.concatenate(outs, axis=1)


def _peer_staged_kernel(st_ref, xn_ref, g_ref, rest_ref, after_ref, out_ref):
    del rest_ref, after_ref
    xg = xn_ref[...]
    gg = g_ref[...]
    rows = []
    for tb in range(xn_ref.shape[0]):
        def tile(kt, c, tb=tb):
            return st_ref[tb, kt * SUBLANES:(kt + 1) * SUBLANES, c * LANES:(c + 1) * LANES]
        rows.append(_peer_token(xg[tb:tb + 1, :], gg[tb:tb + 1, :], tile))
    out_ref[...] = jnp.concatenate(rows, axis=0)


def _peer_staged(staged, xn, g, rest, after=None):
    ts = staged.shape[0]
    t, d = xn.shape
    tb = PEER_STAGED_TB
    return pl.pallas_call(
        _peer_staged_kernel,
        out_shape=jax.ShapeDtypeStruct((t, d), F32),
        grid=(ts // tb,),
        in_specs=[pl.BlockSpec((tb, PEER_SEL, d), lambda i: (i, 0, 0)),
                  pl.BlockSpec((tb, d), lambda i: (i, 0)),
                  pl.BlockSpec((tb, PEER_SEL), lambda i: (i, 0)),
                  pl.BlockSpec(memory_space=pl.ANY),
                  pl.BlockSpec(memory_space=pl.ANY)],
        out_specs=pl.BlockSpec((tb, d), lambda i: (i, 0)),
        input_output_aliases={3: 0},
        compiler_params=_cparams(("parallel",)),
        name="peer_staged",
    )(staged, xn, g, rest, _order_after(after))


SC_CORES = 2
SC_SUBCORES = 16
SC_WORKERS = SC_CORES * SC_SUBCORES
SC_ROWS = 32
SC_IDX_BLK = 128
PEER_SC_SHARE_NUM = 10
PEER_SC_SHARE_DEN = 16
PEER_SC_QUANTUM = SC_WORKERS * SC_ROWS * SC_IDX_BLK // PEER_SEL


def _stage_rows(uv2, idx):
    n_tok, nsel = idx.shape
    n_rows = n_tok * nsel
    rows_per_w = n_rows // SC_WORKERS
    n_blk = rows_per_w // (SC_ROWS * SC_IDX_BLK)
    idx4 = idx.reshape(SC_WORKERS, n_blk, SC_IDX_BLK, SC_ROWS)
    mesh = plsc.VectorSubcoreMesh(core_axis_name="c", subcore_axis_name="s")

    @functools.partial(
        pl.kernel, mesh=mesh,
        out_type=jax.ShapeDtypeStruct((n_rows, uv2.shape[1]), uv2.dtype),
        scratch_types=[
            pltpu.VMEM((SC_IDX_BLK, SC_ROWS), jnp.int32),
            pltpu.VMEM((2, SC_ROWS, uv2.shape[1]), uv2.dtype),
            pltpu.SemaphoreType.DMA((2,)),
            pltpu.SemaphoreType.DMA((2,)),
        ],
    )
    def stage(uv_hbm, idx_hbm, out_hbm, idx_v, rows_v, gsem, wsem):
        wid = lax.axis_index("s") * SC_CORES + lax.axis_index("c")
        base = wid * rows_per_w

        def gather(j, b):
            return pltpu.make_async_copy(uv_hbm.at[idx_v.at[j]], rows_v.at[b], gsem.at[b])

        def write(row0, b):
            return pltpu.make_async_copy(rows_v.at[b], out_hbm.at[pl.ds(row0, SC_ROWS)], wsem.at[b])

        @pl.loop(0, n_blk)
        def _(blk):
            pltpu.sync_copy(idx_hbm.at[wid, blk], idx_v)
            blk_row0 = base + blk * (SC_IDX_BLK * SC_ROWS)
            gather(0, 0).start()

            @pl.loop(0, SC_IDX_BLK, step=2)
            def _(j):
                for b in range(2):
                    jj = j + b
                    row0 = pl.multiple_of(blk_row0 + jj * SC_ROWS, SC_ROWS)
                    gather(jj, b).wait()

                    @pl.when(jj > 0)
                    def _():
                        write(row0 - SC_ROWS, 1 - b).wait()

                    @pl.when(jj + 1 < SC_IDX_BLK)
                    def _():
                        gather(jj + 1, 1 - b).start()

                    write(row0, b).start()

            write(blk_row0 + (SC_IDX_BLK - 1) * SC_ROWS, 1).wait()

    return stage(uv2, idx4)


def _peer_kernel(step0, idx_hbm, xn_ref, g_ref, uv_hbm, after_ref, out_ref, idx_s, buf, sem_i, sem):
    del after_ref
    step = pl.program_id(0) + step0
    nbatch = xn_ref.shape[0] // PEER_NB
    nch = D_MODEL // LANES
    nkt = PEER_SEL // SUBLANES

    cp = pltpu.make_async_copy(idx_hbm.at[step], idx_s, sem_i.at[0])
    cp.start()
    cp.wait()

    def issue(b):
        grp = b & (PEER_NG - 1)

        def tok_body(tb, carry):
            base = (b * PEER_NB + tb) * PEER_SEL
            for k in range(PEER_SEL):
                e = idx_s[base + k]
                pltpu.make_async_copy(
                    uv_hbm.at[e], buf.at[grp, tb, k // SUBLANES, :, k % SUBLANES, :], sem.at[grp]
                ).start(priority=k % 2)
            return carry

        lax.fori_loop(0, PEER_NB, tok_body, 0)

    def wait(b):
        grp = b & (PEER_NG - 1)
        pltpu.make_async_copy(buf.at[grp], buf.at[grp], sem.at[grp]).wait()

    def compute(b):
        grp = b & (PEER_NG - 1)
        r0 = pl.multiple_of(b * PEER_NB, PEER_NB)
        xg = xn_ref[pl.ds(r0, PEER_NB), :]
        gg = g_ref[pl.ds(r0, PEER_NB), :]
        rows = [_peer_token(xg[tb:tb + 1, :], gg[tb:tb + 1, :],
                            functools.partial(lambda tb, kt, c: buf[grp, tb, kt, c], tb))
                for tb in range(PEER_NB)]
        out_ref[pl.ds(r0, PEER_NB), :] = jnp.concatenate(rows, axis=0)

    for b in range(PEER_LA):
        issue(b)

    def body(b, carry):
        wait(b)

        @pl.when(b + PEER_LA < nbatch)
        def _():
            issue(b + PEER_LA)

        compute(b)
        return carry

    lax.fori_loop(0, nbatch, body, 0)


def _order_after(after):
    return jnp.zeros((SUBLANES, LANES), F32) if after is None else after


def _peer(idx, xn, g, uv, ts, after=None):
    t, d = xn.shape
    tb = PEER_TB
    idx2 = idx.reshape(t // tb, tb * PEER_SEL)
    s0 = ts // tb
    return pl.pallas_call(
        functools.partial(_peer_kernel, s0),
        out_shape=jax.ShapeDtypeStruct((t, d), F32),
        grid=((t - ts) // tb,),
        in_specs=[pl.BlockSpec(memory_space=pl.ANY),
                  pl.BlockSpec((tb, d), lambda i: (i + s0, 0)),
                  pl.BlockSpec((tb, PEER_SEL), lambda i: (i + s0, 0)),
                  pl.BlockSpec(memory_space=pl.ANY),
                  pl.BlockSpec(memory_space=pl.ANY)],
        out_specs=pl.BlockSpec((tb, d), lambda i: (i + s0, 0)),
        scratch_shapes=[pltpu.SMEM((tb * PEER_SEL,), jnp.int32),
                        pltpu.VMEM((PEER_NG, PEER_NB, PEER_SEL // SUBLANES, d // LANES, SUBLANES, LANES),
                                   jnp.int32),
                        pltpu.SemaphoreType.DMA((1,)),
                        pltpu.SemaphoreType.DMA((PEER_NG,))],
        compiler_params=_cparams(("arbitrary",)),
        name="peer",
    )(idx2, xn, g, uv, _order_after(after))


def _ple_kernel(x1_ref, pe_ref, p_ref, nw_ref, wg_ref, wp_ref, nf_ref, y_ref):
    x2 = x1_ref[...] + pe_ref[...]
    gate = _sigmoid(_dot(_rms(x2, nw_ref[...]).astype(BF16), wg_ref[...]))
    x3 = x2 + _dot(p_ref[...].astype(BF16), wp_ref[...]) * gate
    y_ref[...] = _rms(x3, nf_ref[...])


def _ple(x1, pe, p, nw, wg, wp, nf, tm):
    t, d = x1.shape
    full = lambda shape: pl.BlockSpec(shape, lambda i: (0,) * len(shape))
    tok = lambda w: pl.BlockSpec((tm, w), lambda i: (i, 0))
    return pl.pallas_call(
        _ple_kernel,
        out_shape=jax.ShapeDtypeStruct((t, d), F32),
        grid=(t // tm,),
        in_specs=[tok(d), tok(d), tok(PLE_DIM), full(nw.shape), full(wg.shape), full(wp.shape),
                  full(nf.shape)],
        out_specs=tok(d),
        compiler_params=_cparams(("parallel",)),
        name="ple",
    )(x1, pe, p, nw, wg, wp, nf)


def _rope_tables(s):
    half = ROT_DIM // 2
    inv = ROPE_THETA ** (-jnp.arange(0, ROT_DIM, 2, dtype=F32) / ROT_DIM)
    ang = jnp.arange(s, dtype=F32)[:, None] * inv[None, :]
    cos, sin = jnp.cos(ang), jnp.sin(ang)
    ones = jnp.ones((s, ATT_QK_DIM - ROT_DIM), F32)
    zeros = jnp.zeros((s, ATT_QK_DIM - ROT_DIM), F32)
    zh = jnp.zeros((s, half), F32)
    c64 = jnp.concatenate([cos, cos, ones], axis=1)
    s1_64 = jnp.concatenate([zh, sin, zeros], axis=1)
    s2_64 = jnp.concatenate([-sin, zh, zeros], axis=1)
    rep = LANES // ATT_QK_DIM
    return jnp.tile(c64, (1, rep)), jnp.tile(s1_64, (1, rep)), jnp.tile(s2_64, (1, rep))


def _pack_uv(u, v):
    ub = lax.bitcast_convert_type(u.astype(BF16), jnp.uint16).astype(jnp.uint32)
    vb = lax.bitcast_convert_type(v.astype(BF16), jnp.uint16).astype(jnp.uint32)
    return lax.bitcast_convert_type((ub << 16) | vb, jnp.int32)


def _mix_route(x, w):
    b, s, d = x.shape
    t = b * s
    nc = s // M_CHUNK
    rc, rs1, rs2 = _rope_tables(s)
    aq, ak, av, mq, mk, mv, mo, gc, gr = _inproj(
        x, w["norm_mix"], w["wqk"], w["wav"], w["wmqk"], w["wmvo"], w["wg"], w["wgt"], w["gb"],
        w["gbt"], w["conv"], rc, rs1, rs2, tm=512)
    att = _attn(w["lq"], aq, ak, av, w["attn_gain"], tq=256)
    gc5 = gc.reshape(b, nc, M_CHUNK, M_HEADS, 4).transpose(0, 3, 1, 2, 4)
    gr5 = gr.reshape(b, M_HEADS, 4, nc, M_CHUNK).transpose(0, 1, 3, 2, 4)
    mem = _mlstm(mq, mk, mv, mo, gc5, gr5, w["mlstm_gain"])
    x1, xn, qp = _outproj(x.reshape(t, d), att.reshape(t, ATT_W), mem.reshape(t, M_W),
                          w["wo_a"], w["wo_m"], w["norm_ffn"], w["wq"], tm=512)
    idx, g = _route(qp, w["sk"], tm=128)
    return x1, xn, idx, g


def _peer_all(groups, uv2):
    uv3 = uv2.reshape(-1, D_MODEL // LANES, LANES)
    staged, outs = [], []
    prev = None
    for xn, idx, g in groups:
        t, d = xn.shape
        ts = (t * PEER_SC_SHARE_NUM // PEER_SC_SHARE_DEN) // PEER_SC_QUANTUM * PEER_SC_QUANTUM
        assert 0 < ts < t and (t - ts) % PEER_TB == 0, (t, ts)
        staged.append(_stage_rows(uv2, idx[:ts]).reshape(ts, PEER_SEL, d))
        prev = _peer(idx, xn, g, uv3, ts, after=prev)
        outs.append(prev)
    for i, (xn, idx, g) in enumerate(groups):
        prev = _peer_staged(staged[i], xn, g, outs[i], after=prev)
        outs[i] = prev
    return outs


def kernel(x_prompt, x_sample, p_prompt, p_sample, norm_mix, w_in, gate_bias, conv_qk, lambda_qk,
           attn_gain, mlstm_gain, w_out, norm_ffn, peer_wq, peer_subkeys, peer_u, peer_v, norm_ple,
           ple_w_gate, ple_w_proj, norm_final):
    li = 0
    win = w_in[li]
    c0 = 0
    cols = {}
    for name, width in (("aq", ATT_W), ("ak", ATT_W), ("av", ATT_W), ("mq", M_W), ("mk", M_W),
                        ("mv", M_W), ("mo", M_W), ("gt", N_GATES)):
        cols[name] = win[:, c0:c0 + width]
        c0 += width
    perm = jnp.array([ty * M_HEADS + h for h in range(M_HEADS) for ty in range(4)], jnp.int32)
    wg = cols["gt"][:, perm].astype(BF16)
    gb = gate_bias[li][perm]
    w = {
        "norm_mix": norm_mix[li][None, :],
        "wqk": jnp.concatenate([cols["aq"], cols["ak"]], axis=1).astype(BF16),
        "wav": cols["av"].astype(BF16),
        "wmqk": jnp.concatenate([cols["mq"], cols["mk"]], axis=1).astype(BF16),
        "wmvo": jnp.concatenate([cols["mv"], cols["mo"]], axis=1).astype(BF16),
        "wg": wg,
        "wgt": wg.T,
        "gb": gb[None, :],
        "gbt": gb[:, None],
        "conv": conv_qk[li],
        "lq": lambda_qk[li],
        "attn_gain": attn_gain[li][None, :],
        "mlstm_gain": mlstm_gain[li][None, :],
        "wo_a": w_out[li][:ATT_W].astype(BF16),
        "wo_m": w_out[li][ATT_W:].astype(BF16),
        "norm_ffn": norm_ffn[li][None, :],
        "wq": peer_wq[li].astype(BF16),
        "sk": peer_subkeys[li].reshape(PEER_HEADS * 2, N_KEYS, PEER_HALF).astype(BF16),
        "uv": _pack_uv(peer_u[li], peer_v[li]),
        "norm_ple": norm_ple[li][None, :],
        "ple_wg": ple_w_gate[li].astype(BF16),
        "ple_wp": ple_w_proj[li].astype(BF16),
        "norm_final": norm_final[None, :],
    }
    xs = (x_prompt, x_sample)
    ps = (p_prompt[li], p_sample[li])
    mixed = [_mix_route(x, w) for x in xs]
    pes = _peer_all([(xn, idx, g) for _, xn, idx, g in mixed], w["uv"])
    ys = []
    for x, p, (x1, _, _, _), pe in zip(xs, ps, mixed, pes):
        y = _ple(x1, pe, p.reshape(-1, PLE_DIM), w["norm_ple"], w["ple_wg"], w["ple_wp"],
                 w["norm_final"], tm=512)
        ys.append(y.reshape(x.shape))
    return tuple(ys)
```

```python
import functools
import math

import jax
import jax.numpy as jnp
from jax import lax
from jax.experimental import pallas as pl
from jax.experimental.pallas import tpu as pltpu
from jax.experimental.pallas import tpu_sc as plsc

F32 = jnp.float32
BF16 = jnp.bfloat16

D_MODEL = 1024
ATT_W = 512
ATT_HEADS = 4
ATT_V_DIM = 128
ATT_QK_DIM = 64
ROT_DIM = 16
ROPE_THETA = 500000.0
M_W = 512
M_HEADS = 4
M_HEAD_DIM = 128
M_CHUNK = 128
CONV_K = 5
N_GATES = 16
PEER_HEADS = 8
PEER_QDIM = 256
PEER_HALF = 128
N_KEYS = 128
PEER_TOPK = 16
PEER_SEL = PEER_HEADS * PEER_TOPK
PLE_DIM = 256
EPS = 1e-6
LAM_INIT = 0.8 - 0.6 * math.exp(-0.3 * 0)
Q_SCALE = ATT_QK_DIM ** -0.5 * math.log2(math.e)

LANES = 128
SUBLANES = 8
VMEM_LIMIT = 56 * 1024 * 1024

NEG_INF = float("-inf")


def _cparams(sem):
    return pltpu.CompilerParams(dimension_semantics=sem, vmem_limit_bytes=VMEM_LIMIT)


def _dot(a, b):
    return jnp.dot(a, b, preferred_element_type=F32)


def _dot_nt(a, b):
    return lax.dot_general(a, b, (((1,), (1,)), ((), ())), preferred_element_type=F32)


def _rms(x, w):
    ms = jnp.mean(x * x, axis=-1, keepdims=True)
    return (x * lax.rsqrt(ms + EPS)) * w


def _sigmoid(x):
    return 1.0 / (1.0 + jnp.exp(-x))


def _log_sigmoid(x):
    return jnp.minimum(x, 0.0) - jnp.log(1.0 + jnp.exp(-jnp.abs(x)))


def _inproj_kernel(x_ref, xp_ref, xn_ref, nw_ref, wqk_ref, wav_ref, wmqk_ref, wmvo_ref,
                   wg_ref, wgt_ref, gb_ref, gbt_ref, cw_ref, rc_ref, rs1_ref, rs2_ref,
                   aq_ref, ak_ref, av_ref, mq_ref, mk_ref, mv_ref, mo_ref, gc_ref, gr_ref):
    i = pl.program_id(1)
    last = pl.num_programs(1) - 1
    nw = nw_ref[...]
    tm = x_ref.shape[1]

    h = _rms(x_ref[0], nw).astype(BF16)
    hp = _rms(xp_ref[0, 0], nw).astype(BF16)
    hn = _rms(xn_ref[0, 0], nw).astype(BF16)

    qk = _dot(h, wqk_ref[...])
    rc = rc_ref[...]
    rs1 = rs1_ref[...]
    rs2 = rs2_ref[...]
    half = ROT_DIM // 2
    for c in range(2 * ATT_W // LANES):
        xc = qk[:, c * LANES:(c + 1) * LANES]
        r = xc * rc + pltpu.roll(xc, half, 1) * rs1 + pltpu.roll(xc, LANES - half, 1) * rs2
        if c < ATT_W // LANES:
            aq_ref[0, :, c * LANES:(c + 1) * LANES] = (r * Q_SCALE).astype(BF16)
        else:
            cc = c - ATT_W // LANES
            ak_ref[0, :, cc * LANES:(cc + 1) * LANES] = r.astype(BF16)

    av_ref[0] = _dot(h, wav_ref[...]).astype(BF16)

    wm = wmqk_ref[...]
    pm = _dot(h, wm)
    pp = _dot(hp, wm) * (i > 0).astype(F32)
    pn = _dot(hn, wm) * (i < last).astype(F32)
    ext = jnp.concatenate([pp, pm, pn], axis=0)
    cw = cw_ref[...]
    off = SUBLANES - CONV_K // 2
    conv = cw[0:1, :] * ext[off:off + tm, :]
    for j in range(1, CONV_K):
        conv = conv + cw[j:j + 1, :] * ext[off + j:off + j + tm, :]
    act = conv * _sigmoid(conv)
    mq_ref[0] = act[:, :M_W].astype(BF16)
    mk_ref[0] = (act[:, M_W:] * (M_HEAD_DIM ** -0.5)).astype(BF16)

    vo = _dot(h, wmvo_ref[...])
    mv_ref[0] = vo[:, :M_W].astype(BF16)
    mo_ref[0] = vo[:, M_W:].astype(BF16)

    gc_ref[0] = _dot(h, wg_ref[...]) + gb_ref[...]
    gr_ref[0] = _dot_nt(wgt_ref[...], h) + gbt_ref[...]


def _inproj(x, norm_w, wqk, wav, wmqk, wmvo, wg, wgt, gb, gbt, conv_w, rc, rs1, rs2, tm):
    b, s, d = x.shape
    nt = s // tm
    x8 = x.reshape(b, s // SUBLANES, SUBLANES, d)
    r8 = tm // SUBLANES
    nb8 = s // SUBLANES
    full = lambda shape: pl.BlockSpec(shape, lambda bi, i: (0,) * len(shape))
    tok = lambda w: pl.BlockSpec((1, tm, w), lambda bi, i: (bi, i, 0))
    out_shapes = (
        [jax.ShapeDtypeStruct((b, s, ATT_W), BF16)] * 3
        + [jax.ShapeDtypeStruct((b, s, M_W), BF16)] * 4
        + [jax.ShapeDtypeStruct((b, s, N_GATES), F32), jax.ShapeDtypeStruct((b, N_GATES, s), F32)]
    )
    return pl.pallas_call(
        _inproj_kernel,
        out_shape=out_shapes,
        grid=(b, nt),
        in_specs=[
            tok(d),
            pl.BlockSpec((1, 1, SUBLANES, d), lambda bi, i: (bi, jnp.maximum(i * r8 - 1, 0), 0, 0)),
            pl.BlockSpec((1, 1, SUBLANES, d), lambda bi, i: (bi, jnp.minimum((i + 1) * r8, nb8 - 1), 0, 0)),
            full((1, d)),
            full(wqk.shape), full(wav.shape), full(wmqk.shape), full(wmvo.shape),
            full(wg.shape), full(wgt.shape), full(gb.shape), full(gbt.shape), full(conv_w.shape),
            pl.BlockSpec((tm, LANES), lambda bi, i: (i, 0)),
            pl.BlockSpec((tm, LANES), lambda bi, i: (i, 0)),
            pl.BlockSpec((tm, LANES), lambda bi, i: (i, 0)),
        ],
        out_specs=[tok(ATT_W)] * 3 + [tok(M_W)] * 4
        + [tok(N_GATES), pl.BlockSpec((1, N_GATES, tm), lambda bi, i: (bi, 0, i))],
        compiler_params=_cparams(("parallel", "parallel")),
        name="inproj",
    )(x, x8, x8, norm_w, wqk, wav, wmqk, wmvo, wg, wgt, gb, gbt, conv_w, rc, rs1, rs2)


ATT_KEY_BLOCK = 1024


def _attn_kernel(lq_ref, q_ref, k_ref, v_ref, gain_ref, o_ref):
    lq = lq_ref[...]
    lam = (jnp.exp(jnp.sum(lq[0:1] * lq[1:2], axis=1, keepdims=True))
           - jnp.exp(jnp.sum(lq[2:3] * lq[3:4], axis=1, keepdims=True)) + LAM_INIT)
    q = q_ref[0]
    tq = q.shape[0]
    lane = lax.broadcasted_iota(jnp.int32, q.shape, 1)
    zero = jnp.zeros_like(q)
    qq = jnp.concatenate([jnp.where(lane < ATT_QK_DIM, q, zero),
                          jnp.where(lane >= ATT_QK_DIM, q, zero)], axis=0)
    m = jnp.full((2 * tq, 1), NEG_INF, F32)
    ln = jnp.zeros((2 * tq, 1), F32)
    acc = jnp.zeros((2 * tq, ATT_V_DIM), F32)
    kb = min(ATT_KEY_BLOCK, k_ref.shape[1])
    assert k_ref.shape[1] % kb == 0
    for c in range(k_ref.shape[1] // kb):
        ks = slice(c * kb, (c + 1) * kb)
        sc = _dot_nt(qq, k_ref[0, ks, :])
        m_new = jnp.maximum(m, jnp.max(sc, axis=1, keepdims=True))
        alpha = jnp.exp2(m - m_new)
        e = jnp.exp2(sc - m_new)
        ln = alpha * ln + jnp.sum(e, axis=1, keepdims=True)
        acc = alpha * acc + _dot(e.astype(BF16), v_ref[0, ks, :])
        m = m_new
    o = acc[:tq] * (1.0 / ln[:tq]) - acc[tq:] * (lam / ln[tq:])
    o_ref[0] = (_rms(o, gain_ref[...]) * (1.0 - LAM_INIT)).astype(BF16)


def _attn(lq, q, k, v, gain, tq):
    b, s, _ = q.shape
    return pl.pallas_call(
        _attn_kernel,
        out_shape=jax.ShapeDtypeStruct((b, s, ATT_W), BF16),
        grid=(b, ATT_HEADS, s // tq),
        in_specs=[
            pl.BlockSpec(lq.shape, lambda bi, h, i: (0, 0)),
            pl.BlockSpec((1, tq, ATT_V_DIM), lambda bi, h, i: (bi, i, h)),
            pl.BlockSpec((1, s, ATT_V_DIM), lambda bi, h, i: (bi, 0, h)),
            pl.BlockSpec((1, s, ATT_V_DIM), lambda bi, h, i: (bi, 0, h)),
            pl.BlockSpec((1, ATT_V_DIM), lambda bi, h, i: (0, 0)),
        ],
        out_specs=pl.BlockSpec((1, tq, ATT_V_DIM), lambda bi, h, i: (bi, i, h)),
        compiler_params=_cparams(("parallel", "parallel", "parallel")),
        name="attn",
    )(lq, q, k, v, gain)


def _mlstm_chunk(q, k, v, ic_col, lf_col, ic_row, lf_row, cst, n, m, fwd):
    ln = q.shape[0]
    row = lax.broadcasted_iota(jnp.int32, (ln, ln), 0)
    col = lax.broadcasted_iota(jnp.int32, (ln, ln), 1)
    lower = col <= row
    upper = col >= row
    mask, mask_t = (lower, upper) if fwd else (upper, lower)
    bcum_col = jnp.sum(jnp.where(mask, lf_row, 0.0), axis=1, keepdims=True)
    bcum_row = jnp.sum(jnp.where(mask_t, lf_col, 0.0), axis=0, keepdims=True)
    dlog = jnp.where(mask, bcum_col - bcum_row + ic_row, NEG_INF)
    inter = bcum_col + m
    m_t = jnp.maximum(inter, jnp.max(dlog, axis=1, keepdims=True))
    w = jnp.exp(dlog - m_t)
    sc = jnp.exp(inter - m_t)
    wqk = w * _dot_nt(q, k)
    num = sc * _dot(q, cst.astype(BF16)) + _dot(wqk.astype(BF16), v)
    qn = jnp.sum(q.astype(F32) * n, axis=1, keepdims=True)
    den = sc * qn + jnp.sum(wqk, axis=1, keepdims=True)
    h = num / jnp.maximum(jnp.abs(den), jnp.exp(-m_t))
    b_last = jnp.sum(lf_row, axis=1, keepdims=True)
    rel_row = b_last - bcum_row + ic_row
    rel_col = b_last - bcum_col + ic_col
    m_new = jnp.maximum(b_last + m, jnp.max(rel_row, axis=1, keepdims=True))
    dec_col = jnp.exp(rel_col - m_new)
    keep = jnp.exp(b_last + m - m_new)
    kd = k.astype(F32) * dec_col
    c_new = keep * cst + lax.dot_general(kd.astype(BF16), v, (((0,), (0,)), ((), ())),
                                         preferred_element_type=F32)
    n_new = keep * n + jnp.sum(kd, axis=0, keepdims=True)
    return h, c_new, n_new, m_new


M_HEADS_PER_STEP = 1


def _mlstm_kernel(q_ref, k_ref, v_ref, o_ref, gc_ref, gr_ref, gain_ref, out_ref, hf_ref, hb_ref):
    s = q_ref.shape[1]
    nc = s // M_CHUNK
    d = M_HEAD_DIM
    heads = range(M_HEADS_PER_STEP)

    def step(j, c, carry, fwd):
        cst, n, m = carry
        sl = pl.ds(pl.multiple_of(c * M_CHUNK, M_CHUNK), M_CHUNK)
        ln = slice(j * d, (j + 1) * d)
        gc = gc_ref[0, j, c]
        gr = gr_ref[0, j, c]
        t = 0 if fwd else 2
        h, cst, n, m = _mlstm_chunk(
            q_ref[0, sl, ln], k_ref[0, sl, ln], v_ref[0, sl, ln],
            gc[:, t:t + 1], _log_sigmoid(gc[:, t + 1:t + 2]),
            gr[t:t + 1, :], _log_sigmoid(gr[t + 1:t + 2, :]),
            cst, n, m, fwd)
        return sl, ln, h, (cst, n, m)

    init = (jnp.zeros((d, d), F32), jnp.zeros((1, d), F32), jnp.zeros((1, 1), F32))

    def scan_body(c, carry):
        new = []
        for j in heads:
            cf, cb = carry[j]
            sl, ln, h, cf = step(j, c, cf, True)
            hf_ref[sl, ln] = h
            sl, ln, h, cb = step(j, nc - 1 - c, cb, False)
            hb_ref[sl, ln] = h
            new.append((cf, cb))
        return tuple(new)

    lax.fori_loop(0, nc, scan_body, tuple((init, init) for _ in heads))

    def gate_body(c, carry):
        sl = pl.ds(pl.multiple_of(c * M_CHUNK, M_CHUNK), M_CHUNK)
        for j in heads:
            ln = slice(j * d, (j + 1) * d)
            y = _sigmoid(o_ref[0, sl, ln].astype(F32)) * (hf_ref[sl, ln] + hb_ref[sl, ln])
            out_ref[0, sl, ln] = _rms(y, gain_ref[:, ln]).astype(BF16)
        return carry

    lax.fori_loop(0, nc, gate_body, 0)


def _mlstm(q, k, v, o, gc, gr, gain):
    b, s, _ = q.shape
    nc = s // M_CHUNK
    hp = M_HEADS_PER_STEP
    w = hp * M_HEAD_DIM
    head = pl.BlockSpec((1, s, w), lambda bi, h: (bi, 0, h))
    return pl.pallas_call(
        _mlstm_kernel,
        out_shape=jax.ShapeDtypeStruct((b, s, M_W), BF16),
        grid=(b, M_HEADS // hp),
        in_specs=[
            head, head, head, head,
            pl.BlockSpec((1, hp, nc, M_CHUNK, 4), lambda bi, h: (bi, h, 0, 0, 0)),
            pl.BlockSpec((1, hp, nc, 4, M_CHUNK), lambda bi, h: (bi, h, 0, 0, 0)),
            pl.BlockSpec((1, w), lambda bi, h: (0, h)),
        ],
        out_specs=head,
        scratch_shapes=[pltpu.VMEM((s, w), F32), pltpu.VMEM((s, w), F32)],
        compiler_params=_cparams(("parallel", "parallel")),
        name="mlstm",
    )(q, k, v, o, gc, gr, gain)


def _outproj_kernel(x_ref, att_ref, mem_ref, wa_ref, wm_ref, nw_ref, wq_ref, x1_ref, xn_ref, qp_ref):
    x1 = x_ref[...] + _dot(att_ref[...], wa_ref[...]) + _dot(mem_ref[...], wm_ref[...])
    x1_ref[...] = x1
    xn = _rms(x1, nw_ref[...])
    xn_ref[...] = xn
    qp = _dot(xn.astype(BF16), wq_ref[...])
    for c in range(qp_ref.shape[0]):
        qp_ref[c] = qp[:, c * LANES:(c + 1) * LANES].astype(BF16)


def _outproj(x, att, mem, wa, wm, nw, wq, tm):
    t, d = x.shape
    nq = wq.shape[1] // LANES
    full = lambda shape: pl.BlockSpec(shape, lambda i: (0,) * len(shape))
    tok = lambda w: pl.BlockSpec((tm, w), lambda i: (i, 0))
    return pl.pallas_call(
        _outproj_kernel,
        out_shape=[jax.ShapeDtypeStruct((t, d), F32), jax.ShapeDtypeStruct((t, d), F32),
                   jax.ShapeDtypeStruct((nq, t, LANES), BF16)],
        grid=(t // tm,),
        in_specs=[tok(d), tok(ATT_W), tok(M_W), full(wa.shape), full(wm.shape), full(nw.shape),
                  full(wq.shape)],
        out_specs=[tok(d), tok(d), pl.BlockSpec((nq, tm, LANES), lambda i: (0, i, 0))],
        compiler_params=_cparams(("parallel",)),
        name="outproj",
    )(x, att, mem, wa, wm, nw, wq)


ROUTE_HEADS_PER_ITER = 4


def _topk_rows(sc, n):
    pos = lax.broadcasted_iota(jnp.int32, sc.shape, 0).astype(F32)
    vals, poss = [], []
    for _ in range(PEER_TOPK):
        m = jnp.max(sc, axis=0, keepdims=True)
        p = jnp.min(jnp.where(sc == m, pos, float(n)), axis=0, keepdims=True)
        vals.append(m)
        poss.append(p)
        sc = jnp.where(pos == p, NEG_INF, sc)
    return jnp.concatenate(vals, axis=0), jnp.concatenate(poss, axis=0).astype(jnp.int32)


def _route_kernel(qp_ref, sk_ref, idx_ref, g_ref, idx_t, g_t):
    def head(h, carry):
        s1 = _dot_nt(sk_ref[2 * h], qp_ref[2 * h])
        s2 = _dot_nt(sk_ref[2 * h + 1], qp_ref[2 * h + 1])
        v1, i1 = _topk_rows(s1, N_KEYS)
        v2, i2 = _topk_rows(s2, N_KEYS)
        k = PEER_TOPK
        sub = SUBLANES
        brow = lax.broadcasted_iota(jnp.int32, (sub, v2.shape[1]), 0)
        slabs = [v1[0:1, :] + v2]
        for a in range(1, sub):
            slabs.append(jnp.where(brow < k // (a + 1), v1[a:a + 1, :] + v2[0:sub, :], NEG_INF))
        slabs.append(v1[sub:k, :] + v2[0:1, :])
        cand = jnp.concatenate(slabs, axis=0)
        sc, cp = _topk_rows(cand, cand.shape[0])
        mid = cp < k + sub * (sub - 1)
        ca = jnp.where(cp < k, 0, jnp.where(mid, lax.shift_right_logical(cp, 3) - 1, cp - (k + sub * (sub - 1)) + sub))
        cb = jnp.where(cp < k, cp, jnp.where(mid, cp & (sub - 1), 0))
        e1 = jnp.zeros_like(cp)
        e2 = jnp.zeros_like(cp)
        for a in range(PEER_TOPK):
            e1 = jnp.where(ca == a, i1[a:a + 1, :], e1)
            e2 = jnp.where(cb == a, i2[a:a + 1, :], e2)
        ex = jnp.exp(sc - sc[0:1, :])
        rows = pl.ds(pl.multiple_of(h * PEER_TOPK, PEER_TOPK), PEER_TOPK)
        idx_t[rows, :] = e1 * N_KEYS + e2
        g_t[rows, :] = ex / jnp.sum(ex, axis=0, keepdims=True)
        return carry

    def heads(hh, carry):
        for j in range(ROUTE_HEADS_PER_ITER):
            head(hh * ROUTE_HEADS_PER_ITER + j, carry)
        return carry

    lax.fori_loop(0, PEER_HEADS // ROUTE_HEADS_PER_ITER, heads, 0)
    idx_ref[...] = idx_t[...].T
    g_ref[...] = g_t[...].T


def _route(qp, sk, tm):
    nq, t, _ = qp.shape
    return pl.pallas_call(
        _route_kernel,
        out_shape=[jax.ShapeDtypeStruct((t, PEER_SEL), jnp.int32),
                   jax.ShapeDtypeStruct((t, PEER_SEL), F32)],
        grid=(t // tm,),
        in_specs=[pl.BlockSpec((nq, tm, LANES), lambda i: (0, i, 0)),
                  pl.BlockSpec(sk.shape, lambda i: (0, 0, 0))],
        out_specs=[pl.BlockSpec((tm, PEER_SEL), lambda i: (i, 0))] * 2,
        scratch_shapes=[pltpu.VMEM((PEER_SEL, tm), jnp.int32), pltpu.VMEM((PEER_SEL, tm), F32)],
        compiler_params=_cparams(("parallel",)),
        name="route",
    )(qp, sk)


PEER_TB = 256
PEER_NB = SUBLANES
PEER_STAGED_TB = 16
PEER_NG = 4
PEER_LA = 2
HI16 = -65536


def _gelu(a):
    return 0.5 * a * (1.0 + lax.erf(a * (2.0 ** -0.5)))


def _peer_token(xr, gr, tile):
    nch = D_MODEL // LANES
    nkt = PEER_SEL // SUBLANES
    xb = [jnp.broadcast_to(xr[:, c * LANES:(c + 1) * LANES], (SUBLANES, LANES)) for c in range(nch)]
    parts = []
    for kt in range(nkt):
        acc = None
        for c in range(nch):
            u = lax.bitcast_convert_type(tile(kt, c) & HI16, F32)
            acc = u * xb[c] if acc is None else acc + u * xb[c]
        parts.append(acc)
    part = jnp.concatenate(parts, axis=0)
    a_row = jnp.sum(part.T, axis=0, keepdims=True)
    act = _gelu(a_row) * gr
    act_b = jnp.broadcast_to(act, (LANES, PEER_SEL)).T
    outs = []
    for c in range(nch):
        acc = None
        for kt in range(nkt):
            v = lax.bitcast_convert_type(tile(kt, c) << 16, F32)
            t = act_b[kt * SUBLANES:(kt + 1) * SUBLANES, :] * v
            acc = t if acc is None else acc + t
        outs.append(jnp.sum(acc, axis=0, keepdims=True))
    return jnp.concatenate(outs, axis=1)


def _peer_staged_kernel(st_ref, xn_ref, g_ref, rest_ref, after_ref, out_ref):
    del rest_ref, after_ref
    xg = xn_ref[...]
    gg = g_ref[...]
    rows = []
    for tb in range(xn_ref.shape[0]):
        def tile(kt, c, tb=tb):
            return st_ref[tb, kt * SUBLANES:(kt + 1) * SUBLANES, c * LANES:(c + 1) * LANES]
        rows.append(_peer_token(xg[tb:tb + 1, :], gg[tb:tb + 1, :], tile))
    out_ref[...] = jnp.concatenate(rows, axis=0)


def _peer_staged(staged, xn, g, rest, after=None):
    ts = staged.shape[0]
    t, d = xn.shape
    tb = PEER_STAGED_TB
    return pl.pallas_call(
        _peer_staged_kernel,
        out_shape=jax.ShapeDtypeStruct((t, d), F32),
        grid=(ts // tb,),
        in_specs=[pl.BlockSpec((tb, PEER_SEL, d), lambda i: (i, 0, 0)),
                  pl.BlockSpec((tb, d), lambda i: (i, 0)),
                  pl.BlockSpec((tb, PEER_SEL), lambda i: (i, 0)),
                  pl.BlockSpec(memory_space=pl.ANY),
                  pl.BlockSpec(memory_space=pl.ANY)],
        out_specs=pl.BlockSpec((tb, d), lambda i: (i, 0)),
        input_output_aliases={3: 0},
        compiler_params=_cparams(("parallel",)),
        name="peer_staged",
    )(staged, xn, g, rest, _order_after(after))


SC_CORES = 2
SC_SUBCORES = 16
SC_WORKERS = SC_CORES * SC_SUBCORES
SC_ROWS = 32
SC_IDX_BLK = 128
PEER_SC_SHARE_NUM = 11
PEER_SC_SHARE_DEN = 16
PEER_SC_QUANTUM = SC_WORKERS * SC_ROWS * SC_IDX_BLK // PEER_SEL


def _stage_rows(uv2, idx):
    n_tok, nsel = idx.shape
    n_rows = n_tok * nsel
    rows_per_w = n_rows // SC_WORKERS
    n_blk = rows_per_w // (SC_ROWS * SC_IDX_BLK)
    idx4 = idx.reshape(SC_WORKERS, n_blk, SC_IDX_BLK, SC_ROWS)
    mesh = plsc.VectorSubcoreMesh(core_axis_name="c", subcore_axis_name="s")

    @functools.partial(
        pl.kernel, mesh=mesh,
        out_type=jax.ShapeDtypeStruct((n_rows, uv2.shape[1]), uv2.dtype),
        scratch_types=[
            pltpu.VMEM((SC_IDX_BLK, SC_ROWS), jnp.int32),
            pltpu.VMEM((2, SC_ROWS, uv2.shape[1]), uv2.dtype),
            pltpu.SemaphoreType.DMA((2,)),
            pltpu.SemaphoreType.DMA((2,)),
        ],
    )
    def stage(uv_hbm, idx_hbm, out_hbm, idx_v, rows_v, gsem, wsem):
        wid = lax.axis_index("s") * SC_CORES + lax.axis_index("c")
        base = wid * rows_per_w

        def gather(j, b):
            return pltpu.make_async_copy(uv_hbm.at[idx_v.at[j]], rows_v.at[b], gsem.at[b])

        def write(row0, b):
            return pltpu.make_async_copy(rows_v.at[b], out_hbm.at[pl.ds(row0, SC_ROWS)], wsem.at[b])

        @pl.loop(0, n_blk)
        def _(blk):
            pltpu.sync_copy(idx_hbm.at[wid, blk], idx_v)
            blk_row0 = base + blk * (SC_IDX_BLK * SC_ROWS)
            gather(0, 0).start()

            @pl.loop(0, SC_IDX_BLK, step=2)
            def _(j):
                for b in range(2):
                    jj = j + b
                    row0 = pl.multiple_of(blk_row0 + jj * SC_ROWS, SC_ROWS)
                    gather(jj, b).wait()

                    @pl.when(jj > 0)
                    def _():
                        write(row0 - SC_ROWS, 1 - b).wait()

                    @pl.when(jj + 1 < SC_IDX_BLK)
                    def _():
                        gather(jj + 1, 1 - b).start()

                    write(row0, b).start()

            write(blk_row0 + (SC_IDX_BLK - 1) * SC_ROWS, 1).wait()

    return stage(uv2, idx4)


def _peer_kernel(step0, idx_hbm, xn_ref, g_ref, uv_hbm, after_ref, out_ref, idx_s, buf, sem_i, sem):
    del after_ref
    step = pl.program_id(0) + step0
    nbatch = xn_ref.shape[0] // PEER_NB
    nch = D_MODEL // LANES
    nkt = PEER_SEL // SUBLANES

    cp = pltpu.make_async_copy(idx_hbm.at[step], idx_s, sem_i.at[0])
    cp.start()
    cp.wait()

    def issue(b):
        grp = b & (PEER_NG - 1)

        def tok_body(tb, carry):
            base = (b * PEER_NB + tb) * PEER_SEL
            for k in range(PEER_SEL):
                e = idx_s[base + k]
                pltpu.make_async_copy(
                    uv_hbm.at[e], buf.at[grp, tb, k // SUBLANES, :, k % SUBLANES, :], sem.at[grp]
                ).start(priority=k % 2)
            return carry

        lax.fori_loop(0, PEER_NB, tok_body, 0)

    def wait(b):
        grp = b & (PEER_NG - 1)
        pltpu.make_async_copy(buf.at[grp], buf.at[grp], sem.at[grp]).wait()

    def compute(b):
        grp = b & (PEER_NG - 1)
        r0 = pl.multiple_of(b * PEER_NB, PEER_NB)
        xg = xn_ref[pl.ds(r0, PEER_NB), :]
        gg = g_ref[pl.ds(r0, PEER_NB), :]
        rows = [_peer_token(xg[tb:tb + 1, :], gg[tb:tb + 1, :],
                            functools.partial(lambda tb, kt, c: buf[grp, tb, kt, c], tb))
                for tb in range(PEER_NB)]
        out_ref[pl.ds(r0, PEER_NB), :] = jnp.concatenate(rows, axis=0)

    for b in range(PEER_LA):
        issue(b)

    def body(b, carry):
        wait(b)

        @pl.when(b + PEER_LA < nbatch)
        def _():
            issue(b + PEER_LA)

        compute(b)
        return carry

    lax.fori_loop(0, nbatch, body, 0)


def _order_after(after):
    return jnp.zeros((SUBLANES, LANES), F32) if after is None else after


def _peer(idx, xn, g, uv, ts, after=None):
    t, d = xn.shape
    tb = PEER_TB
    idx2 = idx.reshape(t // tb, tb * PEER_SEL)
    s0 = ts // tb
    return pl.pallas_call(
        functools.partial(_peer_kernel, s0),
        out_shape=jax.ShapeDtypeStruct((t, d), F32),
        grid=((t - ts) // tb,),
        in_specs=[pl.BlockSpec(memory_space=pl.ANY),
                  pl.BlockSpec((tb, d), lambda i: (i + s0, 0)),
                  pl.BlockSpec((tb, PEER_SEL), lambda i: (i + s0, 0)),
                  pl.BlockSpec(memory_space=pl.ANY),
                  pl.BlockSpec(memory_space=pl.ANY)],
        out_specs=pl.BlockSpec((tb, d), lambda i: (i + s0, 0)),
        scratch_shapes=[pltpu.SMEM((tb * PEER_SEL,), jnp.int32),
                        pltpu.VMEM((PEER_NG, PEER_NB, PEER_SEL // SUBLANES, d // LANES, SUBLANES, LANES),
                                   jnp.int32),
                        pltpu.SemaphoreType.DMA((1,)),
                        pltpu.SemaphoreType.DMA((PEER_NG,))],
        compiler_params=_cparams(("arbitrary",)),
        name="peer",
    )(idx2, xn, g, uv, _order_after(after))


def _ple_kernel(x1_ref, pe_ref, p_ref, nw_ref, wg_ref, wp_ref, nf_ref, y_ref):
    x2 = x1_ref[...] + pe_ref[...]
    gate = _sigmoid(_dot(_rms(x2, nw_ref[...]).astype(BF16), wg_ref[...]))
    x3 = x2 + _dot(p_ref[...].astype(BF16), wp_ref[...]) * gate
    y_ref[...] = _rms(x3, nf_ref[...])


def _ple(x1, pe, p, nw, wg, wp, nf, tm):
    t, d = x1.shape
    full = lambda shape: pl.BlockSpec(shape, lambda i: (0,) * len(shape))
    tok = lambda w: pl.BlockSpec((tm, w), lambda i: (i, 0))
    return pl.pallas_call(
        _ple_kernel,
        out_shape=jax.ShapeDtypeStruct((t, d), F32),
        grid=(t // tm,),
        in_specs=[tok(d), tok(d), tok(PLE_DIM), full(nw.shape), full(wg.shape), full(wp.shape),
                  full(nf.shape)],
        out_specs=tok(d),
        compiler_params=_cparams(("parallel",)),
        name="ple",
    )(x1, pe, p, nw, wg, wp, nf)


def _rope_tables(s):
    half = ROT_DIM // 2
    inv = ROPE_THETA ** (-jnp.arange(0, ROT_DIM, 2, dtype=F32) / ROT_DIM)
    ang = jnp.arange(s, dtype=F32)[:, None] * inv[None, :]
    cos, sin = jnp.cos(ang), jnp.sin(ang)
    ones = jnp.ones((s, ATT_QK_DIM - ROT_DIM), F32)
    zeros = jnp.zeros((s, ATT_QK_DIM - ROT_DIM), F32)
    zh = jnp.zeros((s, half), F32)
    c64 = jnp.concatenate([cos, cos, ones], axis=1)
    s1_64 = jnp.concatenate([zh, sin, zeros], axis=1)
    s2_64 = jnp.concatenate([-sin, zh, zeros], axis=1)
    rep = LANES // ATT_QK_DIM
    return jnp.tile(c64, (1, rep)), jnp.tile(s1_64, (1, rep)), jnp.tile(s2_64, (1, rep))


def _pack_uv(u, v):
    ub = lax.bitcast_convert_type(u.astype(BF16), jnp.uint16).astype(jnp.uint32)
    vb = lax.bitcast_convert_type(v.astype(BF16), jnp.uint16).astype(jnp.uint32)
    return lax.bitcast_convert_type((ub << 16) | vb, jnp.int32)


def _mix_route(x, w):
    b, s, d = x.shape
    t = b * s
    nc = s // M_CHUNK
    rc, rs1, rs2 = _rope_tables(s)
    aq, ak, av, mq, mk, mv, mo, gc, gr = _inproj(
        x, w["norm_mix"], w["wqk"], w["wav"], w["wmqk"], w["wmvo"], w["wg"], w["wgt"], w["gb"],
        w["gbt"], w["conv"], rc, rs1, rs2, tm=512)
    att = _attn(w["lq"], aq, ak, av, w["attn_gain"], tq=256)
    gc5 = gc.reshape(b, nc, M_CHUNK, M_HEADS, 4).transpose(0, 3, 1, 2, 4)
    gr5 = gr.reshape(b, M_HEADS, 4, nc, M_CHUNK).transpose(0, 1, 3, 2, 4)
    mem = _mlstm(mq, mk, mv, mo, gc5, gr5, w["mlstm_gain"])
    x1, xn, qp = _outproj(x.reshape(t, d), att.reshape(t, ATT_W), mem.reshape(t, M_W),
                          w["wo_a"], w["wo_m"], w["norm_ffn"], w["wq"], tm=512)
    idx, g = _route(qp, w["sk"], tm=128)
    return x1, xn, idx, g


def _peer_all(groups, uv2):
    uv3 = uv2.reshape(-1, D_MODEL // LANES, LANES)
    staged, outs = [], []
    prev = None
    for xn, idx, g in groups:
        t, d = xn.shape
        ts = (t * PEER_SC_SHARE_NUM // PEER_SC_SHARE_DEN) // PEER_SC_QUANTUM * PEER_SC_QUANTUM
        assert 0 < ts < t and (t - ts) % PEER_TB == 0, (t, ts)
        staged.append(_stage_rows(uv2, idx[:ts]).reshape(ts, PEER_SEL, d))
        prev = _peer(idx, xn, g, uv3, ts, after=prev)
        outs.append(prev)
    for i, (xn, idx, g) in enumerate(groups):
        prev = _peer_staged(staged[i], xn, g, outs[i], after=prev)
        outs[i] = prev
    return outs


def kernel(x_prompt, x_sample, p_prompt, p_sample, norm_mix, w_in, gate_bias, conv_qk, lambda_qk,
           attn_gain, mlstm_gain, w_out, norm_ffn, peer_wq, peer_subkeys, peer_u, peer_v, norm_ple,
           ple_w_gate, ple_w_proj, norm_final):
    li = 0
    win = w_in[li]
    c0 = 0
    cols = {}
    for name, width in (("aq", ATT_W), ("ak", ATT_W), ("av", ATT_W), ("mq", M_W), ("mk", M_W),
                        ("mv", M_W), ("mo", M_W), ("gt", N_GATES)):
        cols[name] = win[:, c0:c0 + width]
        c0 += width
    perm = jnp.array([ty * M_HEADS + h for h in range(M_HEADS) for ty in range(4)], jnp.int32)
    wg = cols["gt"][:, perm].astype(BF16)
    gb = gate_bias[li][perm]
    w = {
        "norm_mix": norm_mix[li][None, :],
        "wqk": jnp.concatenate([cols["aq"], cols["ak"]], axis=1).astype(BF16),
        "wav": cols["av"].astype(BF16),
        "wmqk": jnp.concatenate([cols["mq"], cols["mk"]], axis=1).astype(BF16),
        "wmvo": jnp.concatenate([cols["mv"], cols["mo"]], axis=1).astype(BF16),
        "wg": wg,
        "wgt": wg.T,
        "gb": gb[None, :],
        "gbt": gb[:, None],
        "conv": conv_qk[li],
        "lq": lambda_qk[li],
        "attn_gain": attn_gain[li][None, :],
        "mlstm_gain": mlstm_gain[li][None, :],
        "wo_a": w_out[li][:ATT_W].astype(BF16),
        "wo_m": w_out[li][ATT_W:].astype(BF16),
        "norm_ffn": norm_ffn[li][None, :],
        "wq": peer_wq[li].astype(BF16),
        "sk": peer_subkeys[li].reshape(PEER_HEADS * 2, N_KEYS, PEER_HALF).astype(BF16),
        "uv": _pack_uv(peer_u[li], peer_v[li]),
        "norm_ple": norm_ple[li][None, :],
        "ple_wg": ple_w_gate[li].astype(BF16),
        "ple_wp": ple_w_proj[li].astype(BF16),
        "norm_final": norm_final[None, :],
    }
    hb = x_prompt.shape[0] // 2
    xs = (x_prompt[:hb], x_prompt[hb:], x_sample)
    ps = (p_prompt[li][:hb], p_prompt[li][hb:], p_sample[li])
    mixed = [_mix_route(x, w) for x in xs]
    pes = _peer_all([(xn, idx, g) for _, xn, idx, g in mixed], w["uv"])
    ys = []
    for x, p, (x1, _, _, _), pe in zip(xs, ps, mixed, pes):
        y = _ple(x1, pe, p.reshape(-1, PLE_DIM), w["norm_ple"], w["ple_wg"], w["ple_wp"],
                 w["norm_final"], tm=512)
        ys.append(y.reshape(x.shape))
    return (jnp.concatenate(ys[:2], axis=0), ys[2])
```

```python
import functools
import math

import jax
import jax.numpy as jnp
from jax import lax
from jax.experimental import pallas as pl
from jax.experimental.pallas import tpu as pltpu
from jax.experimental.pallas import tpu_sc as plsc

F32 = jnp.float32
BF16 = jnp.bfloat16

D_MODEL = 1024
ATT_W = 512
ATT_HEADS = 4
ATT_V_DIM = 128
ATT_QK_DIM = 64
ROT_DIM = 16
ROPE_THETA = 500000.0
M_W = 512
M_HEADS = 4
M_HEAD_DIM = 128
M_CHUNK = 128
CONV_K = 5
N_GATES = 16
PEER_HEADS = 8
PEER_QDIM = 256
PEER_HALF = 128
N_KEYS = 128
PEER_TOPK = 16
PEER_SEL = PEER_HEADS * PEER_TOPK
PLE_DIM = 256
EPS = 1e-6
LAM_INIT = 0.8 - 0.6 * math.exp(-0.3 * 0)
Q_SCALE = ATT_QK_DIM ** -0.5 * math.log2(math.e)

LANES = 128
SUBLANES = 8
VMEM_LIMIT = 56 * 1024 * 1024

NEG_INF = float("-inf")


def _cparams(sem):
    return pltpu.CompilerParams(dimension_semantics=sem, vmem_limit_bytes=VMEM_LIMIT)


def _dot(a, b):
    return jnp.dot(a, b, preferred_element_type=F32)


def _dot_nt(a, b):
    return lax.dot_general(a, b, (((1,), (1,)), ((), ())), preferred_element_type=F32)


def _rms(x, w):
    ms = jnp.mean(x * x, axis=-1, keepdims=True)
    return (x * lax.rsqrt(ms + EPS)) * w


def _sigmoid(x):
    return 1.0 / (1.0 + jnp.exp(-x))


def _log_sigmoid(x):
    return jnp.minimum(x, 0.0) - jnp.log(1.0 + jnp.exp(-jnp.abs(x)))


def _inproj_kernel(x_ref, xp_ref, xn_ref, nw_ref, wqk_ref, wav_ref, wmqk_ref, wmvo_ref,
                   wg_ref, wgt_ref, gb_ref, gbt_ref, cw_ref, rc_ref, rs1_ref, rs2_ref,
                   aq_ref, ak_ref, av_ref, mq_ref, mk_ref, mv_ref, mo_ref, gc_ref, gr_ref):
    i = pl.program_id(1)
    last = pl.num_programs(1) - 1
    nw = nw_ref[...]
    tm = x_ref.shape[1]

    h = _rms(x_ref[0], nw).astype(BF16)
    hp = _rms(xp_ref[0, 0], nw).astype(BF16)
    hn = _rms(xn_ref[0, 0], nw).astype(BF16)

    qk = _dot(h, wqk_ref[...])
    rc = rc_ref[...]
    rs1 = rs1_ref[...]
    rs2 = rs2_ref[...]
    half = ROT_DIM // 2
    for c in range(2 * ATT_W // LANES):
        xc = qk[:, c * LANES:(c + 1) * LANES]
        r = xc * rc + pltpu.roll(xc, half, 1) * rs1 + pltpu.roll(xc, LANES - half, 1) * rs2
        if c < ATT_W // LANES:
            aq_ref[0, :, c * LANES:(c + 1) * LANES] = (r * Q_SCALE).astype(BF16)
        else:
            cc = c - ATT_W // LANES
            ak_ref[0, :, cc * LANES:(cc + 1) * LANES] = r.astype(BF16)

    av_ref[0] = _dot(h, wav_ref[...]).astype(BF16)

    wm = wmqk_ref[...]
    pm = _dot(h, wm)
    pp = _dot(hp, wm) * (i > 0).astype(F32)
    pn = _dot(hn, wm) * (i < last).astype(F32)
    ext = jnp.concatenate([pp, pm, pn], axis=0)
    cw = cw_ref[...]
    off = SUBLANES - CONV_K // 2
    conv = cw[0:1, :] * ext[off:off + tm, :]
    for j in range(1, CONV_K):
        conv = conv + cw[j:j + 1, :] * ext[off + j:off + j + tm, :]
    act = conv * _sigmoid(conv)
    mq_ref[0] = act[:, :M_W].astype(BF16)
    mk_ref[0] = (act[:, M_W:] * (M_HEAD_DIM ** -0.5)).astype(BF16)

    vo = _dot(h, wmvo_ref[...])
    mv_ref[0] = vo[:, :M_W].astype(BF16)
    mo_ref[0] = vo[:, M_W:].astype(BF16)

    gc_ref[0] = _dot(h, wg_ref[...]) + gb_ref[...]
    gr_ref[0] = _dot_nt(wgt_ref[...], h) + gbt_ref[...]


def _inproj(x, norm_w, wqk, wav, wmqk, wmvo, wg, wgt, gb, gbt, conv_w, rc, rs1, rs2, tm):
    b, s, d = x.shape
    nt = s // tm
    x8 = x.reshape(b, s // SUBLANES, SUBLANES, d)
    r8 = tm // SUBLANES
    nb8 = s // SUBLANES
    full = lambda shape: pl.BlockSpec(shape, lambda bi, i: (0,) * len(shape))
    tok = lambda w: pl.BlockSpec((1, tm, w), lambda bi, i: (bi, i, 0))
    out_shapes = (
        [jax.ShapeDtypeStruct((b, s, ATT_W), BF16)] * 3
        + [jax.ShapeDtypeStruct((b, s, M_W), BF16)] * 4
        + [jax.ShapeDtypeStruct((b, s, N_GATES), F32), jax.ShapeDtypeStruct((b, N_GATES, s), F32)]
    )
    return pl.pallas_call(
        _inproj_kernel,
        out_shape=out_shapes,
        grid=(b, nt),
        in_specs=[
            tok(d),
            pl.BlockSpec((1, 1, SUBLANES, d), lambda bi, i: (bi, jnp.maximum(i * r8 - 1, 0), 0, 0)),
            pl.BlockSpec((1, 1, SUBLANES, d), lambda bi, i: (bi, jnp.minimum((i + 1) * r8, nb8 - 1), 0, 0)),
            full((1, d)),
            full(wqk.shape), full(wav.shape), full(wmqk.shape), full(wmvo.shape),
            full(wg.shape), full(wgt.shape), full(gb.shape), full(gbt.shape), full(conv_w.shape),
            pl.BlockSpec((tm, LANES), lambda bi, i: (i, 0)),
            pl.BlockSpec((tm, LANES), lambda bi, i: (i, 0)),
            pl.BlockSpec((tm, LANES), lambda bi, i: (i, 0)),
        ],
        out_specs=[tok(ATT_W)] * 3 + [tok(M_W)] * 4
        + [tok(N_GATES), pl.BlockSpec((1, N_GATES, tm), lambda bi, i: (bi, 0, i))],
        compiler_params=_cparams(("parallel", "parallel")),
        name="inproj",
    )(x, x8, x8, norm_w, wqk, wav, wmqk, wmvo, wg, wgt, gb, gbt, conv_w, rc, rs1, rs2)


ATT_KEY_BLOCK = 1024


def _attn_kernel(lq_ref, q_ref, k_ref, v_ref, gain_ref, o_ref):
    lq = lq_ref[...]
    lam = (jnp.exp(jnp.sum(lq[0:1] * lq[1:2], axis=1, keepdims=True))
           - jnp.exp(jnp.sum(lq[2:3] * lq[3:4], axis=1, keepdims=True)) + LAM_INIT)
    q = q_ref[0]
    tq = q.shape[0]
    lane = lax.broadcasted_iota(jnp.int32, q.shape, 1)
    zero = jnp.zeros_like(q)
    qq = jnp.concatenate([jnp.where(lane < ATT_QK_DIM, q, zero),
                          jnp.where(lane >= ATT_QK_DIM, q, zero)], axis=0)
    m = jnp.full((2 * tq, 1), NEG_INF, F32)
    ln = jnp.zeros((2 * tq, 1), F32)
    acc = jnp.zeros((2 * tq, ATT_V_DIM), F32)
    kb = min(ATT_KEY_BLOCK, k_ref.shape[1])
    assert k_ref.shape[1] % kb == 0
    for c in range(k_ref.shape[1] // kb):
        ks = slice(c * kb, (c + 1) * kb)
        sc = _dot_nt(qq, k_ref[0, ks, :])
        m_new = jnp.maximum(m, jnp.max(sc, axis=1, keepdims=True))
        alpha = jnp.exp2(m - m_new)
        e = jnp.exp2(sc - m_new)
        ln = alpha * ln + jnp.sum(e, axis=1, keepdims=True)
        acc = alpha * acc + _dot(e.astype(BF16), v_ref[0, ks, :])
        m = m_new
    o = acc[:tq] * (1.0 / ln[:tq]) - acc[tq:] * (lam / ln[tq:])
    o_ref[0] = (_rms(o, gain_ref[...]) * (1.0 - LAM_INIT)).astype(BF16)


def _attn(lq, q, k, v, gain, tq):
    b, s, _ = q.shape
    return pl.pallas_call(
        _attn_kernel,
        out_shape=jax.ShapeDtypeStruct((b, s, ATT_W), BF16),
        grid=(b, ATT_HEADS, s // tq),
        in_specs=[
            pl.BlockSpec(lq.shape, lambda bi, h, i: (0, 0)),
            pl.BlockSpec((1, tq, ATT_V_DIM), lambda bi, h, i: (bi, i, h)),
            pl.BlockSpec((1, s, ATT_V_DIM), lambda bi, h, i: (bi, 0, h)),
            pl.BlockSpec((1, s, ATT_V_DIM), lambda bi, h, i: (bi, 0, h)),
            pl.BlockSpec((1, ATT_V_DIM), lambda bi, h, i: (0, 0)),
        ],
        out_specs=pl.BlockSpec((1, tq, ATT_V_DIM), lambda bi, h, i: (bi, i, h)),
        compiler_params=_cparams(("parallel", "parallel", "parallel")),
        name="attn",
    )(lq, q, k, v, gain)


def _mlstm_chunk(q, k, v, ic_col, lf_col, ic_row, lf_row, cst, n, m, fwd):
    ln = q.shape[0]
    row = lax.broadcasted_iota(jnp.int32, (ln, ln), 0)
    col = lax.broadcasted_iota(jnp.int32, (ln, ln), 1)
    lower = col <= row
    upper = col >= row
    mask, mask_t = (lower, upper) if fwd else (upper, lower)
    bcum_col = jnp.sum(jnp.where(mask, lf_row, 0.0), axis=1, keepdims=True)
    bcum_row = jnp.sum(jnp.where(mask_t, lf_col, 0.0), axis=0, keepdims=True)
    dlog = jnp.where(mask, bcum_col - bcum_row + ic_row, NEG_INF)
    inter = bcum_col + m
    m_t = jnp.maximum(inter, jnp.max(dlog, axis=1, keepdims=True))
    w = jnp.exp(dlog - m_t)
    sc = jnp.exp(inter - m_t)
    wqk = w * _dot_nt(q, k)
    num = sc * _dot(q, cst.astype(BF16)) + _dot(wqk.astype(BF16), v)
    qn = jnp.sum(q.astype(F32) * n, axis=1, keepdims=True)
    den = sc * qn + jnp.sum(wqk, axis=1, keepdims=True)
    h = num / jnp.maximum(jnp.abs(den), jnp.exp(-m_t))
    b_last = jnp.sum(lf_row, axis=1, keepdims=True)
    rel_row = b_last - bcum_row + ic_row
    rel_col = b_last - bcum_col + ic_col
    m_new = jnp.maximum(b_last + m, jnp.max(rel_row, axis=1, keepdims=True))
    dec_col = jnp.exp(rel_col - m_new)
    keep = jnp.exp(b_last + m - m_new)
    kd = k.astype(F32) * dec_col
    c_new = keep * cst + lax.dot_general(kd.astype(BF16), v, (((0,), (0,)), ((), ())),
                                         preferred_element_type=F32)
    n_new = keep * n + jnp.sum(kd, axis=0, keepdims=True)
    return h, c_new, n_new, m_new


M_HEADS_PER_STEP = 1


def _mlstm_kernel(q_ref, k_ref, v_ref, o_ref, gc_ref, gr_ref, gain_ref, out_ref, hf_ref, hb_ref):
    s = q_ref.shape[1]
    nc = s // M_CHUNK
    d = M_HEAD_DIM
    heads = range(M_HEADS_PER_STEP)

    def step(j, c, carry, fwd):
        cst, n, m = carry
        sl = pl.ds(pl.multiple_of(c * M_CHUNK, M_CHUNK), M_CHUNK)
        ln = slice(j * d, (j + 1) * d)
        gc = gc_ref[0, j, c]
        gr = gr_ref[0, j, c]
        t = 0 if fwd else 2
        h, cst, n, m = _mlstm_chunk(
            q_ref[0, sl, ln], k_ref[0, sl, ln], v_ref[0, sl, ln],
            gc[:, t:t + 1], _log_sigmoid(gc[:, t + 1:t + 2]),
            gr[t:t + 1, :], _log_sigmoid(gr[t + 1:t + 2, :]),
            cst, n, m, fwd)
        return sl, ln, h, (cst, n, m)

    init = (jnp.zeros((d, d), F32), jnp.zeros((1, d), F32), jnp.zeros((1, 1), F32))

    def scan_body(c, carry):
        new = []
        for j in heads:
            cf, cb = carry[j]
            sl, ln, h, cf = step(j, c, cf, True)
            hf_ref[sl, ln] = h
            sl, ln, h, cb = step(j, nc - 1 - c, cb, False)
            hb_ref[sl, ln] = h
            new.append((cf, cb))
        return tuple(new)

    lax.fori_loop(0, nc, scan_body, tuple((init, init) for _ in heads))

    def gate_body(c, carry):
        sl = pl.ds(pl.multiple_of(c * M_CHUNK, M_CHUNK), M_CHUNK)
        for j in heads:
            ln = slice(j * d, (j + 1) * d)
            y = _sigmoid(o_ref[0, sl, ln].astype(F32)) * (hf_ref[sl, ln] + hb_ref[sl, ln])
            out_ref[0, sl, ln] = _rms(y, gain_ref[:, ln]).astype(BF16)
        return carry

    lax.fori_loop(0, nc, gate_body, 0)


def _mlstm(q, k, v, o, gc, gr, gain):
    b, s, _ = q.shape
    nc = s // M_CHUNK
    hp = M_HEADS_PER_STEP
    w = hp * M_HEAD_DIM
    head = pl.BlockSpec((1, s, w), lambda bi, h: (bi, 0, h))
    return pl.pallas_call(
        _mlstm_kernel,
        out_shape=jax.ShapeDtypeStruct((b, s, M_W), BF16),
        grid=(b, M_HEADS // hp),
        in_specs=[
            head, head, head, head,
            pl.BlockSpec((1, hp, nc, M_CHUNK, 4), lambda bi, h: (bi, h, 0, 0, 0)),
            pl.BlockSpec((1, hp, nc, 4, M_CHUNK), lambda bi, h: (bi, h, 0, 0, 0)),
            pl.BlockSpec((1, w), lambda bi, h: (0, h)),
        ],
        out_specs=head,
        scratch_shapes=[pltpu.VMEM((s, w), F32), pltpu.VMEM((s, w), F32)],
        compiler_params=_cparams(("parallel", "parallel")),
        name="mlstm",
    )(q, k, v, o, gc, gr, gain)


def _outproj_kernel(x_ref, att_ref, mem_ref, wa_ref, wm_ref, nw_ref, wq_ref, x1_ref, xn_ref, qp_ref):
    x1 = x_ref[...] + _dot(att_ref[...], wa_ref[...]) + _dot(mem_ref[...], wm_ref[...])
    x1_ref[...] = x1
    xn = _rms(x1, nw_ref[...])
    xn_ref[...] = xn
    qp = _dot(xn.astype(BF16), wq_ref[...])
    for c in range(qp_ref.shape[0]):
        qp_ref[c] = qp[:, c * LANES:(c + 1) * LANES].astype(BF16)


def _outproj(x, att, mem, wa, wm, nw, wq, tm):
    t, d = x.shape
    nq = wq.shape[1] // LANES
    full = lambda shape: pl.BlockSpec(shape, lambda i: (0,) * len(shape))
    tok = lambda w: pl.BlockSpec((tm, w), lambda i: (i, 0))
    return pl.pallas_call(
        _outproj_kernel,
        out_shape=[jax.ShapeDtypeStruct((t, d), F32), jax.ShapeDtypeStruct((t, d), F32),
                   jax.ShapeDtypeStruct((nq, t, LANES), BF16)],
        grid=(t // tm,),
        in_specs=[tok(d), tok(ATT_W), tok(M_W), full(wa.shape), full(wm.shape), full(nw.shape),
                  full(wq.shape)],
        out_specs=[tok(d), tok(d), pl.BlockSpec((nq, tm, LANES), lambda i: (0, i, 0))],
        compiler_params=_cparams(("parallel",)),
        name="outproj",
    )(x, att, mem, wa, wm, nw, wq)


ROUTE_HEADS_PER_ITER = 4


def _topk_rows(sc, n):
    pos = lax.broadcasted_iota(jnp.int32, sc.shape, 0).astype(F32)
    vals, poss = [], []
    for _ in range(PEER_TOPK):
        m = jnp.max(sc, axis=0, keepdims=True)
        p = jnp.min(jnp.where(sc == m, pos, float(n)), axis=0, keepdims=True)
        vals.append(m)
        poss.append(p)
        sc = jnp.where(pos == p, NEG_INF, sc)
    return jnp.concatenate(vals, axis=0), jnp.concatenate(poss, axis=0).astype(jnp.int32)


def _route_kernel(qp_ref, sk_ref, idx_ref, g_ref, idx_t, g_t):
    def head(h, carry):
        s1 = _dot_nt(sk_ref[2 * h], qp_ref[2 * h])
        s2 = _dot_nt(sk_ref[2 * h + 1], qp_ref[2 * h + 1])
        v1, i1 = _topk_rows(s1, N_KEYS)
        v2, i2 = _topk_rows(s2, N_KEYS)
        k = PEER_TOPK
        sub = SUBLANES
        brow = lax.broadcasted_iota(jnp.int32, (sub, v2.shape[1]), 0)
        slabs = [v1[0:1, :] + v2]
        for a in range(1, sub):
            slabs.append(jnp.where(brow < k // (a + 1), v1[a:a + 1, :] + v2[0:sub, :], NEG_INF))
        slabs.append(v1[sub:k, :] + v2[0:1, :])
        cand = jnp.concatenate(slabs, axis=0)
        sc, cp = _topk_rows(cand, cand.shape[0])
        mid = cp < k + sub * (sub - 1)
        ca = jnp.where(cp < k, 0, jnp.where(mid, lax.shift_right_logical(cp, 3) - 1, cp - (k + sub * (sub - 1)) + sub))
        cb = jnp.where(cp < k, cp, jnp.where(mid, cp & (sub - 1), 0))
        e1 = jnp.zeros_like(cp)
        e2 = jnp.zeros_like(cp)
        for a in range(PEER_TOPK):
            e1 = jnp.where(ca == a, i1[a:a + 1, :], e1)
            e2 = jnp.where(cb == a, i2[a:a + 1, :], e2)
        ex = jnp.exp(sc - sc[0:1, :])
        rows = pl.ds(pl.multiple_of(h * PEER_TOPK, PEER_TOPK), PEER_TOPK)
        idx_t[rows, :] = e1 * N_KEYS + e2
        g_t[rows, :] = ex / jnp.sum(ex, axis=0, keepdims=True)
        return carry

    def heads(hh, carry):
        for j in range(ROUTE_HEADS_PER_ITER):
            head(hh * ROUTE_HEADS_PER_ITER + j, carry)
        return carry

    lax.fori_loop(0, PEER_HEADS // ROUTE_HEADS_PER_ITER, heads, 0)
    idx_ref[...] = idx_t[...].T
    g_ref[...] = g_t[...].T


def _route(qp, sk, tm):
    nq, t, _ = qp.shape
    return pl.pallas_call(
        _route_kernel,
        out_shape=[jax.ShapeDtypeStruct((t, PEER_SEL), jnp.int32),
                   jax.ShapeDtypeStruct((t, PEER_SEL), F32)],
        grid=(t // tm,),
        in_specs=[pl.BlockSpec((nq, tm, LANES), lambda i: (0, i, 0)),
                  pl.BlockSpec(sk.shape, lambda i: (0, 0, 0))],
        out_specs=[pl.BlockSpec((tm, PEER_SEL), lambda i: (i, 0))] * 2,
        scratch_shapes=[pltpu.VMEM((PEER_SEL, tm), jnp.int32), pltpu.VMEM((PEER_SEL, tm), F32)],
        compiler_params=_cparams(("parallel",)),
        name="route",
    )(qp, sk)


PEER_TB = 256
PEER_NB = SUBLANES
PEER_STAGED_TB = 16
PEER_NG = 4
PEER_LA = 2
HI16 = -65536


def _gelu(a):
    return 0.5 * a * (1.0 + lax.erf(a * (2.0 ** -0.5)))


def _peer_token(xr, gr, tile):
    nch = D_MODEL // LANES
    nkt = PEER_SEL // SUBLANES
    xb = [jnp.broadcast_to(xr[:, c * LANES:(c + 1) * LANES], (SUBLANES, LANES)) for c in range(nch)]
    parts = []
    for kt in range(nkt):
        acc = None
        for c in range(nch):
            u = lax.bitcast_convert_type(tile(kt, c) & HI16, F32)
            acc = u * xb[c] if acc is None else acc + u * xb[c]
        parts.append(acc)
    part = jnp.concatenate(parts, axis=0)
    a_row = jnp.sum(part.T, axis=0, keepdims=True)
    act = _gelu(a_row) * gr
    act_b = jnp.broadcast_to(act, (LANES, PEER_SEL)).T
    outs = []
    for c in range(nch):
        acc = None
        for kt in range(nkt):
            v = lax.bitcast_convert_type(tile(kt, c) << 16, F32)
            t = act_b[kt * SUBLANES:(kt + 1) * SUBLANES, :] * v
            acc = t if acc is None else acc + t
        outs.append(jnp.sum(acc, axis=0, keepdims=True))
    return jnp.concatenate(outs, axis=1)


def _peer_staged_kernel(st_ref, xn_ref, g_ref, rest_ref, after_ref, out_ref):
    del rest_ref, after_ref
    xg = xn_ref[...]
    gg = g_ref[...]
    rows = []
    for tb in range(xn_ref.shape[0]):
        def tile(kt, c, tb=tb):
            return st_ref[tb, kt * SUBLANES:(kt + 1) * SUBLANES, c * LANES:(c + 1) * LANES]
        rows.append(_peer_token(xg[tb:tb + 1, :], gg[tb:tb + 1, :], tile))
    out_ref[...] = jnp.concatenate(rows, axis=0)


def _peer_staged(staged, xn, g, rest, after=None):
    ts = staged.shape[0]
    t, d = xn.shape
    tb = PEER_STAGED_TB
    return pl.pallas_call(
        _peer_staged_kernel,
        out_shape=jax.ShapeDtypeStruct((t, d), F32),
        grid=(ts // tb,),
        in_specs=[pl.BlockSpec((tb, PEER_SEL, d), lambda i: (i, 0, 0)),
                  pl.BlockSpec((tb, d), lambda i: (i, 0)),
                  pl.BlockSpec((tb, PEER_SEL), lambda i: (i, 0)),
                  pl.BlockSpec(memory_space=pl.ANY),
                  pl.BlockSpec(memory_space=pl.ANY)],
        out_specs=pl.BlockSpec((tb, d), lambda i: (i, 0)),
        input_output_aliases={3: 0},
        compiler_params=_cparams(("parallel",)),
        name="peer_staged",
    )(staged, xn, g, rest, _order_after(after))


SC_CORES = 2
SC_SUBCORES = 16
SC_WORKERS = SC_CORES * SC_SUBCORES
SC_ROWS = 32
SC_IDX_BLK = 128
PEER_SC_SHARE_NUM = 12
PEER_SC_SHARE_DEN = 16
PEER_SC_QUANTUM = SC_WORKERS * SC_ROWS * SC_IDX_BLK // PEER_SEL


def _stage_rows(uv2, idx):
    n_tok, nsel = idx.shape
    n_rows = n_tok * nsel
    rows_per_w = n_rows // SC_WORKERS
    n_blk = rows_per_w // (SC_ROWS * SC_IDX_BLK)
    idx4 = idx.reshape(SC_WORKERS, n_blk, SC_IDX_BLK, SC_ROWS)
    mesh = plsc.VectorSubcoreMesh(core_axis_name="c", subcore_axis_name="s")

    @functools.partial(
        pl.kernel, mesh=mesh,
        out_type=jax.ShapeDtypeStruct((n_rows, uv2.shape[1]), uv2.dtype),
        scratch_types=[
            pltpu.VMEM((SC_IDX_BLK, SC_ROWS), jnp.int32),
            pltpu.VMEM((2, SC_ROWS, uv2.shape[1]), uv2.dtype),
            pltpu.SemaphoreType.DMA((2,)),
            pltpu.SemaphoreType.DMA((2,)),
        ],
    )
    def stage(uv_hbm, idx_hbm, out_hbm, idx_v, rows_v, gsem, wsem):
        wid = lax.axis_index("s") * SC_CORES + lax.axis_index("c")
        base = wid * rows_per_w

        def gather(j, b):
            return pltpu.make_async_copy(uv_hbm.at[idx_v.at[j]], rows_v.at[b], gsem.at[b])

        def write(row0, b):
            return pltpu.make_async_copy(rows_v.at[b], out_hbm.at[pl.ds(row0, SC_ROWS)], wsem.at[b])

        @pl.loop(0, n_blk)
        def _(blk):
            pltpu.sync_copy(idx_hbm.at[wid, blk], idx_v)
            blk_row0 = base + blk * (SC_IDX_BLK * SC_ROWS)
            gather(0, 0).start()

            @pl.loop(0, SC_IDX_BLK, step=2)
            def _(j):
                for b in range(2):
                    jj = j + b
                    row0 = pl.multiple_of(blk_row0 + jj * SC_ROWS, SC_ROWS)
                    gather(jj, b).wait()

                    @pl.when(jj > 0)
                    def _():
                        write(row0 - SC_ROWS, 1 - b).wait()

                    @pl.when(jj + 1 < SC_IDX_BLK)
                    def _():
                        gather(jj + 1, 1 - b).start()

                    write(row0, b).start()

            write(blk_row0 + (SC_IDX_BLK - 1) * SC_ROWS, 1).wait()

    return stage(uv2, idx4)


def _peer_kernel(step0, idx_hbm, xn_ref, g_ref, uv_hbm, after_ref, out_ref, idx_s, buf, sem_i, sem):
    del after_ref
    step = pl.program_id(0) + step0
    nbatch = xn_ref.shape[0] // PEER_NB
    nch = D_MODEL // LANES
    nkt = PEER_SEL // SUBLANES

    cp = pltpu.make_async_copy(idx_hbm.at[step], idx_s, sem_i.at[0])
    cp.start()
    cp.wait()

    def issue(b):
        grp = b & (PEER_NG - 1)

        def tok_body(tb, carry):
            base = (b * PEER_NB + tb) * PEER_SEL
            for k in range(PEER_SEL):
                e = idx_s[base + k]
                pltpu.make_async_copy(
                    uv_hbm.at[e], buf.at[grp, tb, k // SUBLANES, :, k % SUBLANES, :], sem.at[grp]
                ).start(priority=k % 2)
            return carry

        lax.fori_loop(0, PEER_NB, tok_body, 0)

    def wait(b):
        grp = b & (PEER_NG - 1)
        pltpu.make_async_copy(buf.at[grp], buf.at[grp], sem.at[grp]).wait()

    def compute(b):
        grp = b & (PEER_NG - 1)
        r0 = pl.multiple_of(b * PEER_NB, PEER_NB)
        xg = xn_ref[pl.ds(r0, PEER_NB), :]
        gg = g_ref[pl.ds(r0, PEER_NB), :]
        rows = [_peer_token(xg[tb:tb + 1, :], gg[tb:tb + 1, :],
                            functools.partial(lambda tb, kt, c: buf[grp, tb, kt, c], tb))
                for tb in range(PEER_NB)]
        out_ref[pl.ds(r0, PEER_NB), :] = jnp.concatenate(rows, axis=0)

    for b in range(PEER_LA):
        issue(b)

    def body(b, carry):
        wait(b)

        @pl.when(b + PEER_LA < nbatch)
        def _():
            issue(b + PEER_LA)

        compute(b)
        return carry

    lax.fori_loop(0, nbatch, body, 0)


def _order_after(after):
    return jnp.zeros((SUBLANES, LANES), F32) if after is None else after


def _peer(idx, xn, g, uv, ts, after=None):
    t, d = xn.shape
    tb = PEER_TB
    idx2 = idx.reshape(t // tb, tb * PEER_SEL)
    s0 = ts // tb
    return pl.pallas_call(
        functools.partial(_peer_kernel, s0),
        out_shape=jax.ShapeDtypeStruct((t, d), F32),
        grid=((t - ts) // tb,),
        in_specs=[pl.BlockSpec(memory_space=pl.ANY),
                  pl.BlockSpec((tb, d), lambda i: (i + s0, 0)),
                  pl.BlockSpec((tb, PEER_SEL), lambda i: (i + s0, 0)),
                  pl.BlockSpec(memory_space=pl.ANY),
                  pl.BlockSpec(memory_space=pl.ANY)],
        out_specs=pl.BlockSpec((tb, d), lambda i: (i + s0, 0)),
        scratch_shapes=[pltpu.SMEM((tb * PEER_SEL,), jnp.int32),
                        pltpu.VMEM((PEER_NG, PEER_NB, PEER_SEL // SUBLANES, d // LANES, SUBLANES, LANES),
                                   jnp.int32),
                        pltpu.SemaphoreType.DMA((1,)),
                        pltpu.SemaphoreType.DMA((PEER_NG,))],
        compiler_params=_cparams(("arbitrary",)),
        name="peer",
    )(idx2, xn, g, uv, _order_after(after))


def _ple_kernel(x1_ref, pe_ref, p_ref, nw_ref, wg_ref, wp_ref, nf_ref, y_ref):
    x2 = x1_ref[...] + pe_ref[...]
    gate = _sigmoid(_dot(_rms(x2, nw_ref[...]).astype(BF16), wg_ref[...]))
    x3 = x2 + _dot(p_ref[...].astype(BF16), wp_ref[...]) * gate
    y_ref[...] = _rms(x3, nf_ref[...])


def _ple(x1, pe, p, nw, wg, wp, nf, tm):
    t, d = x1.shape
    full = lambda shape: pl.BlockSpec(shape, lambda i: (0,) * len(shape))
    tok = lambda w: pl.BlockSpec((tm, w), lambda i: (i, 0))
    return pl.pallas_call(
        _ple_kernel,
        out_shape=jax.ShapeDtypeStruct((t, d), F32),
        grid=(t // tm,),
        in_specs=[tok(d), tok(d), tok(PLE_DIM), full(nw.shape), full(wg.shape), full(wp.shape),
                  full(nf.shape)],
        out_specs=tok(d),
        compiler_params=_cparams(("parallel",)),
        name="ple",
    )(x1, pe, p, nw, wg, wp, nf)


def _rope_tables(s):
    half = ROT_DIM // 2
    inv = ROPE_THETA ** (-jnp.arange(0, ROT_DIM, 2, dtype=F32) / ROT_DIM)
    ang = jnp.arange(s, dtype=F32)[:, None] * inv[None, :]
    cos, sin = jnp.cos(ang), jnp.sin(ang)
    ones = jnp.ones((s, ATT_QK_DIM - ROT_DIM), F32)
    zeros = jnp.zeros((s, ATT_QK_DIM - ROT_DIM), F32)
    zh = jnp.zeros((s, half), F32)
    c64 = jnp.concatenate([cos, cos, ones], axis=1)
    s1_64 = jnp.concatenate([zh, sin, zeros], axis=1)
    s2_64 = jnp.concatenate([-sin, zh, zeros], axis=1)
    rep = LANES // ATT_QK_DIM
    return jnp.tile(c64, (1, rep)), jnp.tile(s1_64, (1, rep)), jnp.tile(s2_64, (1, rep))


def _pack_uv(u, v):
    ub = lax.bitcast_convert_type(u.astype(BF16), jnp.uint16).astype(jnp.uint32)
    vb = lax.bitcast_convert_type(v.astype(BF16), jnp.uint16).astype(jnp.uint32)
    return lax.bitcast_convert_type((ub << 16) | vb, jnp.int32)


def _mix_route(x, w):
    b, s, d = x.shape
    t = b * s
    nc = s // M_CHUNK
    rc, rs1, rs2 = _rope_tables(s)
    aq, ak, av, mq, mk, mv, mo, gc, gr = _inproj(
        x, w["norm_mix"], w["wqk"], w["wav"], w["wmqk"], w["wmvo"], w["wg"], w["wgt"], w["gb"],
        w["gbt"], w["conv"], rc, rs1, rs2, tm=512)
    att = _attn(w["lq"], aq, ak, av, w["attn_gain"], tq=256)
    gc5 = gc.reshape(b, nc, M_CHUNK, M_HEADS, 4).transpose(0, 3, 1, 2, 4)
    gr5 = gr.reshape(b, M_HEADS, 4, nc, M_CHUNK).transpose(0, 1, 3, 2, 4)
    mem = _mlstm(mq, mk, mv, mo, gc5, gr5, w["mlstm_gain"])
    x1, xn, qp = _outproj(x.reshape(t, d), att.reshape(t, ATT_W), mem.reshape(t, M_W),
                          w["wo_a"], w["wo_m"], w["norm_ffn"], w["wq"], tm=512)
    idx, g = _route(qp, w["sk"], tm=128)
    return x1, xn, idx, g


def _peer_all(groups, uv2):
    uv3 = uv2.reshape(-1, D_MODEL // LANES, LANES)
    staged, outs = [], []
    prev = None
    for xn, idx, g in groups:
        t, d = xn.shape
        ts = (t * PEER_SC_SHARE_NUM // PEER_SC_SHARE_DEN) // PEER_SC_QUANTUM * PEER_SC_QUANTUM
        assert 0 < ts < t and (t - ts) % PEER_TB == 0, (t, ts)
        staged.append(_stage_rows(uv2, idx[:ts]).reshape(ts, PEER_SEL, d))
        prev = _peer(idx, xn, g, uv3, ts, after=prev)
        outs.append(prev)
    for i, (xn, idx, g) in enumerate(groups):
        prev = _peer_staged(staged[i], xn, g, outs[i], after=prev)
        outs[i] = prev
    return outs


def kernel(x_prompt, x_sample, p_prompt, p_sample, norm_mix, w_in, gate_bias, conv_qk, lambda_qk,
           attn_gain, mlstm_gain, w_out, norm_ffn, peer_wq, peer_subkeys, peer_u, peer_v, norm_ple,
           ple_w_gate, ple_w_proj, norm_final):
    li = 0
    win = w_in[li]
    c0 = 0
    cols = {}
    for name, width in (("aq", ATT_W), ("ak", ATT_W), ("av", ATT_W), ("mq", M_W), ("mk", M_W),
                        ("mv", M_W), ("mo", M_W), ("gt", N_GATES)):
        cols[name] = win[:, c0:c0 + width]
        c0 += width
    perm = jnp.array([ty * M_HEADS + h for h in range(M_HEADS) for ty in range(4)], jnp.int32)
    wg = cols["gt"][:, perm].astype(BF16)
    gb = gate_bias[li][perm]
    w = {
        "norm_mix": norm_mix[li][None, :],
        "wqk": jnp.concatenate([cols["aq"], cols["ak"]], axis=1).astype(BF16),
        "wav": cols["av"].astype(BF16),
        "wmqk": jnp.concatenate([cols["mq"], cols["mk"]], axis=1).astype(BF16),
        "wmvo": jnp.concatenate([cols["mv"], cols["mo"]], axis=1).astype(BF16),
        "wg": wg,
        "wgt": wg.T,
        "gb": gb[None, :],
        "gbt": gb[:, None],
        "conv": conv_qk[li],
        "lq": lambda_qk[li],
        "attn_gain": attn_gain[li][None, :],
        "mlstm_gain": mlstm_gain[li][None, :],
        "wo_a": w_out[li][:ATT_W].astype(BF16),
        "wo_m": w_out[li][ATT_W:].astype(BF16),
        "norm_ffn": norm_ffn[li][None, :],
        "wq": peer_wq[li].astype(BF16),
        "sk": peer_subkeys[li].reshape(PEER_HEADS * 2, N_KEYS, PEER_HALF).astype(BF16),
        "uv": _pack_uv(peer_u[li], peer_v[li]),
        "norm_ple": norm_ple[li][None, :],
        "ple_wg": ple_w_gate[li].astype(BF16),
        "ple_wp": ple_w_proj[li].astype(BF16),
        "norm_final": norm_final[None, :],
    }
    hb = x_prompt.shape[0] // 2
    xs = (x_prompt[:hb], x_prompt[hb:], x_sample)
    ps = (p_prompt[li][:hb], p_prompt[li][hb:], p_sample[li])
    mixed = [_mix_route(x, w) for x in xs]
    pes = _peer_all([(xn, idx, g) for _, xn, idx, g in mixed], w["uv"])
    ys = []
    for x, p, (x1, _, _, _), pe in zip(xs, ps, mixed, pes):
        y = _ple(x1, pe, p.reshape(-1, PLE_DIM), w["norm_ple"], w["ple_wg"], w["ple_wp"],
                 w["norm_final"], tm=512)
        ys.append(y.reshape(x.shape))
    return (jnp.concatenate(ys[:2], axis=0), ys[2])
```

```python
import functools
import math

import jax
import jax.numpy as jnp
from jax import lax
from jax.experimental import pallas as pl
from jax.experimental.pallas import tpu as pltpu
from jax.experimental.pallas import tpu_sc as plsc

F32 = jnp.float32
BF16 = jnp.bfloat16

D_MODEL = 1024
ATT_W = 512
ATT_HEADS = 4
ATT_V_DIM = 128
ATT_QK_DIM = 64
ROT_DIM = 16
ROPE_THETA = 500000.0
M_W = 512
M_HEADS = 4
M_HEAD_DIM = 128
M_CHUNK = 128
CONV_K = 5
N_GATES = 16
PEER_HEADS = 8
PEER_QDIM = 256
PEER_HALF = 128
N_KEYS = 128
PEER_TOPK = 16
PEER_SEL = PEER_HEADS * PEER_TOPK
PLE_DIM = 256
EPS = 1e-6
LAM_INIT = 0.8 - 0.6 * math.exp(-0.3 * 0)
Q_SCALE = ATT_QK_DIM ** -0.5 * math.log2(math.e)

LANES = 128
SUBLANES = 8
VMEM_LIMIT = 56 * 1024 * 1024

NEG_INF = float("-inf")


def _cparams(sem):
    return pltpu.CompilerParams(dimension_semantics=sem, vmem_limit_bytes=VMEM_LIMIT)


def _dot(a, b):
    return jnp.dot(a, b, preferred_element_type=F32)


def _dot_nt(a, b):
    return lax.dot_general(a, b, (((1,), (1,)), ((), ())), preferred_element_type=F32)


def _rms(x, w):
    ms = jnp.mean(x * x, axis=-1, keepdims=True)
    return (x * lax.rsqrt(ms + EPS)) * w


def _sigmoid(x):
    return 1.0 / (1.0 + jnp.exp(-x))


def _log_sigmoid(x):
    return jnp.minimum(x, 0.0) - jnp.log(1.0 + jnp.exp(-jnp.abs(x)))


def _inproj_kernel(x_ref, xp_ref, xn_ref, nw_ref, wqk_ref, wav_ref, wmqk_ref, wmvo_ref,
                   wg_ref, wgt_ref, gb_ref, gbt_ref, cw_ref, rc_ref, rs1_ref, rs2_ref,
                   aq_ref, ak_ref, av_ref, mq_ref, mk_ref, mv_ref, mo_ref, gc_ref, gr_ref):
    i = pl.program_id(1)
    last = pl.num_programs(1) - 1
    nw = nw_ref[...]
    tm = x_ref.shape[1]

    h = _rms(x_ref[0], nw).astype(BF16)
    hp = _rms(xp_ref[0, 0], nw).astype(BF16)
    hn = _rms(xn_ref[0, 0], nw).astype(BF16)

    qk = _dot(h, wqk_ref[...])
    rc = rc_ref[...]
    rs1 = rs1_ref[...]
    rs2 = rs2_ref[...]
    half = ROT_DIM // 2
    for c in range(2 * ATT_W // LANES):
        xc = qk[:, c * LANES:(c + 1) * LANES]
        r = xc * rc + pltpu.roll(xc, half, 1) * rs1 + pltpu.roll(xc, LANES - half, 1) * rs2
        if c < ATT_W // LANES:
            aq_ref[0, :, c * LANES:(c + 1) * LANES] = (r * Q_SCALE).astype(BF16)
        else:
            cc = c - ATT_W // LANES
            ak_ref[0, :, cc * LANES:(cc + 1) * LANES] = r.astype(BF16)

    av_ref[0] = _dot(h, wav_ref[...]).astype(BF16)

    wm = wmqk_ref[...]
    pm = _dot(h, wm)
    pp = _dot(hp, wm) * (i > 0).astype(F32)
    pn = _dot(hn, wm) * (i < last).astype(F32)
    ext = jnp.concatenate([pp, pm, pn], axis=0)
    cw = cw_ref[...]
    off = SUBLANES - CONV_K // 2
    conv = cw[0:1, :] * ext[off:off + tm, :]
    for j in range(1, CONV_K):
        conv = conv + cw[j:j + 1, :] * ext[off + j:off + j + tm, :]
    act = conv * _sigmoid(conv)
    mq_ref[0] = act[:, :M_W].astype(BF16)
    mk_ref[0] = (act[:, M_W:] * (M_HEAD_DIM ** -0.5)).astype(BF16)

    vo = _dot(h, wmvo_ref[...])
    mv_ref[0] = vo[:, :M_W].astype(BF16)
    mo_ref[0] = vo[:, M_W:].astype(BF16)

    gc_ref[0] = _dot(h, wg_ref[...]) + gb_ref[...]
    gr_ref[0] = _dot_nt(wgt_ref[...], h) + gbt_ref[...]


def _inproj(x, norm_w, wqk, wav, wmqk, wmvo, wg, wgt, gb, gbt, conv_w, rc, rs1, rs2, tm):
    b, s, d = x.shape
    nt = s // tm
    x8 = x.reshape(b, s // SUBLANES, SUBLANES, d)
    r8 = tm // SUBLANES
    nb8 = s // SUBLANES
    full = lambda shape: pl.BlockSpec(shape, lambda bi, i: (0,) * len(shape))
    tok = lambda w: pl.BlockSpec((1, tm, w), lambda bi, i: (bi, i, 0))
    out_shapes = (
        [jax.ShapeDtypeStruct((b, s, ATT_W), BF16)] * 3
        + [jax.ShapeDtypeStruct((b, s, M_W), BF16)] * 4
        + [jax.ShapeDtypeStruct((b, s, N_GATES), F32), jax.ShapeDtypeStruct((b, N_GATES, s), F32)]
    )
    return pl.pallas_call(
        _inproj_kernel,
        out_shape=out_shapes,
        grid=(b, nt),
        in_specs=[
            tok(d),
            pl.BlockSpec((1, 1, SUBLANES, d), lambda bi, i: (bi, jnp.maximum(i * r8 - 1, 0), 0, 0)),
            pl.BlockSpec((1, 1, SUBLANES, d), lambda bi, i: (bi, jnp.minimum((i + 1) * r8, nb8 - 1), 0, 0)),
            full((1, d)),
            full(wqk.shape), full(wav.shape), full(wmqk.shape), full(wmvo.shape),
            full(wg.shape), full(wgt.shape), full(gb.shape), full(gbt.shape), full(conv_w.shape),
            pl.BlockSpec((tm, LANES), lambda bi, i: (i, 0)),
            pl.BlockSpec((tm, LANES), lambda bi, i: (i, 0)),
            pl.BlockSpec((tm, LANES), lambda bi, i: (i, 0)),
        ],
        out_specs=[tok(ATT_W)] * 3 + [tok(M_W)] * 4
        + [tok(N_GATES), pl.BlockSpec((1, N_GATES, tm), lambda bi, i: (bi, 0, i))],
        compiler_params=_cparams(("parallel", "parallel")),
        name="inproj",
    )(x, x8, x8, norm_w, wqk, wav, wmqk, wmvo, wg, wgt, gb, gbt, conv_w, rc, rs1, rs2)


ATT_KEY_BLOCK = 1024


def _attn_kernel(lq_ref, q_ref, k_ref, v_ref, gain_ref, o_ref):
    lq = lq_ref[...]
    lam = (jnp.exp(jnp.sum(lq[0:1] * lq[1:2], axis=1, keepdims=True))
           - jnp.exp(jnp.sum(lq[2:3] * lq[3:4], axis=1, keepdims=True)) + LAM_INIT)
    q = q_ref[0]
    tq = q.shape[0]
    lane = lax.broadcasted_iota(jnp.int32, q.shape, 1)
    zero = jnp.zeros_like(q)
    qq = jnp.concatenate([jnp.where(lane < ATT_QK_DIM, q, zero),
                          jnp.where(lane >= ATT_QK_DIM, q, zero)], axis=0)
    m = jnp.full((2 * tq, 1), NEG_INF, F32)
    ln = jnp.zeros((2 * tq, 1), F32)
    acc = jnp.zeros((2 * tq, ATT_V_DIM), F32)
    kb = min(ATT_KEY_BLOCK, k_ref.shape[1])
    assert k_ref.shape[1] % kb == 0
    for c in range(k_ref.shape[1] // kb):
        ks = slice(c * kb, (c + 1) * kb)
        sc = _dot_nt(qq, k_ref[0, ks, :])
        m_new = jnp.maximum(m, jnp.max(sc, axis=1, keepdims=True))
        alpha = jnp.exp2(m - m_new)
        e = jnp.exp2(sc - m_new)
        ln = alpha * ln + jnp.sum(e, axis=1, keepdims=True)
        acc = alpha * acc + _dot(e.astype(BF16), v_ref[0, ks, :])
        m = m_new
    o = acc[:tq] * (1.0 / ln[:tq]) - acc[tq:] * (lam / ln[tq:])
    o_ref[0] = (_rms(o, gain_ref[...]) * (1.0 - LAM_INIT)).astype(BF16)


def _attn(lq, q, k, v, gain, tq):
    b, s, _ = q.shape
    return pl.pallas_call(
        _attn_kernel,
        out_shape=jax.ShapeDtypeStruct((b, s, ATT_W), BF16),
        grid=(b, ATT_HEADS, s // tq),
        in_specs=[
            pl.BlockSpec(lq.shape, lambda bi, h, i: (0, 0)),
            pl.BlockSpec((1, tq, ATT_V_DIM), lambda bi, h, i: (bi, i, h)),
            pl.BlockSpec((1, s, ATT_V_DIM), lambda bi, h, i: (bi, 0, h)),
            pl.BlockSpec((1, s, ATT_V_DIM), lambda bi, h, i: (bi, 0, h)),
            pl.BlockSpec((1, ATT_V_DIM), lambda bi, h, i: (0, 0)),
        ],
        out_specs=pl.BlockSpec((1, tq, ATT_V_DIM), lambda bi, h, i: (bi, i, h)),
        compiler_params=_cparams(("parallel", "parallel", "parallel")),
        name="attn",
    )(lq, q, k, v, gain)


def _mlstm_chunk(q, k, v, ic_col, lf_col, ic_row, lf_row, cst, n, m, fwd):
    ln = q.shape[0]
    row = lax.broadcasted_iota(jnp.int32, (ln, ln), 0)
    col = lax.broadcasted_iota(jnp.int32, (ln, ln), 1)
    lower = col <= row
    upper = col >= row
    mask, mask_t = (lower, upper) if fwd else (upper, lower)
    bcum_col = jnp.sum(jnp.where(mask, lf_row, 0.0), axis=1, keepdims=True)
    bcum_row = jnp.sum(jnp.where(mask_t, lf_col, 0.0), axis=0, keepdims=True)
    dlog = jnp.where(mask, bcum_col - bcum_row + ic_row, NEG_INF)
    inter = bcum_col + m
    m_t = jnp.maximum(inter, jnp.max(dlog, axis=1, keepdims=True))
    w = jnp.exp(dlog - m_t)
    sc = jnp.exp(inter - m_t)
    wqk = w * _dot_nt(q, k)
    num = sc * _dot(q, cst.astype(BF16)) + _dot(wqk.astype(BF16), v)
    qn = jnp.sum(q.astype(F32) * n, axis=1, keepdims=True)
    den = sc * qn + jnp.sum(wqk, axis=1, keepdims=True)
    h = num / jnp.maximum(jnp.abs(den), jnp.exp(-m_t))
    b_last = jnp.sum(lf_row, axis=1, keepdims=True)
    rel_row = b_last - bcum_row + ic_row
    rel_col = b_last - bcum_col + ic_col
    m_new = jnp.maximum(b_last + m, jnp.max(rel_row, axis=1, keepdims=True))
    dec_col = jnp.exp(rel_col - m_new)
    keep = jnp.exp(b_last + m - m_new)
    kd = k.astype(F32) * dec_col
    c_new = keep * cst + lax.dot_general(kd.astype(BF16), v, (((0,), (0,)), ((), ())),
                                         preferred_element_type=F32)
    n_new = keep * n + jnp.sum(kd, axis=0, keepdims=True)
    return h, c_new, n_new, m_new


M_HEADS_PER_STEP = 1


def _mlstm_kernel(q_ref, k_ref, v_ref, o_ref, gc_ref, gr_ref, gain_ref, out_ref, hf_ref, hb_ref):
    s = q_ref.shape[1]
    nc = s // M_CHUNK
    d = M_HEAD_DIM
    heads = range(M_HEADS_PER_STEP)

    def step(j, c, carry, fwd):
        cst, n, m = carry
        sl = pl.ds(pl.multiple_of(c * M_CHUNK, M_CHUNK), M_CHUNK)
        ln = slice(j * d, (j + 1) * d)
        gc = gc_ref[0, j, c]
        gr = gr_ref[0, j, c]
        t = 0 if fwd else 2
        h, cst, n, m = _mlstm_chunk(
            q_ref[0, sl, ln], k_ref[0, sl, ln], v_ref[0, sl, ln],
            gc[:, t:t + 1], _log_sigmoid(gc[:, t + 1:t + 2]),
            gr[t:t + 1, :], _log_sigmoid(gr[t + 1:t + 2, :]),
            cst, n, m, fwd)
        return sl, ln, h, (cst, n, m)

    init = (jnp.zeros((d, d), F32), jnp.zeros((1, d), F32), jnp.zeros((1, 1), F32))

    def scan_body(c, carry):
        new = []
        for j in heads:
            cf, cb = carry[j]
            sl, ln, h, cf = step(j, c, cf, True)
            hf_ref[sl, ln] = h
            sl, ln, h, cb = step(j, nc - 1 - c, cb, False)
            hb_ref[sl, ln] = h
            new.append((cf, cb))
        return tuple(new)

    lax.fori_loop(0, nc, scan_body, tuple((init, init) for _ in heads))

    def gate_body(c, carry):
        sl = pl.ds(pl.multiple_of(c * M_CHUNK, M_CHUNK), M_CHUNK)
        for j in heads:
            ln = slice(j * d, (j + 1) * d)
            y = _sigmoid(o_ref[0, sl, ln].astype(F32)) * (hf_ref[sl, ln] + hb_ref[sl, ln])
            out_ref[0, sl, ln] = _rms(y, gain_ref[:, ln]).astype(BF16)
        return carry

    lax.fori_loop(0, nc, gate_body, 0)


def _mlstm(q, k, v, o, gc, gr, gain):
    b, s, _ = q.shape
    nc = s // M_CHUNK
    hp = M_HEADS_PER_STEP
    w = hp * M_HEAD_DIM
    head = pl.BlockSpec((1, s, w), lambda bi, h: (bi, 0, h))
    return pl.pallas_call(
        _mlstm_kernel,
        out_shape=jax.ShapeDtypeStruct((b, s, M_W), BF16),
        grid=(b, M_HEADS // hp),
        in_specs=[
            head, head, head, head,
            pl.BlockSpec((1, hp, nc, M_CHUNK, 4), lambda bi, h: (bi, h, 0, 0, 0)),
            pl.BlockSpec((1, hp, nc, 4, M_CHUNK), lambda bi, h: (bi, h, 0, 0, 0)),
            pl.BlockSpec((1, w), lambda bi, h: (0, h)),
        ],
        out_specs=head,
        scratch_shapes=[pltpu.VMEM((s, w), F32), pltpu.VMEM((s, w), F32)],
        compiler_params=_cparams(("parallel", "parallel")),
        name="mlstm",
    )(q, k, v, o, gc, gr, gain)


def _outproj_kernel(x_ref, att_ref, mem_ref, wa_ref, wm_ref, nw_ref, wq_ref, x1_ref, xn_ref, qp_ref):
    x1 = x_ref[...] + _dot(att_ref[...], wa_ref[...]) + _dot(mem_ref[...], wm_ref[...])
    x1_ref[...] = x1
    xn = _rms(x1, nw_ref[...])
    xn_ref[...] = xn
    qp = _dot(xn.astype(BF16), wq_ref[...])
    for c in range(qp_ref.shape[0]):
        qp_ref[c] = qp[:, c * LANES:(c + 1) * LANES].astype(BF16)


def _outproj(x, att, mem, wa, wm, nw, wq, tm):
    t, d = x.shape
    nq = wq.shape[1] // LANES
    full = lambda shape: pl.BlockSpec(shape, lambda i: (0,) * len(shape))
    tok = lambda w: pl.BlockSpec((tm, w), lambda i: (i, 0))
    return pl.pallas_call(
        _outproj_kernel,
        out_shape=[jax.ShapeDtypeStruct((t, d), F32), jax.ShapeDtypeStruct((t, d), F32),
                   jax.ShapeDtypeStruct((nq, t, LANES), BF16)],
        grid=(t // tm,),
        in_specs=[tok(d), tok(ATT_W), tok(M_W), full(wa.shape), full(wm.shape), full(nw.shape),
                  full(wq.shape)],
        out_specs=[tok(d), tok(d), pl.BlockSpec((nq, tm, LANES), lambda i: (0, i, 0))],
        compiler_params=_cparams(("parallel",)),
        name="outproj",
    )(x, att, mem, wa, wm, nw, wq)


ROUTE_HEADS_PER_ITER = 4


def _topk_rows(sc, n):
    pos = lax.broadcasted_iota(jnp.int32, sc.shape, 0).astype(F32)
    vals, poss = [], []
    for _ in range(PEER_TOPK):
        m = jnp.max(sc, axis=0, keepdims=True)
        p = jnp.min(jnp.where(sc == m, pos, float(n)), axis=0, keepdims=True)
        vals.append(m)
        poss.append(p)
        sc = jnp.where(pos == p, NEG_INF, sc)
    return jnp.concatenate(vals, axis=0), jnp.concatenate(poss, axis=0).astype(jnp.int32)


def _route_kernel(qp_ref, sk_ref, idx_ref, g_ref, idx_t, g_t):
    def head(h, carry):
        s1 = _dot_nt(sk_ref[2 * h], qp_ref[2 * h])
        s2 = _dot_nt(sk_ref[2 * h + 1], qp_ref[2 * h + 1])
        v1, i1 = _topk_rows(s1, N_KEYS)
        v2, i2 = _topk_rows(s2, N_KEYS)
        k = PEER_TOPK
        sub = SUBLANES
        brow = lax.broadcasted_iota(jnp.int32, (sub, v2.shape[1]), 0)
        slabs = [v1[0:1, :] + v2]
        for a in range(1, sub):
            slabs.append(jnp.where(brow < k // (a + 1), v1[a:a + 1, :] + v2[0:sub, :], NEG_INF))
        slabs.append(v1[sub:k, :] + v2[0:1, :])
        cand = jnp.concatenate(slabs, axis=0)
        sc, cp = _topk_rows(cand, cand.shape[0])
        mid = cp < k + sub * (sub - 1)
        ca = jnp.where(cp < k, 0, jnp.where(mid, lax.shift_right_logical(cp, 3) - 1, cp - (k + sub * (sub - 1)) + sub))
        cb = jnp.where(cp < k, cp, jnp.where(mid, cp & (sub - 1), 0))
        e1 = jnp.zeros_like(cp)
        e2 = jnp.zeros_like(cp)
        for a in range(PEER_TOPK):
            e1 = jnp.where(ca == a, i1[a:a + 1, :], e1)
            e2 = jnp.where(cb == a, i2[a:a + 1, :], e2)
        ex = jnp.exp(sc - sc[0:1, :])
        rows = pl.ds(pl.multiple_of(h * PEER_TOPK, PEER_TOPK), PEER_TOPK)
        idx_t[rows, :] = e1 * N_KEYS + e2
        g_t[rows, :] = ex / jnp.sum(ex, axis=0, keepdims=True)
        return carry

    def heads(hh, carry):
        for j in range(ROUTE_HEADS_PER_ITER):
            head(hh * ROUTE_HEADS_PER_ITER + j, carry)
        return carry

    lax.fori_loop(0, PEER_HEADS // ROUTE_HEADS_PER_ITER, heads, 0)
    idx_ref[...] = idx_t[...].T
    g_ref[...] = g_t[...].T


def _route(qp, sk, tm):
    nq, t, _ = qp.shape
    return pl.pallas_call(
        _route_kernel,
        out_shape=[jax.ShapeDtypeStruct((t, PEER_SEL), jnp.int32),
                   jax.ShapeDtypeStruct((t, PEER_SEL), F32)],
        grid=(t // tm,),
        in_specs=[pl.BlockSpec((nq, tm, LANES), lambda i: (0, i, 0)),
                  pl.BlockSpec(sk.shape, lambda i: (0, 0, 0))],
        out_specs=[pl.BlockSpec((tm, PEER_SEL), lambda i: (i, 0))] * 2,
        scratch_shapes=[pltpu.VMEM((PEER_SEL, tm), jnp.int32), pltpu.VMEM((PEER_SEL, tm), F32)],
        compiler_params=_cparams(("parallel",)),
        name="route",
    )(qp, sk)


PEER_TB = 256
PEER_NB = SUBLANES
PEER_STAGED_TB = 16
PEER_NG = 4
PEER_LA = 2
HI16 = -65536


def _gelu(a):
    return 0.5 * a * (1.0 + lax.erf(a * (2.0 ** -0.5)))


def _peer_token(xr, gr, tile):
    nch = D_MODEL // LANES
    nkt = PEER_SEL // SUBLANES
    xb = [jnp.broadcast_to(xr[:, c * LANES:(c + 1) * LANES], (SUBLANES, LANES)) for c in range(nch)]
    parts = []
    for kt in range(nkt):
        acc = None
        for c in range(nch):
            u = lax.bitcast_convert_type(tile(kt, c) & HI16, F32)
            acc = u * xb[c] if acc is None else acc + u * xb[c]
        parts.append(acc)
    part = jnp.concatenate(parts, axis=0)
    a_row = jnp.sum(part.T, axis=0, keepdims=True)
    act = _gelu(a_row) * gr
    act_b = jnp.broadcast_to(act, (LANES, PEER_SEL)).T
    outs = []
    for c in range(nch):
        acc = None
        for kt in range(nkt):
            v = lax.bitcast_convert_type(tile(kt, c) << 16, F32)
            t = act_b[kt * SUBLANES:(kt + 1) * SUBLANES, :] * v
            acc = t if acc is None else acc + t
        outs.append(jnp.sum(acc, axis=0, keepdims=True))
    return jnp.concatenate(outs, axis=1)


def _peer_staged_kernel(st_ref, xn_ref, g_ref, rest_ref, after_ref, out_ref):
    del rest_ref, after_ref
    xg = xn_ref[...]
    gg = g_ref[...]
    rows = []
    for tb in range(xn_ref.shape[0]):
        def tile(kt, c, tb=tb):
            return st_ref[tb, kt * SUBLANES:(kt + 1) * SUBLANES, c * LANES:(c + 1) * LANES]
        rows.append(_peer_token(xg[tb:tb + 1, :], gg[tb:tb + 1, :], tile))
    out_ref[...] = jnp.concatenate(rows, axis=0)


def _peer_staged(staged, xn, g, rest, after=None):
    ts = staged.shape[0]
    t, d = xn.shape
    tb = PEER_STAGED_TB
    return pl.pallas_call(
        _peer_staged_kernel,
        out_shape=jax.ShapeDtypeStruct((t, d), F32),
        grid=(ts // tb,),
        in_specs=[pl.BlockSpec((tb, PEER_SEL, d), lambda i: (i, 0, 0)),
                  pl.BlockSpec((tb, d), lambda i: (i, 0)),
                  pl.BlockSpec((tb, PEER_SEL), lambda i: (i, 0)),
                  pl.BlockSpec(memory_space=pl.ANY),
                  pl.BlockSpec(memory_space=pl.ANY)],
        out_specs=pl.BlockSpec((tb, d), lambda i: (i, 0)),
        input_output_aliases={3: 0},
        compiler_params=_cparams(("parallel",)),
        name="peer_staged",
    )(staged, xn, g, rest, _order_after(after))


SC_CORES = 2
SC_SUBCORES = 16
SC_WORKERS = SC_CORES * SC_SUBCORES
SC_ROWS = 32
SC_IDX_BLK = 128
PEER_SC_SHARE_NUM = 11
PEER_SC_SHARE_DEN = 16
PEER_SC_QUANTUM = SC_WORKERS * SC_ROWS * SC_IDX_BLK // PEER_SEL


def _stage_rows(uv2, idx):
    n_tok, nsel = idx.shape
    n_rows = n_tok * nsel
    rows_per_w = n_rows // SC_WORKERS
    n_blk = rows_per_w // (SC_ROWS * SC_IDX_BLK)
    idx4 = idx.reshape(SC_WORKERS, n_blk, SC_IDX_BLK, SC_ROWS)
    mesh = plsc.VectorSubcoreMesh(core_axis_name="c", subcore_axis_name="s")

    @functools.partial(
        pl.kernel, mesh=mesh,
        out_type=jax.ShapeDtypeStruct((n_rows, uv2.shape[1]), uv2.dtype),
        scratch_types=[
            pltpu.VMEM((SC_IDX_BLK, SC_ROWS), jnp.int32),
            pltpu.VMEM((2, SC_ROWS, uv2.shape[1]), uv2.dtype),
            pltpu.SemaphoreType.DMA((2,)),
            pltpu.SemaphoreType.DMA((2,)),
        ],
    )
    def stage(uv_hbm, idx_hbm, out_hbm, idx_v, rows_v, gsem, wsem):
        wid = lax.axis_index("s") * SC_CORES + lax.axis_index("c")
        base = wid * rows_per_w

        def gather(j, b):
            return pltpu.make_async_copy(uv_hbm.at[idx_v.at[j]], rows_v.at[b], gsem.at[b])

        def write(row0, b):
            return pltpu.make_async_copy(rows_v.at[b], out_hbm.at[pl.ds(row0, SC_ROWS)], wsem.at[b])

        @pl.loop(0, n_blk)
        def _(blk):
            pltpu.sync_copy(idx_hbm.at[wid, blk], idx_v)
            blk_row0 = base + blk * (SC_IDX_BLK * SC_ROWS)
            gather(0, 0).start()

            @pl.loop(0, SC_IDX_BLK, step=2)
            def _(j):
                for b in range(2):
                    jj = j + b
                    row0 = pl.multiple_of(blk_row0 + jj * SC_ROWS, SC_ROWS)
                    gather(jj, b).wait()

                    @pl.when(jj > 0)
                    def _():
                        write(row0 - SC_ROWS, 1 - b).wait()

                    @pl.when(jj + 1 < SC_IDX_BLK)
                    def _():
                        gather(jj + 1, 1 - b).start()

                    write(row0, b).start()

            write(blk_row0 + (SC_IDX_BLK - 1) * SC_ROWS, 1).wait()

    return stage(uv2, idx4)


def _peer_kernel(step0, idx_hbm, xn_ref, g_ref, uv_hbm, after_ref, out_ref, idx_s, buf, sem_i, sem):
    del after_ref
    step = pl.program_id(0) + step0
    nbatch = xn_ref.shape[0] // PEER_NB
    nch = D_MODEL // LANES
    nkt = PEER_SEL // SUBLANES

    cp = pltpu.make_async_copy(idx_hbm.at[step], idx_s, sem_i.at[0])
    cp.start()
    cp.wait()

    def issue(b):
        grp = b & (PEER_NG - 1)

        def tok_body(tb, carry):
            base = (b * PEER_NB + tb) * PEER_SEL
            for k in range(PEER_SEL):
                e = idx_s[base + k]
                pltpu.make_async_copy(
                    uv_hbm.at[e], buf.at[grp, tb, k // SUBLANES, :, k % SUBLANES, :], sem.at[grp]
                ).start(priority=k % 2)
            return carry

        lax.fori_loop(0, PEER_NB, tok_body, 0)

    def wait(b):
        grp = b & (PEER_NG - 1)
        pltpu.make_async_copy(buf.at[grp], buf.at[grp], sem.at[grp]).wait()

    def compute(b):
        grp = b & (PEER_NG - 1)
        r0 = pl.multiple_of(b * PEER_NB, PEER_NB)
        xg = xn_ref[pl.ds(r0, PEER_NB), :]
        gg = g_ref[pl.ds(r0, PEER_NB), :]
        rows = [_peer_token(xg[tb:tb + 1, :], gg[tb:tb + 1, :],
                            functools.partial(lambda tb, kt, c: buf[grp, tb, kt, c], tb))
                for tb in range(PEER_NB)]
        out_ref[pl.ds(r0, PEER_NB), :] = jnp.concatenate(rows, axis=0)

    for b in range(PEER_LA):
        issue(b)

    def body(b, carry):
        wait(b)

        @pl.when(b + PEER_LA < nbatch)
        def _():
            issue(b + PEER_LA)

        compute(b)
        return carry

    lax.fori_loop(0, nbatch, body, 0)


def _order_after(after):
    return jnp.zeros((SUBLANES, LANES), F32) if after is None else after


def _peer(idx, xn, g, uv, ts, after=None):
    t, d = xn.shape
    tb = PEER_TB
    idx2 = idx.reshape(t // tb, tb * PEER_SEL)
    s0 = ts // tb
    return pl.pallas_call(
        functools.partial(_peer_kernel, s0),
        out_shape=jax.ShapeDtypeStruct((t, d), F32),
        grid=((t - ts) // tb,),
        in_specs=[pl.BlockSpec(memory_space=pl.ANY),
                  pl.BlockSpec((tb, d), lambda i: (i + s0, 0)),
                  pl.BlockSpec((tb, PEER_SEL), lambda i: (i + s0, 0)),
                  pl.BlockSpec(memory_space=pl.ANY),
                  pl.BlockSpec(memory_space=pl.ANY)],
        out_specs=pl.BlockSpec((tb, d), lambda i: (i + s0, 0)),
        scratch_shapes=[pltpu.SMEM((tb * PEER_SEL,), jnp.int32),
                        pltpu.VMEM((PEER_NG, PEER_NB, PEER_SEL // SUBLANES, d // LANES, SUBLANES, LANES),
                                   jnp.int32),
                        pltpu.SemaphoreType.DMA((1,)),
                        pltpu.SemaphoreType.DMA((PEER_NG,))],
        compiler_params=_cparams(("arbitrary",)),
        name="peer",
    )(idx2, xn, g, uv, _order_after(after))


def _ple_kernel(x1_ref, pe_ref, p_ref, nw_ref, wg_ref, wp_ref, nf_ref, y_ref):
    x2 = x1_ref[...] + pe_ref[...]
    gate = _sigmoid(_dot(_rms(x2, nw_ref[...]).astype(BF16), wg_ref[...]))
    x3 = x2 + _dot(p_ref[...].astype(BF16), wp_ref[...]) * gate
    y_ref[...] = _rms(x3, nf_ref[...])


def _ple(x1, pe, p, nw, wg, wp, nf, tm):
    t, d = x1.shape
    full = lambda shape: pl.BlockSpec(shape, lambda i: (0,) * len(shape))
    tok = lambda w: pl.BlockSpec((tm, w), lambda i: (i, 0))
    return pl.pallas_call(
        _ple_kernel,
        out_shape=jax.ShapeDtypeStruct((t, d), F32),
        grid=(t // tm,),
        in_specs=[tok(d), tok(d), tok(PLE_DIM), full(nw.shape), full(wg.shape), full(wp.shape),
                  full(nf.shape)],
        out_specs=tok(d),
        compiler_params=_cparams(("parallel",)),
        name="ple",
    )(x1, pe, p, nw, wg, wp, nf)


def _rope_tables(s):
    half = ROT_DIM // 2
    inv = ROPE_THETA ** (-jnp.arange(0, ROT_DIM, 2, dtype=F32) / ROT_DIM)
    ang = jnp.arange(s, dtype=F32)[:, None] * inv[None, :]
    cos, sin = jnp.cos(ang), jnp.sin(ang)
    ones = jnp.ones((s, ATT_QK_DIM - ROT_DIM), F32)
    zeros = jnp.zeros((s, ATT_QK_DIM - ROT_DIM), F32)
    zh = jnp.zeros((s, half), F32)
    c64 = jnp.concatenate([cos, cos, ones], axis=1)
    s1_64 = jnp.concatenate([zh, sin, zeros], axis=1)
    s2_64 = jnp.concatenate([-sin, zh, zeros], axis=1)
    rep = LANES // ATT_QK_DIM
    return jnp.tile(c64, (1, rep)), jnp.tile(s1_64, (1, rep)), jnp.tile(s2_64, (1, rep))


def _pack_uv(u, v):
    ub = lax.bitcast_convert_type(u.astype(BF16), jnp.uint16).astype(jnp.uint32)
    vb = lax.bitcast_convert_type(v.astype(BF16), jnp.uint16).astype(jnp.uint32)
    return lax.bitcast_convert_type((ub << 16) | vb, jnp.int32)


def _mix_route(x, w):
    b, s, d = x.shape
    t = b * s
    nc = s // M_CHUNK
    rc, rs1, rs2 = _rope_tables(s)
    aq, ak, av, mq, mk, mv, mo, gc, gr = _inproj(
        x, w["norm_mix"], w["wqk"], w["wav"], w["wmqk"], w["wmvo"], w["wg"], w["wgt"], w["gb"],
        w["gbt"], w["conv"], rc, rs1, rs2, tm=512)
    att = _attn(w["lq"], aq, ak, av, w["attn_gain"], tq=256)
    gc5 = gc.reshape(b, nc, M_CHUNK, M_HEADS, 4).transpose(0, 3, 1, 2, 4)
    gr5 = gr.reshape(b, M_HEADS, 4, nc, M_CHUNK).transpose(0, 1, 3, 2, 4)
    mem = _mlstm(mq, mk, mv, mo, gc5, gr5, w["mlstm_gain"])
    x1, xn, qp = _outproj(x.reshape(t, d), att.reshape(t, ATT_W), mem.reshape(t, M_W),
                          w["wo_a"], w["wo_m"], w["norm_ffn"], w["wq"], tm=512)
    idx, g = _route(qp, w["sk"], tm=128)
    return x1, xn, idx, g


def _peer_all(groups, uv2):
    uv3 = uv2.reshape(-1, D_MODEL // LANES, LANES)
    staged, outs = [], []
    prev = None
    for gi, (xn, idx, g) in enumerate(groups):
        t, d = xn.shape
        num = PEER_SC_SHARE_NUM - (gi == len(groups) - 1)
        ts = (t * num // PEER_SC_SHARE_DEN) // PEER_SC_QUANTUM * PEER_SC_QUANTUM
        assert 0 < ts < t and (t - ts) % PEER_TB == 0, (t, ts)
        staged.append(_stage_rows(uv2, idx[:ts]).reshape(ts, PEER_SEL, d))
        prev = _peer(idx, xn, g, uv3, ts, after=prev)
        outs.append(prev)
    for i, (xn, idx, g) in enumerate(groups):
        prev = _peer_staged(staged[i], xn, g, outs[i], after=prev)
        outs[i] = prev
    return outs


def kernel(x_prompt, x_sample, p_prompt, p_sample, norm_mix, w_in, gate_bias, conv_qk, lambda_qk,
           attn_gain, mlstm_gain, w_out, norm_ffn, peer_wq, peer_subkeys, peer_u, peer_v, norm_ple,
           ple_w_gate, ple_w_proj, norm_final):
    li = 0
    win = w_in[li]
    c0 = 0
    cols = {}
    for name, width in (("aq", ATT_W), ("ak", ATT_W), ("av", ATT_W), ("mq", M_W), ("mk", M_W),
                        ("mv", M_W), ("mo", M_W), ("gt", N_GATES)):
        cols[name] = win[:, c0:c0 + width]
        c0 += width
    perm = jnp.array([ty * M_HEADS + h for h in range(M_HEADS) for ty in range(4)], jnp.int32)
    wg = cols["gt"][:, perm].astype(BF16)
    gb = gate_bias[li][perm]
    w = {
        "norm_mix": norm_mix[li][None, :],
        "wqk": jnp.concatenate([cols["aq"], cols["ak"]], axis=1).astype(BF16),
        "wav": cols["av"].astype(BF16),
        "wmqk": jnp.concatenate([cols["mq"], cols["mk"]], axis=1).astype(BF16),
        "wmvo": jnp.concatenate([cols["mv"], cols["mo"]], axis=1).astype(BF16),
        "wg": wg,
        "wgt": wg.T,
        "gb": gb[None, :],
        "gbt": gb[:, None],
        "conv": conv_qk[li],
        "lq": lambda_qk[li],
        "attn_gain": attn_gain[li][None, :],
        "mlstm_gain": mlstm_gain[li][None, :],
        "wo_a": w_out[li][:ATT_W].astype(BF16),
        "wo_m": w_out[li][ATT_W:].astype(BF16),
        "norm_ffn": norm_ffn[li][None, :],
        "wq": peer_wq[li].astype(BF16),
        "sk": peer_subkeys[li].reshape(PEER_HEADS * 2, N_KEYS, PEER_HALF).astype(BF16),
        "uv": _pack_uv(peer_u[li], peer_v[li]),
        "norm_ple": norm_ple[li][None, :],
        "ple_wg": ple_w_gate[li].astype(BF16),
        "ple_wp": ple_w_proj[li].astype(BF16),
        "norm_final": norm_final[None, :],
    }
    hb = x_prompt.shape[0] // 2
    xs = (x_prompt[:hb], x_prompt[hb:], x_sample)
    ps = (p_prompt[li][:hb], p_prompt[li][hb:], p_sample[li])
    mixed = [_mix_route(x, w) for x in xs]
    pes = _peer_all([(xn, idx, g) for _, xn, idx, g in mixed], w["uv"])
    ys = []
    for x, p, (x1, _, _, _), pe in zip(xs, ps, mixed, pes):
        y = _ple(x1, pe, p.reshape(-1, PLE_DIM), w["norm_ple"], w["ple_wg"], w["ple_wp"],
                 w["norm_final"], tm=512)
        ys.append(y.reshape(x.shape))
    return (jnp.concatenate(ys[:2], axis=0), ys[2])
```

```python
import functools
import math

import jax
import jax.numpy as jnp
from jax import lax
from jax.experimental import pallas as pl
from jax.experimental.pallas import tpu as pltpu
from jax.experimental.pallas import tpu_sc as plsc

F32 = jnp.float32
BF16 = jnp.bfloat16

D_MODEL = 1024
ATT_W = 512
ATT_HEADS = 4
ATT_V_DIM = 128
ATT_QK_DIM = 64
ROT_DIM = 16
ROPE_THETA = 500000.0
M_W = 512
M_HEADS = 4
M_HEAD_DIM = 128
M_CHUNK = 128
CONV_K = 5
N_GATES = 16
PEER_HEADS = 8
PEER_QDIM = 256
PEER_HALF = 128
N_KEYS = 128
PEER_TOPK = 16
PEER_SEL = PEER_HEADS * PEER_TOPK
PLE_DIM = 256
EPS = 1e-6
LAM_INIT = 0.8 - 0.6 * math.exp(-0.3 * 0)
Q_SCALE = ATT_QK_DIM ** -0.5 * math.log2(math.e)

LANES = 128
SUBLANES = 8
VMEM_LIMIT = 56 * 1024 * 1024

NEG_INF = float("-inf")


def _cparams(sem):
    return pltpu.CompilerParams(dimension_semantics=sem, vmem_limit_bytes=VMEM_LIMIT)


def _dot(a, b):
    return jnp.dot(a, b, preferred_element_type=F32)


def _dot_nt(a, b):
    return lax.dot_general(a, b, (((1,), (1,)), ((), ())), preferred_element_type=F32)


def _rms(x, w):
    ms = jnp.mean(x * x, axis=-1, keepdims=True)
    return (x * lax.rsqrt(ms + EPS)) * w


def _sigmoid(x):
    return 1.0 / (1.0 + jnp.exp(-x))


def _log_sigmoid(x):
    return jnp.minimum(x, 0.0) - jnp.log(1.0 + jnp.exp(-jnp.abs(x)))


def _inproj_kernel(x_ref, xp_ref, xn_ref, nw_ref, wqk_ref, wav_ref, wmqk_ref, wmvo_ref,
                   wg_ref, wgt_ref, gb_ref, gbt_ref, cw_ref, rc_ref, rs1_ref, rs2_ref,
                   aq_ref, ak_ref, av_ref, mq_ref, mk_ref, mv_ref, mo_ref, gc_ref, gr_ref):
    i = pl.program_id(1)
    last = pl.num_programs(1) - 1
    nw = nw_ref[...]
    tm = x_ref.shape[1]

    h = _rms(x_ref[0], nw).astype(BF16)
    hp = _rms(xp_ref[0, 0], nw).astype(BF16)
    hn = _rms(xn_ref[0, 0], nw).astype(BF16)

    qk = _dot(h, wqk_ref[...])
    rc = rc_ref[...]
    rs1 = rs1_ref[...]
    rs2 = rs2_ref[...]
    half = ROT_DIM // 2
    for c in range(2 * ATT_W // LANES):
        xc = qk[:, c * LANES:(c + 1) * LANES]
        r = xc * rc + pltpu.roll(xc, half, 1) * rs1 + pltpu.roll(xc, LANES - half, 1) * rs2
        if c < ATT_W // LANES:
            aq_ref[0, :, c * LANES:(c + 1) * LANES] = (r * Q_SCALE).astype(BF16)
        else:
            cc = c - ATT_W // LANES
            ak_ref[0, :, cc * LANES:(cc + 1) * LANES] = r.astype(BF16)

    av_ref[0] = _dot(h, wav_ref[...]).astype(BF16)

    wm = wmqk_ref[...]
    pm = _dot(h, wm)
    pp = _dot(hp, wm) * (i > 0).astype(F32)
    pn = _dot(hn, wm) * (i < last).astype(F32)
    ext = jnp.concatenate([pp, pm, pn], axis=0)
    cw = cw_ref[...]
    off = SUBLANES - CONV_K // 2
    conv = cw[0:1, :] * ext[off:off + tm, :]
    for j in range(1, CONV_K):
        conv = conv + cw[j:j + 1, :] * ext[off + j:off + j + tm, :]
    act = conv * _sigmoid(conv)
    mq_ref[0] = act[:, :M_W].astype(BF16)
    mk_ref[0] = (act[:, M_W:] * (M_HEAD_DIM ** -0.5)).astype(BF16)

    vo = _dot(h, wmvo_ref[...])
    mv_ref[0] = vo[:, :M_W].astype(BF16)
    mo_ref[0] = vo[:, M_W:].astype(BF16)

    gc_ref[0] = _dot(h, wg_ref[...]) + gb_ref[...]
    gr_ref[0] = _dot_nt(wgt_ref[...], h) + gbt_ref[...]


def _inproj(x, norm_w, wqk, wav, wmqk, wmvo, wg, wgt, gb, gbt, conv_w, rc, rs1, rs2, tm):
    b, s, d = x.shape
    nt = s // tm
    x8 = x.reshape(b, s // SUBLANES, SUBLANES, d)
    r8 = tm // SUBLANES
    nb8 = s // SUBLANES
    full = lambda shape: pl.BlockSpec(shape, lambda bi, i: (0,) * len(shape))
    tok = lambda w: pl.BlockSpec((1, tm, w), lambda bi, i: (bi, i, 0))
    out_shapes = (
        [jax.ShapeDtypeStruct((b, s, ATT_W), BF16)] * 3
        + [jax.ShapeDtypeStruct((b, s, M_W), BF16)] * 4
        + [jax.ShapeDtypeStruct((b, s, N_GATES), F32), jax.ShapeDtypeStruct((b, N_GATES, s), F32)]
    )
    return pl.pallas_call(
        _inproj_kernel,
        out_shape=out_shapes,
        grid=(b, nt),
        in_specs=[
            tok(d),
            pl.BlockSpec((1, 1, SUBLANES, d), lambda bi, i: (bi, jnp.maximum(i * r8 - 1, 0), 0, 0)),
            pl.BlockSpec((1, 1, SUBLANES, d), lambda bi, i: (bi, jnp.minimum((i + 1) * r8, nb8 - 1), 0, 0)),
            full((1, d)),
            full(wqk.shape), full(wav.shape), full(wmqk.shape), full(wmvo.shape),
            full(wg.shape), full(wgt.shape), full(gb.shape), full(gbt.shape), full(conv_w.shape),
            pl.BlockSpec((tm, LANES), lambda bi, i: (i, 0)),
            pl.BlockSpec((tm, LANES), lambda bi, i: (i, 0)),
            pl.BlockSpec((tm, LANES), lambda bi, i: (i, 0)),
        ],
        out_specs=[tok(ATT_W)] * 3 + [tok(M_W)] * 4
        + [tok(N_GATES), pl.BlockSpec((1, N_GATES, tm), lambda bi, i: (bi, 0, i))],
        compiler_params=_cparams(("parallel", "parallel")),
        name="inproj",
    )(x, x8, x8, norm_w, wqk, wav, wmqk, wmvo, wg, wgt, gb, gbt, conv_w, rc, rs1, rs2)


ATT_KEY_BLOCK = 1024


def _attn_kernel(lq_ref, q_ref, k_ref, v_ref, gain_ref, o_ref):
    lq = lq_ref[...]
    lam = (jnp.exp(jnp.sum(lq[0:1] * lq[1:2], axis=1, keepdims=True))
           - jnp.exp(jnp.sum(lq[2:3] * lq[3:4], axis=1, keepdims=True)) + LAM_INIT)
    q = q_ref[0]
    tq = q.shape[0]
    lane = lax.broadcasted_iota(jnp.int32, q.shape, 1)
    zero = jnp.zeros_like(q)
    qq = jnp.concatenate([jnp.where(lane < ATT_QK_DIM, q, zero),
                          jnp.where(lane >= ATT_QK_DIM, q, zero)], axis=0)
    m = jnp.full((2 * tq, 1), NEG_INF, F32)
    ln = jnp.zeros((2 * tq, 1), F32)
    acc = jnp.zeros((2 * tq, ATT_V_DIM), F32)
    kb = min(ATT_KEY_BLOCK, k_ref.shape[1])
    assert k_ref.shape[1] % kb == 0
    for c in range(k_ref.shape[1] // kb):
        ks = slice(c * kb, (c + 1) * kb)
        sc = _dot_nt(qq, k_ref[0, ks, :])
        m_new = jnp.maximum(m, jnp.max(sc, axis=1, keepdims=True))
        alpha = jnp.exp2(m - m_new)
        e = jnp.exp2(sc - m_new)
        ln = alpha * ln + jnp.sum(e, axis=1, keepdims=True)
        acc = alpha * acc + _dot(e.astype(BF16), v_ref[0, ks, :])
        m = m_new
    o = acc[:tq] * (1.0 / ln[:tq]) - acc[tq:] * (lam / ln[tq:])
    o_ref[0] = (_rms(o, gain_ref[...]) * (1.0 - LAM_INIT)).astype(BF16)


def _attn(lq, q, k, v, gain, tq):
    b, s, _ = q.shape
    return pl.pallas_call(
        _attn_kernel,
        out_shape=jax.ShapeDtypeStruct((b, s, ATT_W), BF16),
        grid=(b, ATT_HEADS, s // tq),
        in_specs=[
            pl.BlockSpec(lq.shape, lambda bi, h, i: (0, 0)),
            pl.BlockSpec((1, tq, ATT_V_DIM), lambda bi, h, i: (bi, i, h)),
            pl.BlockSpec((1, s, ATT_V_DIM), lambda bi, h, i: (bi, 0, h)),
            pl.BlockSpec((1, s, ATT_V_DIM), lambda bi, h, i: (bi, 0, h)),
            pl.BlockSpec((1, ATT_V_DIM), lambda bi, h, i: (0, 0)),
        ],
        out_specs=pl.BlockSpec((1, tq, ATT_V_DIM), lambda bi, h, i: (bi, i, h)),
        compiler_params=_cparams(("parallel", "parallel", "parallel")),
        name="attn",
    )(lq, q, k, v, gain)


def _mlstm_chunk(q, k, v, ic_col, lf_col, ic_row, lf_row, cst, n, m, fwd):
    ln = q.shape[0]
    row = lax.broadcasted_iota(jnp.int32, (ln, ln), 0)
    col = lax.broadcasted_iota(jnp.int32, (ln, ln), 1)
    lower = col <= row
    upper = col >= row
    mask, mask_t = (lower, upper) if fwd else (upper, lower)
    bcum_col = jnp.sum(jnp.where(mask, lf_row, 0.0), axis=1, keepdims=True)
    bcum_row = jnp.sum(jnp.where(mask_t, lf_col, 0.0), axis=0, keepdims=True)
    dlog = jnp.where(mask, bcum_col - bcum_row + ic_row, NEG_INF)
    inter = bcum_col + m
    m_t = jnp.maximum(inter, jnp.max(dlog, axis=1, keepdims=True))
    w = jnp.exp(dlog - m_t)
    sc = jnp.exp(inter - m_t)
    wqk = w * _dot_nt(q, k)
    num = sc * _dot(q, cst.astype(BF16)) + _dot(wqk.astype(BF16), v)
    qn = jnp.sum(q.astype(F32) * n, axis=1, keepdims=True)
    den = sc * qn + jnp.sum(wqk, axis=1, keepdims=True)
    h = num / jnp.maximum(jnp.abs(den), jnp.exp(-m_t))
    b_last = jnp.sum(lf_row, axis=1, keepdims=True)
    rel_row = b_last - bcum_row + ic_row
    rel_col = b_last - bcum_col + ic_col
    m_new = jnp.maximum(b_last + m, jnp.max(rel_row, axis=1, keepdims=True))
    dec_col = jnp.exp(rel_col - m_new)
    keep = jnp.exp(b_last + m - m_new)
    kd = k.astype(F32) * dec_col
    c_new = keep * cst + lax.dot_general(kd.astype(BF16), v, (((0,), (0,)), ((), ())),
                                         preferred_element_type=F32)
    n_new = keep * n + jnp.sum(kd, axis=0, keepdims=True)
    return h, c_new, n_new, m_new


M_HEADS_PER_STEP = 1


def _mlstm_kernel(q_ref, k_ref, v_ref, o_ref, gc_ref, gr_ref, gain_ref, out_ref, hf_ref, hb_ref):
    s = q_ref.shape[1]
    nc = s // M_CHUNK
    d = M_HEAD_DIM
    heads = range(M_HEADS_PER_STEP)

    def step(j, c, carry, fwd):
        cst, n, m = carry
        sl = pl.ds(pl.multiple_of(c * M_CHUNK, M_CHUNK), M_CHUNK)
        ln = slice(j * d, (j + 1) * d)
        gc = gc_ref[0, j, c]
        gr = gr_ref[0, j, c]
        t = 0 if fwd else 2
        h, cst, n, m = _mlstm_chunk(
            q_ref[0, sl, ln], k_ref[0, sl, ln], v_ref[0, sl, ln],
            gc[:, t:t + 1], _log_sigmoid(gc[:, t + 1:t + 2]),
            gr[t:t + 1, :], _log_sigmoid(gr[t + 1:t + 2, :]),
            cst, n, m, fwd)
        return sl, ln, h, (cst, n, m)

    init = (jnp.zeros((d, d), F32), jnp.zeros((1, d), F32), jnp.zeros((1, 1), F32))

    def scan_body(c, carry):
        new = []
        for j in heads:
            cf, cb = carry[j]
            sl, ln, h, cf = step(j, c, cf, True)
            hf_ref[sl, ln] = h
            sl, ln, h, cb = step(j, nc - 1 - c, cb, False)
            hb_ref[sl, ln] = h
            new.append((cf, cb))
        return tuple(new)

    lax.fori_loop(0, nc, scan_body, tuple((init, init) for _ in heads))

    def gate_body(c, carry):
        sl = pl.ds(pl.multiple_of(c * M_CHUNK, M_CHUNK), M_CHUNK)
        for j in heads:
            ln = slice(j * d, (j + 1) * d)
            y = _sigmoid(o_ref[0, sl, ln].astype(F32)) * (hf_ref[sl, ln] + hb_ref[sl, ln])
            out_ref[0, sl, ln] = _rms(y, gain_ref[:, ln]).astype(BF16)
        return carry

    lax.fori_loop(0, nc, gate_body, 0)


def _mlstm(q, k, v, o, gc, gr, gain):
    b, s, _ = q.shape
    nc = s // M_CHUNK
    hp = M_HEADS_PER_STEP
    w = hp * M_HEAD_DIM
    head = pl.BlockSpec((1, s, w), lambda bi, h: (bi, 0, h))
    return pl.pallas_call(
        _mlstm_kernel,
        out_shape=jax.ShapeDtypeStruct((b, s, M_W), BF16),
        grid=(b, M_HEADS // hp),
        in_specs=[
            head, head, head, head,
            pl.BlockSpec((1, hp, nc, M_CHUNK, 4), lambda bi, h: (bi, h, 0, 0, 0)),
            pl.BlockSpec((1, hp, nc, 4, M_CHUNK), lambda bi, h: (bi, h, 0, 0, 0)),
            pl.BlockSpec((1, w), lambda bi, h: (0, h)),
        ],
        out_specs=head,
        scratch_shapes=[pltpu.VMEM((s, w), F32), pltpu.VMEM((s, w), F32)],
        compiler_params=_cparams(("parallel", "parallel")),
        name="mlstm",
    )(q, k, v, o, gc, gr, gain)


def _outproj_kernel(x_ref, att_ref, mem_ref, wa_ref, wm_ref, nw_ref, wq_ref, x1_ref, xn_ref, qp_ref):
    x1 = x_ref[...] + _dot(att_ref[...], wa_ref[...]) + _dot(mem_ref[...], wm_ref[...])
    x1_ref[...] = x1
    xn = _rms(x1, nw_ref[...])
    xn_ref[...] = xn
    qp = _dot(xn.astype(BF16), wq_ref[...])
    for c in range(qp_ref.shape[0]):
        qp_ref[c] = qp[:, c * LANES:(c + 1) * LANES].astype(BF16)


def _outproj(x, att, mem, wa, wm, nw, wq, tm):
    t, d = x.shape
    nq = wq.shape[1] // LANES
    full = lambda shape: pl.BlockSpec(shape, lambda i: (0,) * len(shape))
    tok = lambda w: pl.BlockSpec((tm, w), lambda i: (i, 0))
    return pl.pallas_call(
        _outproj_kernel,
        out_shape=[jax.ShapeDtypeStruct((t, d), F32), jax.ShapeDtypeStruct((t, d), F32),
                   jax.ShapeDtypeStruct((nq, t, LANES), BF16)],
        grid=(t // tm,),
        in_specs=[tok(d), tok(ATT_W), tok(M_W), full(wa.shape), full(wm.shape), full(nw.shape),
                  full(wq.shape)],
        out_specs=[tok(d), tok(d), pl.BlockSpec((nq, tm, LANES), lambda i: (0, i, 0))],
        compiler_params=_cparams(("parallel",)),
        name="outproj",
    )(x, att, mem, wa, wm, nw, wq)


ROUTE_HEADS_PER_ITER = 8


def _topk_rows(sc, n):
    pos = lax.broadcasted_iota(jnp.int32, sc.shape, 0).astype(F32)
    vals, poss = [], []
    for _ in range(PEER_TOPK):
        m = jnp.max(sc, axis=0, keepdims=True)
        p = jnp.min(jnp.where(sc == m, pos, float(n)), axis=0, keepdims=True)
        vals.append(m)
        poss.append(p)
        sc = jnp.where(pos == p, NEG_INF, sc)
    return jnp.concatenate(vals, axis=0), jnp.concatenate(poss, axis=0).astype(jnp.int32)


def _route_kernel(qp_ref, sk_ref, idx_ref, g_ref, idx_t, g_t):
    def head(h, carry):
        s1 = _dot_nt(sk_ref[2 * h], qp_ref[2 * h])
        s2 = _dot_nt(sk_ref[2 * h + 1], qp_ref[2 * h + 1])
        v1, i1 = _topk_rows(s1, N_KEYS)
        v2, i2 = _topk_rows(s2, N_KEYS)
        k = PEER_TOPK
        sub = SUBLANES
        brow = lax.broadcasted_iota(jnp.int32, (sub, v2.shape[1]), 0)
        slabs = [v1[0:1, :] + v2]
        for a in range(1, sub):
            slabs.append(jnp.where(brow < k // (a + 1), v1[a:a + 1, :] + v2[0:sub, :], NEG_INF))
        slabs.append(v1[sub:k, :] + v2[0:1, :])
        cand = jnp.concatenate(slabs, axis=0)
        sc, cp = _topk_rows(cand, cand.shape[0])
        mid = cp < k + sub * (sub - 1)
        slab = lax.shift_right_logical(cp, int(math.log2(sub)))
        ca = jnp.where(cp < k, 0, jnp.where(mid, slab - 1, cp - (k + sub * (sub - 1)) + sub))
        cb = jnp.where(cp < k, cp, jnp.where(mid, cp & (sub - 1), 0))
        e1 = jnp.zeros_like(cp)
        e2 = jnp.zeros_like(cp)
        for a in range(PEER_TOPK):
            e1 = jnp.where(ca == a, i1[a:a + 1, :], e1)
            e2 = jnp.where(cb == a, i2[a:a + 1, :], e2)
        ex = jnp.exp(sc - sc[0:1, :])
        rows = pl.ds(pl.multiple_of(h * PEER_TOPK, PEER_TOPK), PEER_TOPK)
        idx_t[rows, :] = e1 * N_KEYS + e2
        g_t[rows, :] = ex / jnp.sum(ex, axis=0, keepdims=True)
        return carry

    def heads(hh, carry):
        for j in range(ROUTE_HEADS_PER_ITER):
            head(hh * ROUTE_HEADS_PER_ITER + j, carry)
        return carry

    lax.fori_loop(0, PEER_HEADS // ROUTE_HEADS_PER_ITER, heads, 0)
    idx_ref[...] = idx_t[...].T
    g_ref[...] = g_t[...].T


def _route(qp, sk, tm):
    nq, t, _ = qp.shape
    return pl.pallas_call(
        _route_kernel,
        out_shape=[jax.ShapeDtypeStruct((t, PEER_SEL), jnp.int32),
                   jax.ShapeDtypeStruct((t, PEER_SEL), F32)],
        grid=(t // tm,),
        in_specs=[pl.BlockSpec((nq, tm, LANES), lambda i: (0, i, 0)),
                  pl.BlockSpec(sk.shape, lambda i: (0, 0, 0))],
        out_specs=[pl.BlockSpec((tm, PEER_SEL), lambda i: (i, 0))] * 2,
        scratch_shapes=[pltpu.VMEM((PEER_SEL, tm), jnp.int32), pltpu.VMEM((PEER_SEL, tm), F32)],
        compiler_params=_cparams(("parallel",)),
        name="route",
    )(qp, sk)


PEER_TB = 512
PEER_NB = SUBLANES
PEER_STAGED_TB = 16
PEER_NG = 4
PEER_LA = 2
HI16 = -65536


def _gelu(a):
    return 0.5 * a * (1.0 + lax.erf(a * (2.0 ** -0.5)))


def _peer_token(xr, gr, tile):
    nch = D_MODEL // LANES
    nkt = PEER_SEL // SUBLANES
    xb = [jnp.broadcast_to(xr[:, c * LANES:(c + 1) * LANES], (SUBLANES, LANES)) for c in range(nch)]
    parts = []
    for kt in range(nkt):
        acc = None
        for c in range(nch):
            u = lax.bitcast_convert_type(tile(kt, c) & HI16, F32)
            acc = u * xb[c] if acc is None else acc + u * xb[c]
        parts.append(acc)
    part = jnp.concatenate(parts, axis=0)
    a_row = jnp.sum(part.T, axis=0, keepdims=True)
    act = _gelu(a_row) * gr
    act_b = jnp.broadcast_to(act, (LANES, PEER_SEL)).T
    outs = []
    for c in range(nch):
        acc = None
        for kt in range(nkt):
            v = lax.bitcast_convert_type(tile(kt, c) << 16, F32)
            t = act_b[kt * SUBLANES:(kt + 1) * SUBLANES, :] * v
            acc = t if acc is None else acc + t
        outs.append(jnp.sum(acc, axis=0, keepdims=True))
    return jnp.concatenate(outs, axis=1)


def _peer_staged_kernel(st_ref, xn_ref, g_ref, rest_ref, after_ref, out_ref):
    del rest_ref, after_ref
    xg = xn_ref[...]
    gg = g_ref[...]
    rows = []
    for tb in range(xn_ref.shape[0]):
        def tile(kt, c, tb=tb):
            return st_ref[tb, kt * SUBLANES:(kt + 1) * SUBLANES, c * LANES:(c + 1) * LANES]
        rows.append(_peer_token(xg[tb:tb + 1, :], gg[tb:tb + 1, :], tile))
    out_ref[...] = jnp.concatenate(rows, axis=0)


def _peer_staged(staged, xn, g, rest, after=None):
    ts = staged.shape[0]
    t, d = xn.shape
    tb = PEER_STAGED_TB
    return pl.pallas_call(
        _peer_staged_kernel,
        out_shape=jax.ShapeDtypeStruct((t, d), F32),
        grid=(ts // tb,),
        in_specs=[pl.BlockSpec((tb, PEER_SEL, d), lambda i: (i, 0, 0)),
                  pl.BlockSpec((tb, d), lambda i: (i, 0)),
                  pl.BlockSpec((tb, PEER_SEL), lambda i: (i, 0)),
                  pl.BlockSpec(memory_space=pl.ANY),
                  pl.BlockSpec(memory_space=pl.ANY)],
        out_specs=pl.BlockSpec((tb, d), lambda i: (i, 0)),
        input_output_aliases={3: 0},
        compiler_params=_cparams(("parallel",)),
        name="peer_staged",
    )(staged, xn, g, rest, _order_after(after))


SC_CORES = 2
SC_SUBCORES = 16
SC_WORKERS = SC_CORES * SC_SUBCORES
SC_ROWS = 32
SC_IDX_BLK = 128
PEER_SC_SHARE_NUM = 11
PEER_SC_SHARE_DEN = 16
PEER_SC_QUANTUM = SC_WORKERS * SC_ROWS * SC_IDX_BLK // PEER_SEL


def _stage_rows(uv2, idx):
    n_tok, nsel = idx.shape
    n_rows = n_tok * nsel
    rows_per_w = n_rows // SC_WORKERS
    n_blk = rows_per_w // (SC_ROWS * SC_IDX_BLK)
    idx4 = idx.reshape(SC_WORKERS, n_blk, SC_IDX_BLK, SC_ROWS)
    mesh = plsc.VectorSubcoreMesh(core_axis_name="c", subcore_axis_name="s")

    @functools.partial(
        pl.kernel, mesh=mesh,
        out_type=jax.ShapeDtypeStruct((n_rows, uv2.shape[1]), uv2.dtype),
        scratch_types=[
            pltpu.VMEM((SC_IDX_BLK, SC_ROWS), jnp.int32),
            pltpu.VMEM((2, SC_ROWS, uv2.shape[1]), uv2.dtype),
            pltpu.SemaphoreType.DMA((2,)),
            pltpu.SemaphoreType.DMA((2,)),
        ],
    )
    def stage(uv_hbm, idx_hbm, out_hbm, idx_v, rows_v, gsem, wsem):
        wid = lax.axis_index("s") * SC_CORES + lax.axis_index("c")
        base = wid * rows_per_w

        def gather(j, b):
            return pltpu.make_async_copy(uv_hbm.at[idx_v.at[j]], rows_v.at[b], gsem.at[b])

        def write(row0, b):
            return pltpu.make_async_copy(rows_v.at[b], out_hbm.at[pl.ds(row0, SC_ROWS)], wsem.at[b])

        @pl.loop(0, n_blk)
        def _(blk):
            pltpu.sync_copy(idx_hbm.at[wid, blk], idx_v)
            blk_row0 = base + blk * (SC_IDX_BLK * SC_ROWS)
            gather(0, 0).start()

            @pl.loop(0, SC_IDX_BLK, step=2)
            def _(j):
                for b in range(2):
                    jj = j + b
                    row0 = pl.multiple_of(blk_row0 + jj * SC_ROWS, SC_ROWS)
                    gather(jj, b).wait()

                    @pl.when(jj > 0)
                    def _():
                        write(row0 - SC_ROWS, 1 - b).wait()

                    @pl.when(jj + 1 < SC_IDX_BLK)
                    def _():
                        gather(jj + 1, 1 - b).start()

                    write(row0, b).start()

            write(blk_row0 + (SC_IDX_BLK - 1) * SC_ROWS, 1).wait()

    return stage(uv2, idx4)


def _peer_kernel(step0, idx_hbm, xn_ref, g_ref, uv_hbm, after_ref, out_ref, idx_s, buf, sem_i, sem):
    del after_ref
    step = pl.program_id(0) + step0
    nbatch = xn_ref.shape[0] // PEER_NB
    nch = D_MODEL // LANES
    nkt = PEER_SEL // SUBLANES

    cp = pltpu.make_async_copy(idx_hbm.at[step], idx_s, sem_i.at[0])
    cp.start()
    cp.wait()

    def issue(b):
        grp = b & (PEER_NG - 1)

        def tok_body(tb, carry):
            base = (b * PEER_NB + tb) * PEER_SEL
            for k in range(PEER_SEL):
                e = idx_s[base + k]
                pltpu.make_async_copy(
                    uv_hbm.at[e], buf.at[grp, tb, k // SUBLANES, :, k % SUBLANES, :], sem.at[grp]
                ).start(priority=k % 2)
            return carry

        lax.fori_loop(0, PEER_NB, tok_body, 0)

    def wait(b):
        grp = b & (PEER_NG - 1)
        pltpu.make_async_copy(buf.at[grp], buf.at[grp], sem.at[grp]).wait()

    def compute(b):
        grp = b & (PEER_NG - 1)
        r0 = pl.multiple_of(b * PEER_NB, PEER_NB)
        xg = xn_ref[pl.ds(r0, PEER_NB), :]
        gg = g_ref[pl.ds(r0, PEER_NB), :]
        rows = [_peer_token(xg[tb:tb + 1, :], gg[tb:tb + 1, :],
                            functools.partial(lambda tb, kt, c: buf[grp, tb, kt, c], tb))
                for tb in range(PEER_NB)]
        out_ref[pl.ds(r0, PEER_NB), :] = jnp.concatenate(rows, axis=0)

    for b in range(PEER_LA):
        issue(b)

    def body(b, carry):
        wait(b)

        @pl.when(b + PEER_LA < nbatch)
        def _():
            issue(b + PEER_LA)

        compute(b)
        return carry

    lax.fori_loop(0, nbatch, body, 0)


def _order_after(after):
    return jnp.zeros((SUBLANES, LANES), F32) if after is None else after


def _peer(idx, xn, g, uv, ts, after=None):
    t, d = xn.shape
    tb = PEER_TB
    idx2 = idx.reshape(t // tb, tb * PEER_SEL)
    s0 = ts // tb
    return pl.pallas_call(
        functools.partial(_peer_kernel, s0),
        out_shape=jax.ShapeDtypeStruct((t, d), F32),
        grid=((t - ts) // tb,),
        in_specs=[pl.BlockSpec(memory_space=pl.ANY),
                  pl.BlockSpec((tb, d), lambda i: (i + s0, 0)),
                  pl.BlockSpec((tb, PEER_SEL), lambda i: (i + s0, 0)),
                  pl.BlockSpec(memory_space=pl.ANY),
                  pl.BlockSpec(memory_space=pl.ANY)],
        out_specs=pl.BlockSpec((tb, d), lambda i: (i + s0, 0)),
        scratch_shapes=[pltpu.SMEM((tb * PEER_SEL,), jnp.int32),
                        pltpu.VMEM((PEER_NG, PEER_NB, PEER_SEL // SUBLANES, d // LANES, SUBLANES, LANES),
                                   jnp.int32),
                        pltpu.SemaphoreType.DMA((1,)),
                        pltpu.SemaphoreType.DMA((PEER_NG,))],
        compiler_params=_cparams(("arbitrary",)),
        name="peer",
    )(idx2, xn, g, uv, _order_after(after))


def _ple_kernel(x1_ref, pe_ref, p_ref, nw_ref, wg_ref, wp_ref, nf_ref, y_ref):
    x2 = x1_ref[...] + pe_ref[...]
    gate = _sigmoid(_dot(_rms(x2, nw_ref[...]).astype(BF16), wg_ref[...]))
    x3 = x2 + _dot(p_ref[...].astype(BF16), wp_ref[...]) * gate
    y_ref[...] = _rms(x3, nf_ref[...])


def _ple(x1, pe, p, nw, wg, wp, nf, tm):
    t, d = x1.shape
    full = lambda shape: pl.BlockSpec(shape, lambda i: (0,) * len(shape))
    tok = lambda w: pl.BlockSpec((tm, w), lambda i: (i, 0))
    return pl.pallas_call(
        _ple_kernel,
        out_shape=jax.ShapeDtypeStruct((t, d), F32),
        grid=(t // tm,),
        in_specs=[tok(d), tok(d), tok(PLE_DIM), full(nw.shape), full(wg.shape), full(wp.shape),
                  full(nf.shape)],
        out_specs=tok(d),
        compiler_params=_cparams(("parallel",)),
        name="ple",
    )(x1, pe, p, nw, wg, wp, nf)


def _rope_tables(s):
    half = ROT_DIM // 2
    inv = ROPE_THETA ** (-jnp.arange(0, ROT_DIM, 2, dtype=F32) / ROT_DIM)
    ang = jnp.arange(s, dtype=F32)[:, None] * inv[None, :]
    cos, sin = jnp.cos(ang), jnp.sin(ang)
    ones = jnp.ones((s, ATT_QK_DIM - ROT_DIM), F32)
    zeros = jnp.zeros((s, ATT_QK_DIM - ROT_DIM), F32)
    zh = jnp.zeros((s, half), F32)
    c64 = jnp.concatenate([cos, cos, ones], axis=1)
    s1_64 = jnp.concatenate([zh, sin, zeros], axis=1)
    s2_64 = jnp.concatenate([-sin, zh, zeros], axis=1)
    rep = LANES // ATT_QK_DIM
    return jnp.tile(c64, (1, rep)), jnp.tile(s1_64, (1, rep)), jnp.tile(s2_64, (1, rep))


def _pack_uv(u, v):
    ub = lax.bitcast_convert_type(u.astype(BF16), jnp.uint16).astype(jnp.uint32)
    vb = lax.bitcast_convert_type(v.astype(BF16), jnp.uint16).astype(jnp.uint32)
    return lax.bitcast_convert_type((ub << 16) | vb, jnp.int32)


def _mix_route(x, w):
    b, s, d = x.shape
    t = b * s
    nc = s // M_CHUNK
    rc, rs1, rs2 = _rope_tables(s)
    aq, ak, av, mq, mk, mv, mo, gc, gr = _inproj(
        x, w["norm_mix"], w["wqk"], w["wav"], w["wmqk"], w["wmvo"], w["wg"], w["wgt"], w["gb"],
        w["gbt"], w["conv"], rc, rs1, rs2, tm=512)
    att = _attn(w["lq"], aq, ak, av, w["attn_gain"], tq=256)
    gc5 = gc.reshape(b, nc, M_CHUNK, M_HEADS, 4).transpose(0, 3, 1, 2, 4)
    gr5 = gr.reshape(b, M_HEADS, 4, nc, M_CHUNK).transpose(0, 1, 3, 2, 4)
    mem = _mlstm(mq, mk, mv, mo, gc5, gr5, w["mlstm_gain"])
    x1, xn, qp = _outproj(x.reshape(t, d), att.reshape(t, ATT_W), mem.reshape(t, M_W),
                          w["wo_a"], w["wo_m"], w["norm_ffn"], w["wq"], tm=512)
    idx, g = _route(qp, w["sk"], tm=128)
    return x1, xn, idx, g


def _peer_all(groups, uv2):
    uv3 = uv2.reshape(-1, D_MODEL // LANES, LANES)
    staged, outs = [], []
    prev = None
    for gi, (xn, idx, g) in enumerate(groups):
        t, d = xn.shape
        num = PEER_SC_SHARE_NUM - (gi == len(groups) - 1)
        ts = (t * num // PEER_SC_SHARE_DEN) // PEER_SC_QUANTUM * PEER_SC_QUANTUM
        assert 0 < ts < t and (t - ts) % PEER_TB == 0, (t, ts)
        staged.append(_stage_rows(uv2, idx[:ts]).reshape(ts, PEER_SEL, d))
        prev = _peer(idx, xn, g, uv3, ts, after=prev)
        outs.append(prev)
    for i, (xn, idx, g) in enumerate(groups):
        prev = _peer_staged(staged[i], xn, g, outs[i], after=prev)
        outs[i] = prev
    return outs


def kernel(x_prompt, x_sample, p_prompt, p_sample, norm_mix, w_in, gate_bias, conv_qk, lambda_qk,
           attn_gain, mlstm_gain, w_out, norm_ffn, peer_wq, peer_subkeys, peer_u, peer_v, norm_ple,
           ple_w_gate, ple_w_proj, norm_final):
    li = 0
    win = w_in[li]
    c0 = 0
    cols = {}
    for name, width in (("aq", ATT_W), ("ak", ATT_W), ("av", ATT_W), ("mq", M_W), ("mk", M_W),
                        ("mv", M_W), ("mo", M_W), ("gt", N_GATES)):
        cols[name] = win[:, c0:c0 + width]
        c0 += width
    perm = jnp.array([ty * M_HEADS + h for h in range(M_HEADS) for ty in range(4)], jnp.int32)
    wg = cols["gt"][:, perm].astype(BF16)
    gb = gate_bias[li][perm]
    w = {
        "norm_mix": norm_mix[li][None, :],
        "wqk": jnp.concatenate([cols["aq"], cols["ak"]], axis=1).astype(BF16),
        "wav": cols["av"].astype(BF16),
        "wmqk": jnp.concatenate([cols["mq"], cols["mk"]], axis=1).astype(BF16),
        "wmvo": jnp.concatenate([cols["mv"], cols["mo"]], axis=1).astype(BF16),
        "wg": wg,
        "wgt": wg.T,
        "gb": gb[None, :],
        "gbt": gb[:, None],
        "conv": conv_qk[li],
        "lq": lambda_qk[li],
        "attn_gain": attn_gain[li][None, :],
        "mlstm_gain": mlstm_gain[li][None, :],
        "wo_a": w_out[li][:ATT_W].astype(BF16),
        "wo_m": w_out[li][ATT_W:].astype(BF16),
        "norm_ffn": norm_ffn[li][None, :],
        "wq": peer_wq[li].astype(BF16),
        "sk": peer_subkeys[li].reshape(PEER_HEADS * 2, N_KEYS, PEER_HALF).astype(BF16),
        "uv": _pack_uv(peer_u[li], peer_v[li]),
        "norm_ple": norm_ple[li][None, :],
        "ple_wg": ple_w_gate[li].astype(BF16),
        "ple_wp": ple_w_proj[li].astype(BF16),
        "norm_final": norm_final[None, :],
    }
    hb = x_prompt.shape[0] // 2
    xs = (x_prompt[:hb], x_prompt[hb:], x_sample)
    ps = (p_prompt[li][:hb], p_prompt[li][hb:], p_sample[li])
    mixed = [_mix_route(x, w) for x in xs]
    pes = _peer_all([(xn, idx, g) for _, xn, idx, g in mixed], w["uv"])
    ys = []
    for x, p, (x1, _, _, _), pe in zip(xs, ps, mixed, pes):
        y = _ple(x1, pe, p.reshape(-1, PLE_DIM), w["norm_ple"], w["ple_wg"], w["ple_wp"],
                 w["norm_final"], tm=512)
        ys.append(y.reshape(x.shape))
    return (jnp.concatenate(ys[:2], axis=0), ys[2])
```

```python
import functools
import math

import jax
import jax.numpy as jnp
from jax import lax
from jax.experimental import pallas as pl
from jax.experimental.pallas import tpu as pltpu
from jax.experimental.pallas import tpu_sc as plsc

F32 = jnp.float32
BF16 = jnp.bfloat16

D_MODEL = 1024
ATT_W = 512
ATT_HEADS = 4
ATT_V_DIM = 128
ATT_QK_DIM = 64
ROT_DIM = 16
ROPE_THETA = 500000.0
M_W = 512
M_HEADS = 4
M_HEAD_DIM = 128
M_CHUNK = 128
CONV_K = 5
N_GATES = 16
PEER_HEADS = 8
PEER_QDIM = 256
PEER_HALF = 128
N_KEYS = 128
PEER_TOPK = 16
PEER_SEL = PEER_HEADS * PEER_TOPK
PLE_DIM = 256
EPS = 1e-6
LAM_INIT = 0.8 - 0.6 * math.exp(-0.3 * 0)
Q_SCALE = ATT_QK_DIM ** -0.5 * math.log2(math.e)

LANES = 128
SUBLANES = 8
VMEM_LIMIT = 56 * 1024 * 1024

NEG_INF = float("-inf")


def _cparams(sem):
    return pltpu.CompilerParams(dimension_semantics=sem, vmem_limit_bytes=VMEM_LIMIT)


def _dot(a, b):
    return jnp.dot(a, b, preferred_element_type=F32)


def _dot_nt(a, b):
    return lax.dot_general(a, b, (((1,), (1,)), ((), ())), preferred_element_type=F32)


def _rms(x, w):
    ms = jnp.mean(x * x, axis=-1, keepdims=True)
    return (x * lax.rsqrt(ms + EPS)) * w


def _sigmoid(x):
    return 1.0 / (1.0 + jnp.exp(-x))


def _log_sigmoid(x):
    return jnp.minimum(x, 0.0) - jnp.log(1.0 + jnp.exp(-jnp.abs(x)))


def _inproj_kernel(x_ref, xp_ref, xn_ref, nw_ref, wqk_ref, wav_ref, wmqk_ref, wmvo_ref,
                   wg_ref, wgt_ref, gb_ref, gbt_ref, cw_ref, rc_ref, rs1_ref, rs2_ref,
                   aq_ref, ak_ref, av_ref, mq_ref, mk_ref, mv_ref, mo_ref, gc_ref, gr_ref):
    i = pl.program_id(1)
    last = pl.num_programs(1) - 1
    nw = nw_ref[...]
    tm = x_ref.shape[1]

    h = _rms(x_ref[0], nw).astype(BF16)
    hp = _rms(xp_ref[0, 0], nw).astype(BF16)
    hn = _rms(xn_ref[0, 0], nw).astype(BF16)

    qk = _dot(h, wqk_ref[...])
    rc = rc_ref[...]
    rs1 = rs1_ref[...]
    rs2 = rs2_ref[...]
    half = ROT_DIM // 2
    for c in range(2 * ATT_W // LANES):
        xc = qk[:, c * LANES:(c + 1) * LANES]
        r = xc * rc + pltpu.roll(xc, half, 1) * rs1 + pltpu.roll(xc, LANES - half, 1) * rs2
        if c < ATT_W // LANES:
            aq_ref[0, :, c * LANES:(c + 1) * LANES] = (r * Q_SCALE).astype(BF16)
        else:
            cc = c - ATT_W // LANES
            ak_ref[0, :, cc * LANES:(cc + 1) * LANES] = r.astype(BF16)

    av_ref[0] = _dot(h, wav_ref[...]).astype(BF16)

    wm = wmqk_ref[...]
    pm = _dot(h, wm)
    pp = _dot(hp, wm) * (i > 0).astype(F32)
    pn = _dot(hn, wm) * (i < last).astype(F32)
    ext = jnp.concatenate([pp, pm, pn], axis=0)
    cw = cw_ref[...]
    off = SUBLANES - CONV_K // 2
    conv = cw[0:1, :] * ext[off:off + tm, :]
    for j in range(1, CONV_K):
        conv = conv + cw[j:j + 1, :] * ext[off + j:off + j + tm, :]
    act = conv * _sigmoid(conv)
    mq_ref[0] = act[:, :M_W].astype(BF16)
    mk_ref[0] = (act[:, M_W:] * (M_HEAD_DIM ** -0.5)).astype(BF16)

    vo = _dot(h, wmvo_ref[...])
    mv_ref[0] = vo[:, :M_W].astype(BF16)
    mo_ref[0] = vo[:, M_W:].astype(BF16)

    gc_ref[0] = _dot(h, wg_ref[...]) + gb_ref[...]
    gr_ref[0] = _dot_nt(wgt_ref[...], h) + gbt_ref[...]


def _inproj(x, norm_w, wqk, wav, wmqk, wmvo, wg, wgt, gb, gbt, conv_w, rc, rs1, rs2, tm):
    b, s, d = x.shape
    nt = s // tm
    x8 = x.reshape(b, s // SUBLANES, SUBLANES, d)
    r8 = tm // SUBLANES
    nb8 = s // SUBLANES
    full = lambda shape: pl.BlockSpec(shape, lambda bi, i: (0,) * len(shape))
    tok = lambda w: pl.BlockSpec((1, tm, w), lambda bi, i: (bi, i, 0))
    out_shapes = (
        [jax.ShapeDtypeStruct((b, s, ATT_W), BF16)] * 3
        + [jax.ShapeDtypeStruct((b, s, M_W), BF16)] * 4
        + [jax.ShapeDtypeStruct((b, s, N_GATES), F32), jax.ShapeDtypeStruct((b, N_GATES, s), F32)]
    )
    return pl.pallas_call(
        _inproj_kernel,
        out_shape=out_shapes,
        grid=(b, nt),
        in_specs=[
            tok(d),
            pl.BlockSpec((1, 1, SUBLANES, d), lambda bi, i: (bi, jnp.maximum(i * r8 - 1, 0), 0, 0)),
            pl.BlockSpec((1, 1, SUBLANES, d), lambda bi, i: (bi, jnp.minimum((i + 1) * r8, nb8 - 1), 0, 0)),
            full((1, d)),
            full(wqk.shape), full(wav.shape), full(wmqk.shape), full(wmvo.shape),
            full(wg.shape), full(wgt.shape), full(gb.shape), full(gbt.shape), full(conv_w.shape),
            pl.BlockSpec((tm, LANES), lambda bi, i: (i, 0)),
            pl.BlockSpec((tm, LANES), lambda bi, i: (i, 0)),
            pl.BlockSpec((tm, LANES), lambda bi, i: (i, 0)),
        ],
        out_specs=[tok(ATT_W)] * 3 + [tok(M_W)] * 4
        + [tok(N_GATES), pl.BlockSpec((1, N_GATES, tm), lambda bi, i: (bi, 0, i))],
        compiler_params=_cparams(("parallel", "parallel")),
        name="inproj",
    )(x, x8, x8, norm_w, wqk, wav, wmqk, wmvo, wg, wgt, gb, gbt, conv_w, rc, rs1, rs2)


ATT_KEY_BLOCK = 1024


def _attn_kernel(lq_ref, q_ref, k_ref, v_ref, gain_ref, o_ref):
    lq = lq_ref[...]
    lam = (jnp.exp(jnp.sum(lq[0:1] * lq[1:2], axis=1, keepdims=True))
           - jnp.exp(jnp.sum(lq[2:3] * lq[3:4], axis=1, keepdims=True)) + LAM_INIT)
    q = q_ref[0]
    tq = q.shape[0]
    lane = lax.broadcasted_iota(jnp.int32, q.shape, 1)
    zero = jnp.zeros_like(q)
    qq = jnp.concatenate([jnp.where(lane < ATT_QK_DIM, q, zero),
                          jnp.where(lane >= ATT_QK_DIM, q, zero)], axis=0)
    m = jnp.full((2 * tq, 1), NEG_INF, F32)
    ln = jnp.zeros((2 * tq, 1), F32)
    acc = jnp.zeros((2 * tq, ATT_V_DIM), F32)
    kb = min(ATT_KEY_BLOCK, k_ref.shape[1])
    assert k_ref.shape[1] % kb == 0
    for c in range(k_ref.shape[1] // kb):
        ks = slice(c * kb, (c + 1) * kb)
        sc = _dot_nt(qq, k_ref[0, ks, :])
        m_new = jnp.maximum(m, jnp.max(sc, axis=1, keepdims=True))
        alpha = jnp.exp2(m - m_new)
        e = jnp.exp2(sc - m_new)
        ln = alpha * ln + jnp.sum(e, axis=1, keepdims=True)
        acc = alpha * acc + _dot(e.astype(BF16), v_ref[0, ks, :])
        m = m_new
    o = acc[:tq] * (1.0 / ln[:tq]) - acc[tq:] * (lam / ln[tq:])
    o_ref[0] = (_rms(o, gain_ref[...]) * (1.0 - LAM_INIT)).astype(BF16)


def _attn(lq, q, k, v, gain, tq):
    b, s, _ = q.shape
    return pl.pallas_call(
        _attn_kernel,
        out_shape=jax.ShapeDtypeStruct((b, s, ATT_W), BF16),
        grid=(b, ATT_HEADS, s // tq),
        in_specs=[
            pl.BlockSpec(lq.shape, lambda bi, h, i: (0, 0)),
            pl.BlockSpec((1, tq, ATT_V_DIM), lambda bi, h, i: (bi, i, h)),
            pl.BlockSpec((1, s, ATT_V_DIM), lambda bi, h, i: (bi, 0, h)),
            pl.BlockSpec((1, s, ATT_V_DIM), lambda bi, h, i: (bi, 0, h)),
            pl.BlockSpec((1, ATT_V_DIM), lambda bi, h, i: (0, 0)),
        ],
        out_specs=pl.BlockSpec((1, tq, ATT_V_DIM), lambda bi, h, i: (bi, i, h)),
        compiler_params=_cparams(("parallel", "parallel", "parallel")),
        name="attn",
    )(lq, q, k, v, gain)


def _mlstm_chunk(q, k, v, ic_col, lf_col, ic_row, lf_row, cst, n, m, fwd):
    ln = q.shape[0]
    row = lax.broadcasted_iota(jnp.int32, (ln, ln), 0)
    col = lax.broadcasted_iota(jnp.int32, (ln, ln), 1)
    lower = col <= row
    upper = col >= row
    mask, mask_t = (lower, upper) if fwd else (upper, lower)
    bcum_col = jnp.sum(jnp.where(mask, lf_row, 0.0), axis=1, keepdims=True)
    bcum_row = jnp.sum(jnp.where(mask_t, lf_col, 0.0), axis=0, keepdims=True)
    dlog = jnp.where(mask, bcum_col - bcum_row + ic_row, NEG_INF)
    inter = bcum_col + m
    m_t = jnp.maximum(inter, jnp.max(dlog, axis=1, keepdims=True))
    w = jnp.exp(dlog - m_t)
    sc = jnp.exp(inter - m_t)
    wqk = w * _dot_nt(q, k)
    num = sc * _dot(q, cst.astype(BF16)) + _dot(wqk.astype(BF16), v)
    qn = jnp.sum(q.astype(F32) * n, axis=1, keepdims=True)
    den = sc * qn + jnp.sum(wqk, axis=1, keepdims=True)
    h = num / jnp.maximum(jnp.abs(den), jnp.exp(-m_t))
    b_last = jnp.sum(lf_row, axis=1, keepdims=True)
    rel_row = b_last - bcum_row + ic_row
    rel_col = b_last - bcum_col + ic_col
    m_new = jnp.maximum(b_last + m, jnp.max(rel_row, axis=1, keepdims=True))
    dec_col = jnp.exp(rel_col - m_new)
    keep = jnp.exp(b_last + m - m_new)
    kd = k.astype(F32) * dec_col
    c_new = keep * cst + lax.dot_general(kd.astype(BF16), v, (((0,), (0,)), ((), ())),
                                         preferred_element_type=F32)
    n_new = keep * n + jnp.sum(kd, axis=0, keepdims=True)
    return h, c_new, n_new, m_new


M_HEADS_PER_STEP = 1


def _mlstm_kernel(q_ref, k_ref, v_ref, o_ref, gc_ref, gr_ref, gain_ref, out_ref, hf_ref, hb_ref):
    s = q_ref.shape[1]
    nc = s // M_CHUNK
    d = M_HEAD_DIM
    heads = range(M_HEADS_PER_STEP)

    def step(j, c, carry, fwd):
        cst, n, m = carry
        sl = pl.ds(pl.multiple_of(c * M_CHUNK, M_CHUNK), M_CHUNK)
        ln = slice(j * d, (j + 1) * d)
        gc = gc_ref[0, j, c]
        gr = gr_ref[0, j, c]
        t = 0 if fwd else 2
        h, cst, n, m = _mlstm_chunk(
            q_ref[0, sl, ln], k_ref[0, sl, ln], v_ref[0, sl, ln],
            gc[:, t:t + 1], _log_sigmoid(gc[:, t + 1:t + 2]),
            gr[t:t + 1, :], _log_sigmoid(gr[t + 1:t + 2, :]),
            cst, n, m, fwd)
        return sl, ln, h, (cst, n, m)

    init = (jnp.zeros((d, d), F32), jnp.zeros((1, d), F32), jnp.zeros((1, 1), F32))

    def scan_body(c, carry):
        new = []
        for j in heads:
            cf, cb = carry[j]
            sl, ln, h, cf = step(j, c, cf, True)
            hf_ref[sl, ln] = h
            sl, ln, h, cb = step(j, nc - 1 - c, cb, False)
            hb_ref[sl, ln] = h
            new.append((cf, cb))
        return tuple(new)

    lax.fori_loop(0, nc, scan_body, tuple((init, init) for _ in heads))

    def gate_body(c, carry):
        sl = pl.ds(pl.multiple_of(c * M_CHUNK, M_CHUNK), M_CHUNK)
        for j in heads:
            ln = slice(j * d, (j + 1) * d)
            y = _sigmoid(o_ref[0, sl, ln].astype(F32)) * (hf_ref[sl, ln] + hb_ref[sl, ln])
            out_ref[0, sl, ln] = _rms(y, gain_ref[:, ln]).astype(BF16)
        return carry

    lax.fori_loop(0, nc, gate_body, 0)


def _mlstm(q, k, v, o, gc, gr, gain):
    b, s, _ = q.shape
    nc = s // M_CHUNK
    hp = M_HEADS_PER_STEP
    w = hp * M_HEAD_DIM
    head = pl.BlockSpec((1, s, w), lambda bi, h: (bi, 0, h))
    return pl.pallas_call(
        _mlstm_kernel,
        out_shape=jax.ShapeDtypeStruct((b, s, M_W), BF16),
        grid=(b, M_HEADS // hp),
        in_specs=[
            head, head, head, head,
            pl.BlockSpec((1, hp, nc, M_CHUNK, 4), lambda bi, h: (bi, h, 0, 0, 0)),
            pl.BlockSpec((1, hp, nc, 4, M_CHUNK), lambda bi, h: (bi, h, 0, 0, 0)),
            pl.BlockSpec((1, w), lambda bi, h: (0, h)),
        ],
        out_specs=head,
        scratch_shapes=[pltpu.VMEM((s, w), F32), pltpu.VMEM((s, w), F32)],
        compiler_params=_cparams(("parallel", "parallel")),
        name="mlstm",
    )(q, k, v, o, gc, gr, gain)


def _outproj_kernel(x_ref, att_ref, mem_ref, wa_ref, wm_ref, nw_ref, wq_ref, x1_ref, xn_ref, qp_ref):
    x1 = x_ref[...] + _dot(att_ref[...], wa_ref[...]) + _dot(mem_ref[...], wm_ref[...])
    x1_ref[...] = x1
    xn = _rms(x1, nw_ref[...])
    xn_ref[...] = xn
    qp = _dot(xn.astype(BF16), wq_ref[...])
    for c in range(qp_ref.shape[0]):
        qp_ref[c] = qp[:, c * LANES:(c + 1) * LANES].astype(BF16)


def _outproj(x, att, mem, wa, wm, nw, wq, tm):
    t, d = x.shape
    nq = wq.shape[1] // LANES
    full = lambda shape: pl.BlockSpec(shape, lambda i: (0,) * len(shape))
    tok = lambda w: pl.BlockSpec((tm, w), lambda i: (i, 0))
    return pl.pallas_call(
        _outproj_kernel,
        out_shape=[jax.ShapeDtypeStruct((t, d), F32), jax.ShapeDtypeStruct((t, d), F32),
                   jax.ShapeDtypeStruct((nq, t, LANES), BF16)],
        grid=(t // tm,),
        in_specs=[tok(d), tok(ATT_W), tok(M_W), full(wa.shape), full(wm.shape), full(nw.shape),
                  full(wq.shape)],
        out_specs=[tok(d), tok(d), pl.BlockSpec((nq, tm, LANES), lambda i: (0, i, 0))],
        compiler_params=_cparams(("parallel",)),
        name="outproj",
    )(x, att, mem, wa, wm, nw, wq)


ROUTE_HEADS_PER_ITER = 8


def _topk_rows(sc, n):
    pos = lax.broadcasted_iota(jnp.int32, sc.shape, 0).astype(F32)
    vals, poss = [], []
    for _ in range(PEER_TOPK):
        m = jnp.max(sc, axis=0, keepdims=True)
        p = jnp.min(jnp.where(sc == m, pos, float(n)), axis=0, keepdims=True)
        vals.append(m)
        poss.append(p)
        sc = jnp.where(pos == p, NEG_INF, sc)
    return jnp.concatenate(vals, axis=0), jnp.concatenate(poss, axis=0).astype(jnp.int32)


def _route_kernel(qp_ref, sk_ref, idx_ref, g_ref, idx_t, g_t):
    def head(h, carry):
        s1 = _dot_nt(sk_ref[2 * h], qp_ref[2 * h])
        s2 = _dot_nt(sk_ref[2 * h + 1], qp_ref[2 * h + 1])
        v1, i1 = _topk_rows(s1, N_KEYS)
        v2, i2 = _topk_rows(s2, N_KEYS)
        k = PEER_TOPK
        sub = SUBLANES
        brow = lax.broadcasted_iota(jnp.int32, (sub, v2.shape[1]), 0)
        slabs = [v1[0:1, :] + v2]
        for a in range(1, sub):
            slabs.append(jnp.where(brow < k // (a + 1), v1[a:a + 1, :] + v2[0:sub, :], NEG_INF))
        slabs.append(v1[sub:k, :] + v2[0:1, :])
        cand = jnp.concatenate(slabs, axis=0)
        sc, cp = _topk_rows(cand, cand.shape[0])
        mid = cp < k + sub * (sub - 1)
        slab = lax.shift_right_logical(cp, int(math.log2(sub)))
        ca = jnp.where(cp < k, 0, jnp.where(mid, slab - 1, cp - (k + sub * (sub - 1)) + sub))
        cb = jnp.where(cp < k, cp, jnp.where(mid, cp & (sub - 1), 0))
        e1 = jnp.zeros_like(cp)
        e2 = jnp.zeros_like(cp)
        for a in range(PEER_TOPK):
            e1 = jnp.where(ca == a, i1[a:a + 1, :], e1)
            e2 = jnp.where(cb == a, i2[a:a + 1, :], e2)
        ex = jnp.exp(sc - sc[0:1, :])
        rows = pl.ds(pl.multiple_of(h * PEER_TOPK, PEER_TOPK), PEER_TOPK)
        idx_t[rows, :] = e1 * N_KEYS + e2
        g_t[rows, :] = ex / jnp.sum(ex, axis=0, keepdims=True)
        return carry

    def heads(hh, carry):
        for j in range(ROUTE_HEADS_PER_ITER):
            head(hh * ROUTE_HEADS_PER_ITER + j, carry)
        return carry

    lax.fori_loop(0, PEER_HEADS // ROUTE_HEADS_PER_ITER, heads, 0)
    idx_ref[...] = idx_t[...].T
    g_ref[...] = g_t[...].T


def _route(qp, sk, tm):
    nq, t, _ = qp.shape
    return pl.pallas_call(
        _route_kernel,
        out_shape=[jax.ShapeDtypeStruct((t, PEER_SEL), jnp.int32),
                   jax.ShapeDtypeStruct((t, PEER_SEL), F32)],
        grid=(t // tm,),
        in_specs=[pl.BlockSpec((nq, tm, LANES), lambda i: (0, i, 0)),
                  pl.BlockSpec(sk.shape, lambda i: (0, 0, 0))],
        out_specs=[pl.BlockSpec((tm, PEER_SEL), lambda i: (i, 0))] * 2,
        scratch_shapes=[pltpu.VMEM((PEER_SEL, tm), jnp.int32), pltpu.VMEM((PEER_SEL, tm), F32)],
        compiler_params=_cparams(("parallel",)),
        name="route",
    )(qp, sk)


PEER_TB = 512
PEER_NB = SUBLANES
PEER_STAGED_TB = 32
PEER_NG = 4
PEER_LA = 2
HI16 = -65536


def _gelu(a):
    return 0.5 * a * (1.0 + lax.erf(a * (2.0 ** -0.5)))


def _peer_token(xr, gr, tile):
    nch = D_MODEL // LANES
    nkt = PEER_SEL // SUBLANES
    xb = [jnp.broadcast_to(xr[:, c * LANES:(c + 1) * LANES], (SUBLANES, LANES)) for c in range(nch)]
    parts = []
    for kt in range(nkt):
        acc = None
        for c in range(nch):
            u = lax.bitcast_convert_type(tile(kt, c) & HI16, F32)
            acc = u * xb[c] if acc is None else acc + u * xb[c]
        parts.append(acc)
    part = jnp.concatenate(parts, axis=0)
    a_row = jnp.sum(part.T, axis=0, keepdims=True)
    act = _gelu(a_row) * gr
    act_b = jnp.broadcast_to(act, (LANES, PEER_SEL)).T
    outs = []
    for c in range(nch):
        acc = None
        for kt in range(nkt):
            v = lax.bitcast_convert_type(tile(kt, c) << 16, F32)
            t = act_b[kt * SUBLANES:(kt + 1) * SUBLANES, :] * v
            acc = t if acc is None else acc + t
        outs.append(jnp.sum(acc, axis=0, keepdims=True))
    return jnp.concatenate(outs, axis=1)


def _peer_staged_kernel(st_ref, xn_ref, g_ref, rest_ref, after_ref, out_ref):
    del rest_ref, after_ref
    xg = xn_ref[...]
    gg = g_ref[...]
    rows = []
    for tb in range(xn_ref.shape[0]):
        def tile(kt, c, tb=tb):
            return st_ref[tb, kt * SUBLANES:(kt + 1) * SUBLANES, c * LANES:(c + 1) * LANES]
        rows.append(_peer_token(xg[tb:tb + 1, :], gg[tb:tb + 1, :], tile))
    out_ref[...] = jnp.concatenate(rows, axis=0)


def _peer_staged(staged, xn, g, rest, after=None):
    ts = staged.shape[0]
    t, d = xn.shape
    tb = PEER_STAGED_TB
    return pl.pallas_call(
        _peer_staged_kernel,
        out_shape=jax.ShapeDtypeStruct((t, d), F32),
        grid=(ts // tb,),
        in_specs=[pl.BlockSpec((tb, PEER_SEL, d), lambda i: (i, 0, 0)),
                  pl.BlockSpec((tb, d), lambda i: (i, 0)),
                  pl.BlockSpec((tb, PEER_SEL), lambda i: (i, 0)),
                  pl.BlockSpec(memory_space=pl.ANY),
                  pl.BlockSpec(memory_space=pl.ANY)],
        out_specs=pl.BlockSpec((tb, d), lambda i: (i, 0)),
        input_output_aliases={3: 0},
        compiler_params=_cparams(("parallel",)),
        name="peer_staged",
    )(staged, xn, g, rest, _order_after(after))


SC_CORES = 2
SC_SUBCORES = 16
SC_WORKERS = SC_CORES * SC_SUBCORES
SC_ROWS = 32
SC_IDX_BLK = 128
PEER_SC_SHARE_NUM = 11
PEER_SC_SHARE_DEN = 16
PEER_SC_QUANTUM = SC_WORKERS * SC_ROWS * SC_IDX_BLK // PEER_SEL


def _stage_rows(uv2, idx):
    n_tok, nsel = idx.shape
    n_rows = n_tok * nsel
    rows_per_w = n_rows // SC_WORKERS
    n_blk = rows_per_w // (SC_ROWS * SC_IDX_BLK)
    idx4 = idx.reshape(SC_WORKERS, n_blk, SC_IDX_BLK, SC_ROWS)
    mesh = plsc.VectorSubcoreMesh(core_axis_name="c", subcore_axis_name="s")

    @functools.partial(
        pl.kernel, mesh=mesh,
        out_type=jax.ShapeDtypeStruct((n_rows, uv2.shape[1]), uv2.dtype),
        scratch_types=[
            pltpu.VMEM((SC_IDX_BLK, SC_ROWS), jnp.int32),
            pltpu.VMEM((2, SC_ROWS, uv2.shape[1]), uv2.dtype),
            pltpu.SemaphoreType.DMA((2,)),
            pltpu.SemaphoreType.DMA((2,)),
        ],
    )
    def stage(uv_hbm, idx_hbm, out_hbm, idx_v, rows_v, gsem, wsem):
        wid = lax.axis_index("s") * SC_CORES + lax.axis_index("c")
        base = wid * rows_per_w

        def gather(j, b):
            return pltpu.make_async_copy(uv_hbm.at[idx_v.at[j]], rows_v.at[b], gsem.at[b])

        def write(row0, b):
            return pltpu.make_async_copy(rows_v.at[b], out_hbm.at[pl.ds(row0, SC_ROWS)], wsem.at[b])

        @pl.loop(0, n_blk)
        def _(blk):
            pltpu.sync_copy(idx_hbm.at[wid, blk], idx_v)
            blk_row0 = base + blk * (SC_IDX_BLK * SC_ROWS)
            gather(0, 0).start()

            @pl.loop(0, SC_IDX_BLK, step=2)
            def _(j):
                for b in range(2):
                    jj = j + b
                    row0 = pl.multiple_of(blk_row0 + jj * SC_ROWS, SC_ROWS)
                    gather(jj, b).wait()

                    @pl.when(jj > 0)
                    def _():
                        write(row0 - SC_ROWS, 1 - b).wait()

                    @pl.when(jj + 1 < SC_IDX_BLK)
                    def _():
                        gather(jj + 1, 1 - b).start()

                    write(row0, b).start()

            write(blk_row0 + (SC_IDX_BLK - 1) * SC_ROWS, 1).wait()

    return stage(uv2, idx4)


def _peer_kernel(step0, idx_hbm, xn_ref, g_ref, uv_hbm, after_ref, out_ref, idx_s, buf, sem_i, sem):
    del after_ref
    step = pl.program_id(0) + step0
    nbatch = xn_ref.shape[0] // PEER_NB
    nch = D_MODEL // LANES
    nkt = PEER_SEL // SUBLANES

    cp = pltpu.make_async_copy(idx_hbm.at[step], idx_s, sem_i.at[0])
    cp.start()
    cp.wait()

    def issue(b):
        grp = b & (PEER_NG - 1)

        def tok_body(tb, carry):
            base = (b * PEER_NB + tb) * PEER_SEL
            for k in range(PEER_SEL):
                e = idx_s[base + k]
                pltpu.make_async_copy(
                    uv_hbm.at[e], buf.at[grp, tb, k // SUBLANES, :, k % SUBLANES, :], sem.at[grp]
                ).start(priority=k % 2)
            return carry

        lax.fori_loop(0, PEER_NB, tok_body, 0)

    def wait(b):
        grp = b & (PEER_NG - 1)
        pltpu.make_async_copy(buf.at[grp], buf.at[grp], sem.at[grp]).wait()

    def compute(b):
        grp = b & (PEER_NG - 1)
        r0 = pl.multiple_of(b * PEER_NB, PEER_NB)
        xg = xn_ref[pl.ds(r0, PEER_NB), :]
        gg = g_ref[pl.ds(r0, PEER_NB), :]
        rows = [_peer_token(xg[tb:tb + 1, :], gg[tb:tb + 1, :],
                            functools.partial(lambda tb, kt, c: buf[grp, tb, kt, c], tb))
                for tb in range(PEER_NB)]
        out_ref[pl.ds(r0, PEER_NB), :] = jnp.concatenate(rows, axis=0)

    for b in range(PEER_LA):
        issue(b)

    def body(b, carry):
        wait(b)

        @pl.when(b + PEER_LA < nbatch)
        def _():
            issue(b + PEER_LA)

        compute(b)
        return carry

    lax.fori_loop(0, nbatch, body, 0)


def _order_after(after):
    return jnp.zeros((SUBLANES, LANES), F32) if after is None else after


def _peer(idx, xn, g, uv, ts, after=None):
    t, d = xn.shape
    tb = PEER_TB
    idx2 = idx.reshape(t // tb, tb * PEER_SEL)
    s0 = ts // tb
    return pl.pallas_call(
        functools.partial(_peer_kernel, s0),
        out_shape=jax.ShapeDtypeStruct((t, d), F32),
        grid=((t - ts) // tb,),
        in_specs=[pl.BlockSpec(memory_space=pl.ANY),
                  pl.BlockSpec((tb, d), lambda i: (i + s0, 0)),
                  pl.BlockSpec((tb, PEER_SEL), lambda i: (i + s0, 0)),
                  pl.BlockSpec(memory_space=pl.ANY),
                  pl.BlockSpec(memory_space=pl.ANY)],
        out_specs=pl.BlockSpec((tb, d), lambda i: (i + s0, 0)),
        scratch_shapes=[pltpu.SMEM((tb * PEER_SEL,), jnp.int32),
                        pltpu.VMEM((PEER_NG, PEER_NB, PEER_SEL // SUBLANES, d // LANES, SUBLANES, LANES),
                                   jnp.int32),
                        pltpu.SemaphoreType.DMA((1,)),
                        pltpu.SemaphoreType.DMA((PEER_NG,))],
        compiler_params=_cparams(("arbitrary",)),
        name="peer",
    )(idx2, xn, g, uv, _order_after(after))


def _ple_kernel(x1_ref, pe_ref, p_ref, nw_ref, wg_ref, wp_ref, nf_ref, y_ref):
    x2 = x1_ref[...] + pe_ref[...]
    gate = _sigmoid(_dot(_rms(x2, nw_ref[...]).astype(BF16), wg_ref[...]))
    x3 = x2 + _dot(p_ref[...].astype(BF16), wp_ref[...]) * gate
    y_ref[...] = _rms(x3, nf_ref[...])


def _ple(x1, pe, p, nw, wg, wp, nf, tm):
    t, d = x1.shape
    full = lambda shape: pl.BlockSpec(shape, lambda i: (0,) * len(shape))
    tok = lambda w: pl.BlockSpec((tm, w), lambda i: (i, 0))
    return pl.pallas_call(
        _ple_kernel,
        out_shape=jax.ShapeDtypeStruct((t, d), F32),
        grid=(t // tm,),
        in_specs=[tok(d), tok(d), tok(PLE_DIM), full(nw.shape), full(wg.shape), full(wp.shape),
                  full(nf.shape)],
        out_specs=tok(d),
        compiler_params=_cparams(("parallel",)),
        name="ple",
    )(x1, pe, p, nw, wg, wp, nf)


def _rope_tables(s):
    half = ROT_DIM // 2
    inv = ROPE_THETA ** (-jnp.arange(0, ROT_DIM, 2, dtype=F32) / ROT_DIM)
    ang = jnp.arange(s, dtype=F32)[:, None] * inv[None, :]
    cos, sin = jnp.cos(ang), jnp.sin(ang)
    ones = jnp.ones((s, ATT_QK_DIM - ROT_DIM), F32)
    zeros = jnp.zeros((s, ATT_QK_DIM - ROT_DIM), F32)
    zh = jnp.zeros((s, half), F32)
    c64 = jnp.concatenate([cos, cos, ones], axis=1)
    s1_64 = jnp.concatenate([zh, sin, zeros], axis=1)
    s2_64 = jnp.concatenate([-sin, zh, zeros], axis=1)
    rep = LANES // ATT_QK_DIM
    return jnp.tile(c64, (1, rep)), jnp.tile(s1_64, (1, rep)), jnp.tile(s2_64, (1, rep))


def _pack_uv(u, v):
    ub = lax.bitcast_convert_type(u.astype(BF16), jnp.uint16).astype(jnp.uint32)
    vb = lax.bitcast_convert_type(v.astype(BF16), jnp.uint16).astype(jnp.uint32)
    return lax.bitcast_convert_type((ub << 16) | vb, jnp.int32)


def _mix_route(x, w):
    b, s, d = x.shape
    t = b * s
    nc = s // M_CHUNK
    rc, rs1, rs2 = _rope_tables(s)
    aq, ak, av, mq, mk, mv, mo, gc, gr = _inproj(
        x, w["norm_mix"], w["wqk"], w["wav"], w["wmqk"], w["wmvo"], w["wg"], w["wgt"], w["gb"],
        w["gbt"], w["conv"], rc, rs1, rs2, tm=512)
    att = _attn(w["lq"], aq, ak, av, w["attn_gain"], tq=256)
    gc5 = gc.reshape(b, nc, M_CHUNK, M_HEADS, 4).transpose(0, 3, 1, 2, 4)
    gr5 = gr.reshape(b, M_HEADS, 4, nc, M_CHUNK).transpose(0, 1, 3, 2, 4)
    mem = _mlstm(mq, mk, mv, mo, gc5, gr5, w["mlstm_gain"])
    x1, xn, qp = _outproj(x.reshape(t, d), att.reshape(t, ATT_W), mem.reshape(t, M_W),
                          w["wo_a"], w["wo_m"], w["norm_ffn"], w["wq"], tm=512)
    idx, g = _route(qp, w["sk"], tm=128)
    return x1, xn, idx, g


def _peer_all(groups, uv2):
    uv3 = uv2.reshape(-1, D_MODEL // LANES, LANES)
    staged, outs = [], []
    prev = None
    for gi, (xn, idx, g) in enumerate(groups):
        t, d = xn.shape
        num = PEER_SC_SHARE_NUM - (gi == len(groups) - 1)
        ts = (t * num // PEER_SC_SHARE_DEN) // PEER_SC_QUANTUM * PEER_SC_QUANTUM
        assert 0 < ts < t and (t - ts) % PEER_TB == 0, (t, ts)
        staged.append(_stage_rows(uv2, idx[:ts]).reshape(ts, PEER_SEL, d))
        prev = _peer(idx, xn, g, uv3, ts, after=prev)
        outs.append(prev)
    for i, (xn, idx, g) in enumerate(groups):
        prev = _peer_staged(staged[i], xn, g, outs[i], after=prev)
        outs[i] = prev
    return outs


def kernel(x_prompt, x_sample, p_prompt, p_sample, norm_mix, w_in, gate_bias, conv_qk, lambda_qk,
           attn_gain, mlstm_gain, w_out, norm_ffn, peer_wq, peer_subkeys, peer_u, peer_v, norm_ple,
           ple_w_gate, ple_w_proj, norm_final):
    li = 0
    win = w_in[li]
    c0 = 0
    cols = {}
    for name, width in (("aq", ATT_W), ("ak", ATT_W), ("av", ATT_W), ("mq", M_W), ("mk", M_W),
                        ("mv", M_W), ("mo", M_W), ("gt", N_GATES)):
        cols[name] = win[:, c0:c0 + width]
        c0 += width
    perm = jnp.array([ty * M_HEADS + h for h in range(M_HEADS) for ty in range(4)], jnp.int32)
    wg = cols["gt"][:, perm].astype(BF16)
    gb = gate_bias[li][perm]
    w = {
        "norm_mix": norm_mix[li][None, :],
        "wqk": jnp.concatenate([cols["aq"], cols["ak"]], axis=1).astype(BF16),
        "wav": cols["av"].astype(BF16),
        "wmqk": jnp.concatenate([cols["mq"], cols["mk"]], axis=1).astype(BF16),
        "wmvo": jnp.concatenate([cols["mv"], cols["mo"]], axis=1).astype(BF16),
        "wg": wg,
        "wgt": wg.T,
        "gb": gb[None, :],
        "gbt": gb[:, None],
        "conv": conv_qk[li],
        "lq": lambda_qk[li],
        "attn_gain": attn_gain[li][None, :],
        "mlstm_gain": mlstm_gain[li][None, :],
        "wo_a": w_out[li][:ATT_W].astype(BF16),
        "wo_m": w_out[li][ATT_W:].astype(BF16),
        "norm_ffn": norm_ffn[li][None, :],
        "wq": peer_wq[li].astype(BF16),
        "sk": peer_subkeys[li].reshape(PEER_HEADS * 2, N_KEYS, PEER_HALF).astype(BF16),
        "uv": _pack_uv(peer_u[li], peer_v[li]),
        "norm_ple": norm_ple[li][None, :],
        "ple_wg": ple_w_gate[li].astype(BF16),
        "ple_wp": ple_w_proj[li].astype(BF16),
        "norm_final": norm_final[None, :],
    }
    hb = x_prompt.shape[0] // 2
    xs = (x_prompt[:hb], x_prompt[hb:], x_sample)
    ps = (p_prompt[li][:hb], p_prompt[li][hb:], p_sample[li])
    mixed = [_mix_route(x, w) for x in xs]
    pes = _peer_all([(xn, idx, g) for _, xn, idx, g in mixed], w["uv"])
    ys = []
    for x, p, (x1, _, _, _), pe in zip(xs, ps, mixed, pes):
        y = _ple(x1, pe, p.reshape(-1, PLE_DIM), w["norm_ple"], w["ple_wg"], w["ple_wp"],
                 w["norm_final"], tm=512)
        ys.append(y.reshape(x.shape))
    return (jnp.concatenate(ys[:2], axis=0), ys[2])
```

```python
import functools
import math

import jax
import jax.numpy as jnp
from jax import lax
from jax.experimental import pallas as pl
from jax.experimental.pallas import tpu as pltpu
from jax.experimental.pallas import tpu_sc as plsc

F32 = jnp.float32
BF16 = jnp.bfloat16

D_MODEL = 1024
ATT_W = 512
ATT_HEADS = 4
ATT_V_DIM = 128
ATT_QK_DIM = 64
ROT_DIM = 16
ROPE_THETA = 500000.0
M_W = 512
M_HEADS = 4
M_HEAD_DIM = 128
M_CHUNK = 128
CONV_K = 5
N_GATES = 16
PEER_HEADS = 8
PEER_QDIM = 256
PEER_HALF = 128
N_KEYS = 128
PEER_TOPK = 16
PEER_SEL = PEER_HEADS * PEER_TOPK
PLE_DIM = 256
EPS = 1e-6
LAM_INIT = 0.8 - 0.6 * math.exp(-0.3 * 0)
Q_SCALE = ATT_QK_DIM ** -0.5 * math.log2(math.e)

LANES = 128
SUBLANES = 8
VMEM_LIMIT = 56 * 1024 * 1024

NEG_INF = float("-inf")


def _cparams(sem):
    return pltpu.CompilerParams(dimension_semantics=sem, vmem_limit_bytes=VMEM_LIMIT)


def _dot(a, b):
    return jnp.dot(a, b, preferred_element_type=F32)


def _dot_nt(a, b):
    return lax.dot_general(a, b, (((1,), (1,)), ((), ())), preferred_element_type=F32)


def _rms(x, w):
    ms = jnp.mean(x * x, axis=-1, keepdims=True)
    return (x * lax.rsqrt(ms + EPS)) * w


def _sigmoid(x):
    return 1.0 / (1.0 + jnp.exp(-x))


def _log_sigmoid(x):
    return jnp.minimum(x, 0.0) - jnp.log(1.0 + jnp.exp(-jnp.abs(x)))


def _inproj_kernel(x_ref, xp_ref, xn_ref, nw_ref, wqk_ref, wav_ref, wmqk_ref, wmvo_ref,
                   wg_ref, wgt_ref, gb_ref, gbt_ref, cw_ref, rc_ref, rs1_ref, rs2_ref,
                   aq_ref, ak_ref, av_ref, mq_ref, mk_ref, mv_ref, mo_ref, gc_ref, gr_ref):
    i = pl.program_id(1)
    last = pl.num_programs(1) - 1
    nw = nw_ref[...]
    tm = x_ref.shape[1]

    h = _rms(x_ref[0], nw).astype(BF16)
    hp = _rms(xp_ref[0, 0], nw).astype(BF16)
    hn = _rms(xn_ref[0, 0], nw).astype(BF16)

    qk = _dot(h, wqk_ref[...])
    rc = rc_ref[...]
    rs1 = rs1_ref[...]
    rs2 = rs2_ref[...]
    half = ROT_DIM // 2
    for c in range(2 * ATT_W // LANES):
        xc = qk[:, c * LANES:(c + 1) * LANES]
        r = xc * rc + pltpu.roll(xc, half, 1) * rs1 + pltpu.roll(xc, LANES - half, 1) * rs2
        if c < ATT_W // LANES:
            aq_ref[0, :, c * LANES:(c + 1) * LANES] = (r * Q_SCALE).astype(BF16)
        else:
            cc = c - ATT_W // LANES
            ak_ref[0, :, cc * LANES:(cc + 1) * LANES] = r.astype(BF16)

    av_ref[0] = _dot(h, wav_ref[...]).astype(BF16)

    wm = wmqk_ref[...]
    pm = _dot(h, wm)
    pp = _dot(hp, wm) * (i > 0).astype(F32)
    pn = _dot(hn, wm) * (i < last).astype(F32)
    ext = jnp.concatenate([pp, pm, pn], axis=0)
    cw = cw_ref[...]
    off = SUBLANES - CONV_K // 2
    conv = cw[0:1, :] * ext[off:off + tm, :]
    for j in range(1, CONV_K):
        conv = conv + cw[j:j + 1, :] * ext[off + j:off + j + tm, :]
    act = conv * _sigmoid(conv)
    mq_ref[0] = act[:, :M_W].astype(BF16)
    mk_ref[0] = (act[:, M_W:] * (M_HEAD_DIM ** -0.5)).astype(BF16)

    vo = _dot(h, wmvo_ref[...])
    mv_ref[0] = vo[:, :M_W].astype(BF16)
    mo_ref[0] = vo[:, M_W:].astype(BF16)

    gc_ref[0] = _dot(h, wg_ref[...]) + gb_ref[...]
    gr_ref[0] = _dot_nt(wgt_ref[...], h) + gbt_ref[...]


def _inproj(x, norm_w, wqk, wav, wmqk, wmvo, wg, wgt, gb, gbt, conv_w, rc, rs1, rs2, tm):
    b, s, d = x.shape
    nt = s // tm
    x8 = x.reshape(b, s // SUBLANES, SUBLANES, d)
    r8 = tm // SUBLANES
    nb8 = s // SUBLANES
    full = lambda shape: pl.BlockSpec(shape, lambda bi, i: (0,) * len(shape))
    tok = lambda w: pl.BlockSpec((1, tm, w), lambda bi, i: (bi, i, 0))
    out_shapes = (
        [jax.ShapeDtypeStruct((b, s, ATT_W), BF16)] * 3
        + [jax.ShapeDtypeStruct((b, s, M_W), BF16)] * 4
        + [jax.ShapeDtypeStruct((b, s, N_GATES), F32), jax.ShapeDtypeStruct((b, N_GATES, s), F32)]
    )
    return pl.pallas_call(
        _inproj_kernel,
        out_shape=out_shapes,
        grid=(b, nt),
        in_specs=[
            tok(d),
            pl.BlockSpec((1, 1, SUBLANES, d), lambda bi, i: (bi, jnp.maximum(i * r8 - 1, 0), 0, 0)),
            pl.BlockSpec((1, 1, SUBLANES, d), lambda bi, i: (bi, jnp.minimum((i + 1) * r8, nb8 - 1), 0, 0)),
            full((1, d)),
            full(wqk.shape), full(wav.shape), full(wmqk.shape), full(wmvo.shape),
            full(wg.shape), full(wgt.shape), full(gb.shape), full(gbt.shape), full(conv_w.shape),
            pl.BlockSpec((tm, LANES), lambda bi, i: (i, 0)),
            pl.BlockSpec((tm, LANES), lambda bi, i: (i, 0)),
            pl.BlockSpec((tm, LANES), lambda bi, i: (i, 0)),
        ],
        out_specs=[tok(ATT_W)] * 3 + [tok(M_W)] * 4
        + [tok(N_GATES), pl.BlockSpec((1, N_GATES, tm), lambda bi, i: (bi, 0, i))],
        compiler_params=_cparams(("parallel", "parallel")),
        name="inproj",
    )(x, x8, x8, norm_w, wqk, wav, wmqk, wmvo, wg, wgt, gb, gbt, conv_w, rc, rs1, rs2)


ATT_KEY_BLOCK = 1024


def _attn_kernel(lq_ref, q_ref, k_ref, v_ref, gain_ref, o_ref):
    lq = lq_ref[...]
    lam = (jnp.exp(jnp.sum(lq[0:1] * lq[1:2], axis=1, keepdims=True))
           - jnp.exp(jnp.sum(lq[2:3] * lq[3:4], axis=1, keepdims=True)) + LAM_INIT)
    q = q_ref[0]
    tq = q.shape[0]
    lane = lax.broadcasted_iota(jnp.int32, q.shape, 1)
    zero = jnp.zeros_like(q)
    qq = jnp.concatenate([jnp.where(lane < ATT_QK_DIM, q, zero),
                          jnp.where(lane >= ATT_QK_DIM, q, zero)], axis=0)
    m = jnp.full((2 * tq, 1), NEG_INF, F32)
    ln = jnp.zeros((2 * tq, 1), F32)
    acc = jnp.zeros((2 * tq, ATT_V_DIM), F32)
    kb = min(ATT_KEY_BLOCK, k_ref.shape[1])
    assert k_ref.shape[1] % kb == 0
    for c in range(k_ref.shape[1] // kb):
        ks = slice(c * kb, (c + 1) * kb)
        sc = _dot_nt(qq, k_ref[0, ks, :])
        m_new = jnp.maximum(m, jnp.max(sc, axis=1, keepdims=True))
        alpha = jnp.exp2(m - m_new)
        e = jnp.exp2(sc - m_new)
        ln = alpha * ln + jnp.sum(e, axis=1, keepdims=True)
        acc = alpha * acc + _dot(e.astype(BF16), v_ref[0, ks, :])
        m = m_new
    o = acc[:tq] * (1.0 / ln[:tq]) - acc[tq:] * (lam / ln[tq:])
    o_ref[0] = (_rms(o, gain_ref[...]) * (1.0 - LAM_INIT)).astype(BF16)


def _attn(lq, q, k, v, gain, tq):
    b, s, _ = q.shape
    return pl.pallas_call(
        _attn_kernel,
        out_shape=jax.ShapeDtypeStruct((b, s, ATT_W), BF16),
        grid=(b, ATT_HEADS, s // tq),
        in_specs=[
            pl.BlockSpec(lq.shape, lambda bi, h, i: (0, 0)),
            pl.BlockSpec((1, tq, ATT_V_DIM), lambda bi, h, i: (bi, i, h)),
            pl.BlockSpec((1, s, ATT_V_DIM), lambda bi, h, i: (bi, 0, h)),
            pl.BlockSpec((1, s, ATT_V_DIM), lambda bi, h, i: (bi, 0, h)),
            pl.BlockSpec((1, ATT_V_DIM), lambda bi, h, i: (0, 0)),
        ],
        out_specs=pl.BlockSpec((1, tq, ATT_V_DIM), lambda bi, h, i: (bi, i, h)),
        compiler_params=_cparams(("parallel", "parallel", "parallel")),
        name="attn",
    )(lq, q, k, v, gain)


def _mlstm_chunk(q, k, v, ic_col, lf_col, ic_row, lf_row, cst, n, m, fwd):
    ln = q.shape[0]
    row = lax.broadcasted_iota(jnp.int32, (ln, ln), 0)
    col = lax.broadcasted_iota(jnp.int32, (ln, ln), 1)
    lower = col <= row
    upper = col >= row
    mask, mask_t = (lower, upper) if fwd else (upper, lower)
    bcum_col = jnp.sum(jnp.where(mask, lf_row, 0.0), axis=1, keepdims=True)
    bcum_row = jnp.sum(jnp.where(mask_t, lf_col, 0.0), axis=0, keepdims=True)
    dlog = jnp.where(mask, bcum_col - bcum_row + ic_row, NEG_INF)
    inter = bcum_col + m
    m_t = jnp.maximum(inter, jnp.max(dlog, axis=1, keepdims=True))
    w = jnp.exp(dlog - m_t)
    sc = jnp.exp(inter - m_t)
    wqk = w * _dot_nt(q, k)
    num = sc * _dot(q, cst.astype(BF16)) + _dot(wqk.astype(BF16), v)
    qn = jnp.sum(q.astype(F32) * n, axis=1, keepdims=True)
    den = sc * qn + jnp.sum(wqk, axis=1, keepdims=True)
    h = num / jnp.maximum(jnp.abs(den), jnp.exp(-m_t))
    b_last = jnp.sum(lf_row, axis=1, keepdims=True)
    rel_row = b_last - bcum_row + ic_row
    rel_col = b_last - bcum_col + ic_col
    m_new = jnp.maximum(b_last + m, jnp.max(rel_row, axis=1, keepdims=True))
    dec_col = jnp.exp(rel_col - m_new)
    keep = jnp.exp(b_last + m - m_new)
    kd = k.astype(F32) * dec_col
    c_new = keep * cst + lax.dot_general(kd.astype(BF16), v, (((0,), (0,)), ((), ())),
                                         preferred_element_type=F32)
    n_new = keep * n + jnp.sum(kd, axis=0, keepdims=True)
    return h, c_new, n_new, m_new


M_HEADS_PER_STEP = 1


def _mlstm_kernel(q_ref, k_ref, v_ref, o_ref, gc_ref, gr_ref, gain_ref, out_ref, hf_ref, hb_ref):
    s = q_ref.shape[1]
    nc = s // M_CHUNK
    d = M_HEAD_DIM
    heads = range(M_HEADS_PER_STEP)

    def step(j, c, carry, fwd):
        cst, n, m = carry
        sl = pl.ds(pl.multiple_of(c * M_CHUNK, M_CHUNK), M_CHUNK)
        ln = slice(j * d, (j + 1) * d)
        gc = gc_ref[0, j, c]
        gr = gr_ref[0, j, c]
        t = 0 if fwd else 2
        h, cst, n, m = _mlstm_chunk(
            q_ref[0, sl, ln], k_ref[0, sl, ln], v_ref[0, sl, ln],
            gc[:, t:t + 1], _log_sigmoid(gc[:, t + 1:t + 2]),
            gr[t:t + 1, :], _log_sigmoid(gr[t + 1:t + 2, :]),
            cst, n, m, fwd)
        return sl, ln, h, (cst, n, m)

    init = (jnp.zeros((d, d), F32), jnp.zeros((1, d), F32), jnp.zeros((1, 1), F32))

    def scan_body(c, carry):
        new = []
        for j in heads:
            cf, cb = carry[j]
            sl, ln, h, cf = step(j, c, cf, True)
            hf_ref[sl, ln] = h
            sl, ln, h, cb = step(j, nc - 1 - c, cb, False)
            hb_ref[sl, ln] = h
            new.append((cf, cb))
        return tuple(new)

    lax.fori_loop(0, nc, scan_body, tuple((init, init) for _ in heads))

    def gate_body(c, carry):
        sl = pl.ds(pl.multiple_of(c * M_CHUNK, M_CHUNK), M_CHUNK)
        for j in heads:
            ln = slice(j * d, (j + 1) * d)
            y = _sigmoid(o_ref[0, sl, ln].astype(F32)) * (hf_ref[sl, ln] + hb_ref[sl, ln])
            out_ref[0, sl, ln] = _rms(y, gain_ref[:, ln]).astype(BF16)
        return carry

    lax.fori_loop(0, nc, gate_body, 0)


def _mlstm(q, k, v, o, gc, gr, gain):
    b, s, _ = q.shape
    nc = s // M_CHUNK
    hp = M_HEADS_PER_STEP
    w = hp * M_HEAD_DIM
    head = pl.BlockSpec((1, s, w), lambda bi, h: (bi, 0, h))
    return pl.pallas_call(
        _mlstm_kernel,
        out_shape=jax.ShapeDtypeStruct((b, s, M_W), BF16),
        grid=(b, M_HEADS // hp),
        in_specs=[
            head, head, head, head,
            pl.BlockSpec((1, hp, nc, M_CHUNK, 4), lambda bi, h: (bi, h, 0, 0, 0)),
            pl.BlockSpec((1, hp, nc, 4, M_CHUNK), lambda bi, h: (bi, h, 0, 0, 0)),
            pl.BlockSpec((1, w), lambda bi, h: (0, h)),
        ],
        out_specs=head,
        scratch_shapes=[pltpu.VMEM((s, w), F32), pltpu.VMEM((s, w), F32)],
        compiler_params=_cparams(("parallel", "parallel")),
        name="mlstm",
    )(q, k, v, o, gc, gr, gain)


def _outproj_kernel(x_ref, att_ref, mem_ref, wa_ref, wm_ref, nw_ref, wq_ref, x1_ref, xn_ref, qp_ref):
    x1 = x_ref[...] + _dot(att_ref[...], wa_ref[...]) + _dot(mem_ref[...], wm_ref[...])
    x1_ref[...] = x1
    xn = _rms(x1, nw_ref[...])
    xn_ref[...] = xn
    qp = _dot(xn.astype(BF16), wq_ref[...])
    for c in range(qp_ref.shape[0]):
        qp_ref[c] = qp[:, c * LANES:(c + 1) * LANES].astype(BF16)


def _outproj(x, att, mem, wa, wm, nw, wq, tm):
    t, d = x.shape
    nq = wq.shape[1] // LANES
    full = lambda shape: pl.BlockSpec(shape, lambda i: (0,) * len(shape))
    tok = lambda w: pl.BlockSpec((tm, w), lambda i: (i, 0))
    return pl.pallas_call(
        _outproj_kernel,
        out_shape=[jax.ShapeDtypeStruct((t, d), F32), jax.ShapeDtypeStruct((t, d), F32),
                   jax.ShapeDtypeStruct((nq, t, LANES), BF16)],
        grid=(t // tm,),
        in_specs=[tok(d), tok(ATT_W), tok(M_W), full(wa.shape), full(wm.shape), full(nw.shape),
                  full(wq.shape)],
        out_specs=[tok(d), tok(d), pl.BlockSpec((nq, tm, LANES), lambda i: (0, i, 0))],
        compiler_params=_cparams(("parallel",)),
        name="outproj",
    )(x, att, mem, wa, wm, nw, wq)


ROUTE_HEADS_PER_ITER = 8


def _topk_rows(sc, n):
    pos = lax.broadcasted_iota(jnp.int32, sc.shape, 0).astype(F32)
    vals, poss = [], []
    for _ in range(PEER_TOPK):
        m = jnp.max(sc, axis=0, keepdims=True)
        p = jnp.min(jnp.where(sc == m, pos, float(n)), axis=0, keepdims=True)
        vals.append(m)
        poss.append(p)
        sc = jnp.where(pos == p, NEG_INF, sc)
    return jnp.concatenate(vals, axis=0), jnp.concatenate(poss, axis=0).astype(jnp.int32)


def _route_kernel(qp_ref, sk_ref, idx_ref, g_ref, idx_t, g_t):
    def head(h, carry):
        s1 = _dot_nt(sk_ref[2 * h], qp_ref[2 * h])
        s2 = _dot_nt(sk_ref[2 * h + 1], qp_ref[2 * h + 1])
        v1, i1 = _topk_rows(s1, N_KEYS)
        v2, i2 = _topk_rows(s2, N_KEYS)
        k = PEER_TOPK
        sub = SUBLANES
        brow = lax.broadcasted_iota(jnp.int32, (sub, v2.shape[1]), 0)
        slabs = [v1[0:1, :] + v2]
        for a in range(1, sub):
            slabs.append(jnp.where(brow < k // (a + 1), v1[a:a + 1, :] + v2[0:sub, :], NEG_INF))
        slabs.append(v1[sub:k, :] + v2[0:1, :])
        cand = jnp.concatenate(slabs, axis=0)
        sc, cp = _topk_rows(cand, cand.shape[0])
        mid = cp < k + sub * (sub - 1)
        slab = lax.shift_right_logical(cp, int(math.log2(sub)))
        ca = jnp.where(cp < k, 0, jnp.where(mid, slab - 1, cp - (k + sub * (sub - 1)) + sub))
        cb = jnp.where(cp < k, cp, jnp.where(mid, cp & (sub - 1), 0))
        e1 = jnp.zeros_like(cp)
        e2 = jnp.zeros_like(cp)
        for a in range(PEER_TOPK):
            e1 = jnp.where(ca == a, i1[a:a + 1, :], e1)
            e2 = jnp.where(cb == a, i2[a:a + 1, :], e2)
        ex = jnp.exp(sc - sc[0:1, :])
        rows = pl.ds(pl.multiple_of(h * PEER_TOPK, PEER_TOPK), PEER_TOPK)
        idx_t[rows, :] = e1 * N_KEYS + e2
        g_t[rows, :] = ex / jnp.sum(ex, axis=0, keepdims=True)
        return carry

    def heads(hh, carry):
        for j in range(ROUTE_HEADS_PER_ITER):
            head(hh * ROUTE_HEADS_PER_ITER + j, carry)
        return carry

    lax.fori_loop(0, PEER_HEADS // ROUTE_HEADS_PER_ITER, heads, 0)
    idx_ref[...] = idx_t[...].T
    g_ref[...] = g_t[...].T


def _route(qp, sk, tm):
    nq, t, _ = qp.shape
    return pl.pallas_call(
        _route_kernel,
        out_shape=[jax.ShapeDtypeStruct((t, PEER_SEL), jnp.int32),
                   jax.ShapeDtypeStruct((t, PEER_SEL), F32)],
        grid=(t // tm,),
        in_specs=[pl.BlockSpec((nq, tm, LANES), lambda i: (0, i, 0)),
                  pl.BlockSpec(sk.shape, lambda i: (0, 0, 0))],
        out_specs=[pl.BlockSpec((tm, PEER_SEL), lambda i: (i, 0))] * 2,
        scratch_shapes=[pltpu.VMEM((PEER_SEL, tm), jnp.int32), pltpu.VMEM((PEER_SEL, tm), F32)],
        compiler_params=_cparams(("parallel",)),
        name="route",
    )(qp, sk)


PEER_TB = 512
PEER_NB = SUBLANES
PEER_STAGED_TB = 32
PEER_NG = 4
PEER_LA = 2
HI16 = -65536


def _gelu(a):
    return 0.5 * a * (1.0 + lax.erf(a * (2.0 ** -0.5)))


def _peer_token(xr, gr, tile):
    nch = D_MODEL // LANES
    nkt = PEER_SEL // SUBLANES
    xb = [jnp.broadcast_to(xr[:, c * LANES:(c + 1) * LANES], (SUBLANES, LANES)) for c in range(nch)]
    parts = []
    for kt in range(nkt):
        acc = None
        for c in range(nch):
            u = lax.bitcast_convert_type(tile(kt, c) & HI16, F32)
            acc = u * xb[c] if acc is None else acc + u * xb[c]
        parts.append(acc)
    part = jnp.concatenate(parts, axis=0)
    a_row = jnp.sum(part.T, axis=0, keepdims=True)
    act = _gelu(a_row) * gr
    act_b = jnp.broadcast_to(act, (LANES, PEER_SEL)).T
    outs = []
    for c in range(nch):
        acc = None
        for kt in range(nkt):
            v = lax.bitcast_convert_type(tile(kt, c) << 16, F32)
            t = act_b[kt * SUBLANES:(kt + 1) * SUBLANES, :] * v
            acc = t if acc is None else acc + t
        outs.append(jnp.sum(acc, axis=0, keepdims=True))
    return jnp.concatenate(outs, axis=1)


def _peer_staged_kernel(st_ref, xn_ref, g_ref, rest_ref, after_ref, out_ref):
    del rest_ref, after_ref
    xg = xn_ref[...]
    gg = g_ref[...]
    rows = []
    for tb in range(xn_ref.shape[0]):
        def tile(kt, c, tb=tb):
            return st_ref[tb, kt * SUBLANES:(kt + 1) * SUBLANES, c * LANES:(c + 1) * LANES]
        rows.append(_peer_token(xg[tb:tb + 1, :], gg[tb:tb + 1, :], tile))
    out_ref[...] = jnp.concatenate(rows, axis=0)


def _peer_staged(staged, xn, g, rest, after=None):
    ts = staged.shape[0]
    t, d = xn.shape
    tb = PEER_STAGED_TB
    return pl.pallas_call(
        _peer_staged_kernel,
        out_shape=jax.ShapeDtypeStruct((t, d), F32),
        grid=(ts // tb,),
        in_specs=[pl.BlockSpec((tb, PEER_SEL, d), lambda i: (i, 0, 0)),
                  pl.BlockSpec((tb, d), lambda i: (i, 0)),
                  pl.BlockSpec((tb, PEER_SEL), lambda i: (i, 0)),
                  pl.BlockSpec(memory_space=pl.ANY),
                  pl.BlockSpec(memory_space=pl.ANY)],
        out_specs=pl.BlockSpec((tb, d), lambda i: (i, 0)),
        input_output_aliases={3: 0},
        compiler_params=_cparams(("parallel",)),
        name="peer_staged",
    )(staged, xn, g, rest, _order_after(after))


SC_CORES = 2
SC_SUBCORES = 16
SC_WORKERS = SC_CORES * SC_SUBCORES
SC_ROWS = 32
SC_IDX_BLK = 128
PEER_SC_SHARE_NUM = 12
PEER_SC_SHARE_DEN = 16
PEER_SC_SHARE_LAST_CUT = 2
FIRST_GROUP_PARTS = 4
PEER_SC_QUANTUM = SC_WORKERS * SC_ROWS * SC_IDX_BLK // PEER_SEL


def _stage_rows(uv2, idx):
    n_tok, nsel = idx.shape
    n_rows = n_tok * nsel
    rows_per_w = n_rows // SC_WORKERS
    n_blk = rows_per_w // (SC_ROWS * SC_IDX_BLK)
    idx4 = idx.reshape(SC_WORKERS, n_blk, SC_IDX_BLK, SC_ROWS)
    mesh = plsc.VectorSubcoreMesh(core_axis_name="c", subcore_axis_name="s")

    @functools.partial(
        pl.kernel, mesh=mesh,
        out_type=jax.ShapeDtypeStruct((n_rows, uv2.shape[1]), uv2.dtype),
        scratch_types=[
            pltpu.VMEM((SC_IDX_BLK, SC_ROWS), jnp.int32),
            pltpu.VMEM((2, SC_ROWS, uv2.shape[1]), uv2.dtype),
            pltpu.SemaphoreType.DMA((2,)),
            pltpu.SemaphoreType.DMA((2,)),
        ],
    )
    def stage(uv_hbm, idx_hbm, out_hbm, idx_v, rows_v, gsem, wsem):
        wid = lax.axis_index("s") * SC_CORES + lax.axis_index("c")
        base = wid * rows_per_w

        def gather(j, b):
            return pltpu.make_async_copy(uv_hbm.at[idx_v.at[j]], rows_v.at[b], gsem.at[b])

        def write(row0, b):
            return pltpu.make_async_copy(rows_v.at[b], out_hbm.at[pl.ds(row0, SC_ROWS)], wsem.at[b])

        @pl.loop(0, n_blk)
        def _(blk):
            pltpu.sync_copy(idx_hbm.at[wid, blk], idx_v)
            blk_row0 = base + blk * (SC_IDX_BLK * SC_ROWS)
            gather(0, 0).start()

            @pl.loop(0, SC_IDX_BLK, step=2)
            def _(j):
                for b in range(2):
                    jj = j + b
                    row0 = pl.multiple_of(blk_row0 + jj * SC_ROWS, SC_ROWS)
                    gather(jj, b).wait()

                    @pl.when(jj > 0)
                    def _():
                        write(row0 - SC_ROWS, 1 - b).wait()

                    @pl.when(jj + 1 < SC_IDX_BLK)
                    def _():
                        gather(jj + 1, 1 - b).start()

                    write(row0, b).start()

            write(blk_row0 + (SC_IDX_BLK - 1) * SC_ROWS, 1).wait()

    return stage(uv2, idx4)


def _peer_kernel(step0, idx_hbm, xn_ref, g_ref, uv_hbm, after_ref, out_ref, idx_s, buf, sem_i, sem):
    del after_ref
    step = pl.program_id(0) + step0
    nbatch = xn_ref.shape[0] // PEER_NB
    nch = D_MODEL // LANES
    nkt = PEER_SEL // SUBLANES

    cp = pltpu.make_async_copy(idx_hbm.at[step], idx_s, sem_i.at[0])
    cp.start()
    cp.wait()

    def issue(b):
        grp = b & (PEER_NG - 1)

        def tok_body(tb, carry):
            base = (b * PEER_NB + tb) * PEER_SEL
            for k in range(PEER_SEL):
                e = idx_s[base + k]
                pltpu.make_async_copy(
                    uv_hbm.at[e], buf.at[grp, tb, k // SUBLANES, :, k % SUBLANES, :], sem.at[grp]
                ).start(priority=k % 2)
            return carry

        lax.fori_loop(0, PEER_NB, tok_body, 0)

    def wait(b):
        grp = b & (PEER_NG - 1)
        pltpu.make_async_copy(buf.at[grp], buf.at[grp], sem.at[grp]).wait()

    def compute(b):
        grp = b & (PEER_NG - 1)
        r0 = pl.multiple_of(b * PEER_NB, PEER_NB)
        xg = xn_ref[pl.ds(r0, PEER_NB), :]
        gg = g_ref[pl.ds(r0, PEER_NB), :]
        rows = [_peer_token(xg[tb:tb + 1, :], gg[tb:tb + 1, :],
                            functools.partial(lambda tb, kt, c: buf[grp, tb, kt, c], tb))
                for tb in range(PEER_NB)]
        out_ref[pl.ds(r0, PEER_NB), :] = jnp.concatenate(rows, axis=0)

    for b in range(PEER_LA):
        issue(b)

    def body(b, carry):
        wait(b)

        @pl.when(b + PEER_LA < nbatch)
        def _():
            issue(b + PEER_LA)

        compute(b)
        return carry

    lax.fori_loop(0, nbatch, body, 0)


def _order_after(after):
    return jnp.zeros((SUBLANES, LANES), F32) if after is None else after


def _peer(idx, xn, g, uv, ts, after=None):
    t, d = xn.shape
    tb = PEER_TB
    idx2 = idx.reshape(t // tb, tb * PEER_SEL)
    s0 = ts // tb
    return pl.pallas_call(
        functools.partial(_peer_kernel, s0),
        out_shape=jax.ShapeDtypeStruct((t, d), F32),
        grid=((t - ts) // tb,),
        in_specs=[pl.BlockSpec(memory_space=pl.ANY),
                  pl.BlockSpec((tb, d), lambda i: (i + s0, 0)),
                  pl.BlockSpec((tb, PEER_SEL), lambda i: (i + s0, 0)),
                  pl.BlockSpec(memory_space=pl.ANY),
                  pl.BlockSpec(memory_space=pl.ANY)],
        out_specs=pl.BlockSpec((tb, d), lambda i: (i + s0, 0)),
        scratch_shapes=[pltpu.SMEM((tb * PEER_SEL,), jnp.int32),
                        pltpu.VMEM((PEER_NG, PEER_NB, PEER_SEL // SUBLANES, d // LANES, SUBLANES, LANES),
                                   jnp.int32),
                        pltpu.SemaphoreType.DMA((1,)),
                        pltpu.SemaphoreType.DMA((PEER_NG,))],
        compiler_params=_cparams(("arbitrary",)),
        name="peer",
    )(idx2, xn, g, uv, _order_after(after))


def _ple_kernel(x1_ref, pe_ref, p_ref, nw_ref, wg_ref, wp_ref, nf_ref, y_ref):
    x2 = x1_ref[...] + pe_ref[...]
    gate = _sigmoid(_dot(_rms(x2, nw_ref[...]).astype(BF16), wg_ref[...]))
    x3 = x2 + _dot(p_ref[...].astype(BF16), wp_ref[...]) * gate
    y_ref[...] = _rms(x3, nf_ref[...])


def _ple(x1, pe, p, nw, wg, wp, nf, tm):
    t, d = x1.shape
    full = lambda shape: pl.BlockSpec(shape, lambda i: (0,) * len(shape))
    tok = lambda w: pl.BlockSpec((tm, w), lambda i: (i, 0))
    return pl.pallas_call(
        _ple_kernel,
        out_shape=jax.ShapeDtypeStruct((t, d), F32),
        grid=(t // tm,),
        in_specs=[tok(d), tok(d), tok(PLE_DIM), full(nw.shape), full(wg.shape), full(wp.shape),
                  full(nf.shape)],
        out_specs=tok(d),
        compiler_params=_cparams(("parallel",)),
        name="ple",
    )(x1, pe, p, nw, wg, wp, nf)


def _rope_tables(s):
    half = ROT_DIM // 2
    inv = ROPE_THETA ** (-jnp.arange(0, ROT_DIM, 2, dtype=F32) / ROT_DIM)
    ang = jnp.arange(s, dtype=F32)[:, None] * inv[None, :]
    cos, sin = jnp.cos(ang), jnp.sin(ang)
    ones = jnp.ones((s, ATT_QK_DIM - ROT_DIM), F32)
    zeros = jnp.zeros((s, ATT_QK_DIM - ROT_DIM), F32)
    zh = jnp.zeros((s, half), F32)
    c64 = jnp.concatenate([cos, cos, ones], axis=1)
    s1_64 = jnp.concatenate([zh, sin, zeros], axis=1)
    s2_64 = jnp.concatenate([-sin, zh, zeros], axis=1)
    rep = LANES // ATT_QK_DIM
    return jnp.tile(c64, (1, rep)), jnp.tile(s1_64, (1, rep)), jnp.tile(s2_64, (1, rep))


def _pack_uv(u, v):
    ub = lax.bitcast_convert_type(u.astype(BF16), jnp.uint16).astype(jnp.uint32)
    vb = lax.bitcast_convert_type(v.astype(BF16), jnp.uint16).astype(jnp.uint32)
    return lax.bitcast_convert_type((ub << 16) | vb, jnp.int32)


def _mix_route(x, w):
    b, s, d = x.shape
    t = b * s
    nc = s // M_CHUNK
    rc, rs1, rs2 = _rope_tables(s)
    aq, ak, av, mq, mk, mv, mo, gc, gr = _inproj(
        x, w["norm_mix"], w["wqk"], w["wav"], w["wmqk"], w["wmvo"], w["wg"], w["wgt"], w["gb"],
        w["gbt"], w["conv"], rc, rs1, rs2, tm=512)
    att = _attn(w["lq"], aq, ak, av, w["attn_gain"], tq=256)
    gc5 = gc.reshape(b, nc, M_CHUNK, M_HEADS, 4).transpose(0, 3, 1, 2, 4)
    gr5 = gr.reshape(b, M_HEADS, 4, nc, M_CHUNK).transpose(0, 1, 3, 2, 4)
    mem = _mlstm(mq, mk, mv, mo, gc5, gr5, w["mlstm_gain"])
    x1, xn, qp = _outproj(x.reshape(t, d), att.reshape(t, ATT_W), mem.reshape(t, M_W),
                          w["wo_a"], w["wo_m"], w["norm_ffn"], w["wq"], tm=512)
    idx, g = _route(qp, w["sk"], tm=128)
    return x1, xn, idx, g


def _peer_all(groups, uv2):
    uv3 = uv2.reshape(-1, D_MODEL // LANES, LANES)
    staged, outs = [], []
    prev = None
    for gi, (xn, idx, g) in enumerate(groups):
        t, d = xn.shape
        num = PEER_SC_SHARE_NUM - PEER_SC_SHARE_LAST_CUT * (gi == len(groups) - 1)
        ts = (t * num // PEER_SC_SHARE_DEN) // PEER_SC_QUANTUM * PEER_SC_QUANTUM
        assert 0 < ts < t and (t - ts) % PEER_TB == 0, (t, ts)
        staged.append(_stage_rows(uv2, idx[:ts]).reshape(ts, PEER_SEL, d))
        prev = _peer(idx, xn, g, uv3, ts, after=prev)
        outs.append(prev)
    for i, (xn, idx, g) in enumerate(groups):
        prev = _peer_staged(staged[i], xn, g, outs[i], after=prev)
        outs[i] = prev
    return outs


def kernel(x_prompt, x_sample, p_prompt, p_sample, norm_mix, w_in, gate_bias, conv_qk, lambda_qk,
           attn_gain, mlstm_gain, w_out, norm_ffn, peer_wq, peer_subkeys, peer_u, peer_v, norm_ple,
           ple_w_gate, ple_w_proj, norm_final):
    li = 0
    win = w_in[li]
    c0 = 0
    cols = {}
    for name, width in (("aq", ATT_W), ("ak", ATT_W), ("av", ATT_W), ("mq", M_W), ("mk", M_W),
                        ("mv", M_W), ("mo", M_W), ("gt", N_GATES)):
        cols[name] = win[:, c0:c0 + width]
        c0 += width
    perm = jnp.array([ty * M_HEADS + h for h in range(M_HEADS) for ty in range(4)], jnp.int32)
    wg = cols["gt"][:, perm].astype(BF16)
    gb = gate_bias[li][perm]
    w = {
        "norm_mix": norm_mix[li][None, :],
        "wqk": jnp.concatenate([cols["aq"], cols["ak"]], axis=1).astype(BF16),
        "wav": cols["av"].astype(BF16),
        "wmqk": jnp.concatenate([cols["mq"], cols["mk"]], axis=1).astype(BF16),
        "wmvo": jnp.concatenate([cols["mv"], cols["mo"]], axis=1).astype(BF16),
        "wg": wg,
        "wgt": wg.T,
        "gb": gb[None, :],
        "gbt": gb[:, None],
        "conv": conv_qk[li],
        "lq": lambda_qk[li],
        "attn_gain": attn_gain[li][None, :],
        "mlstm_gain": mlstm_gain[li][None, :],
        "wo_a": w_out[li][:ATT_W].astype(BF16),
        "wo_m": w_out[li][ATT_W:].astype(BF16),
        "norm_ffn": norm_ffn[li][None, :],
        "wq": peer_wq[li].astype(BF16),
        "sk": peer_subkeys[li].reshape(PEER_HEADS * 2, N_KEYS, PEER_HALF).astype(BF16),
        "uv": _pack_uv(peer_u[li], peer_v[li]),
        "norm_ple": norm_ple[li][None, :],
        "ple_wg": ple_w_gate[li].astype(BF16),
        "ple_wp": ple_w_proj[li].astype(BF16),
        "norm_final": norm_final[None, :],
    }
    parts = FIRST_GROUP_PARTS
    rows = x_prompt.shape[0] // parts
    xs = tuple(x_prompt[i * rows:(i + 1) * rows] for i in range(parts)) + (x_sample,)
    ps = tuple(p_prompt[li][i * rows:(i + 1) * rows] for i in range(parts)) + (p_sample[li],)
    mixed = [_mix_route(x, w) for x in xs]
    pes = _peer_all([(xn, idx, g) for _, xn, idx, g in mixed], w["uv"])
    ys = []
    for x, p, (x1, _, _, _), pe in zip(xs, ps, mixed, pes):
        y = _ple(x1, pe, p.reshape(-1, PLE_DIM), w["norm_ple"], w["ple_wg"], w["ple_wp"],
                 w["norm_final"], tm=512)
        ys.append(y.reshape(x.shape))
    return (jnp.concatenate(ys[:parts], axis=0), ys[parts])
```

```python
import functools
import math

import jax
import jax.numpy as jnp
from jax import lax
from jax.experimental import pallas as pl
from jax.experimental.pallas import tpu as pltpu
from jax.experimental.pallas import tpu_sc as plsc

F32 = jnp.float32
BF16 = jnp.bfloat16

D_MODEL = 1024
ATT_W = 512
ATT_HEADS = 4
ATT_V_DIM = 128
ATT_QK_DIM = 64
ROT_DIM = 16
ROPE_THETA = 500000.0
M_W = 512
M_HEADS = 4
M_HEAD_DIM = 128
M_CHUNK = 128
CONV_K = 5
N_GATES = 16
PEER_HEADS = 8
PEER_QDIM = 256
PEER_HALF = 128
N_KEYS = 128
PEER_TOPK = 16
PEER_SEL = PEER_HEADS * PEER_TOPK
PLE_DIM = 256
EPS = 1e-6
LAM_INIT = 0.8 - 0.6 * math.exp(-0.3 * 0)
Q_SCALE = ATT_QK_DIM ** -0.5 * math.log2(math.e)

LANES = 128
SUBLANES = 8
VMEM_LIMIT = 56 * 1024 * 1024

NEG_INF = float("-inf")


def _cparams(sem):
    return pltpu.CompilerParams(dimension_semantics=sem, vmem_limit_bytes=VMEM_LIMIT)


def _dot(a, b):
    return jnp.dot(a, b, preferred_element_type=F32)


def _dot_nt(a, b):
    return lax.dot_general(a, b, (((1,), (1,)), ((), ())), preferred_element_type=F32)


def _rms(x, w):
    ms = jnp.mean(x * x, axis=-1, keepdims=True)
    return (x * lax.rsqrt(ms + EPS)) * w


def _sigmoid(x):
    return 1.0 / (1.0 + jnp.exp(-x))


def _log_sigmoid(x):
    return jnp.minimum(x, 0.0) - jnp.log(1.0 + jnp.exp(-jnp.abs(x)))


def _inproj_kernel(x_ref, xp_ref, xn_ref, nw_ref, wqk_ref, wav_ref, wmqk_ref, wmvo_ref,
                   wg_ref, wgt_ref, gb_ref, gbt_ref, cw_ref, rc_ref, rs1_ref, rs2_ref,
                   aq_ref, ak_ref, av_ref, mq_ref, mk_ref, mv_ref, mo_ref, gc_ref, gr_ref):
    i = pl.program_id(1)
    last = pl.num_programs(1) - 1
    nw = nw_ref[...]
    tm = x_ref.shape[1]

    h = _rms(x_ref[0], nw).astype(BF16)
    hp = _rms(xp_ref[0, 0], nw).astype(BF16)
    hn = _rms(xn_ref[0, 0], nw).astype(BF16)

    qk = _dot(h, wqk_ref[...])
    rc = rc_ref[...]
    rs1 = rs1_ref[...]
    rs2 = rs2_ref[...]
    half = ROT_DIM // 2
    for c in range(2 * ATT_W // LANES):
        xc = qk[:, c * LANES:(c + 1) * LANES]
        r = xc * rc + pltpu.roll(xc, half, 1) * rs1 + pltpu.roll(xc, LANES - half, 1) * rs2
        if c < ATT_W // LANES:
            aq_ref[0, :, c * LANES:(c + 1) * LANES] = (r * Q_SCALE).astype(BF16)
        else:
            cc = c - ATT_W // LANES
            ak_ref[0, :, cc * LANES:(cc + 1) * LANES] = r.astype(BF16)

    av_ref[0] = _dot(h, wav_ref[...]).astype(BF16)

    wm = wmqk_ref[...]
    pm = _dot(h, wm)
    pp = _dot(hp, wm) * (i > 0).astype(F32)
    pn = _dot(hn, wm) * (i < last).astype(F32)
    ext = jnp.concatenate([pp, pm, pn], axis=0)
    cw = cw_ref[...]
    off = SUBLANES - CONV_K // 2
    conv = cw[0:1, :] * ext[off:off + tm, :]
    for j in range(1, CONV_K):
        conv = conv + cw[j:j + 1, :] * ext[off + j:off + j + tm, :]
    act = conv * _sigmoid(conv)
    mq_ref[0] = act[:, :M_W].astype(BF16)
    mk_ref[0] = (act[:, M_W:] * (M_HEAD_DIM ** -0.5)).astype(BF16)

    vo = _dot(h, wmvo_ref[...])
    mv_ref[0] = vo[:, :M_W].astype(BF16)
    mo_ref[0] = vo[:, M_W:].astype(BF16)

    gc_ref[0] = _dot(h, wg_ref[...]) + gb_ref[...]
    gr_ref[0] = _dot_nt(wgt_ref[...], h) + gbt_ref[...]


def _inproj(x, norm_w, wqk, wav, wmqk, wmvo, wg, wgt, gb, gbt, conv_w, rc, rs1, rs2, tm):
    b, s, d = x.shape
    nt = s // tm
    x8 = x.reshape(b, s // SUBLANES, SUBLANES, d)
    r8 = tm // SUBLANES
    nb8 = s // SUBLANES
    full = lambda shape: pl.BlockSpec(shape, lambda bi, i: (0,) * len(shape))
    tok = lambda w: pl.BlockSpec((1, tm, w), lambda bi, i: (bi, i, 0))
    out_shapes = (
        [jax.ShapeDtypeStruct((b, s, ATT_W), BF16)] * 3
        + [jax.ShapeDtypeStruct((b, s, M_W), BF16)] * 4
        + [jax.ShapeDtypeStruct((b, s, N_GATES), F32), jax.ShapeDtypeStruct((b, N_GATES, s), F32)]
    )
    return pl.pallas_call(
        _inproj_kernel,
        out_shape=out_shapes,
        grid=(b, nt),
        in_specs=[
            tok(d),
            pl.BlockSpec((1, 1, SUBLANES, d), lambda bi, i: (bi, jnp.maximum(i * r8 - 1, 0), 0, 0)),
            pl.BlockSpec((1, 1, SUBLANES, d), lambda bi, i: (bi, jnp.minimum((i + 1) * r8, nb8 - 1), 0, 0)),
            full((1, d)),
            full(wqk.shape), full(wav.shape), full(wmqk.shape), full(wmvo.shape),
            full(wg.shape), full(wgt.shape), full(gb.shape), full(gbt.shape), full(conv_w.shape),
            pl.BlockSpec((tm, LANES), lambda bi, i: (i, 0)),
            pl.BlockSpec((tm, LANES), lambda bi, i: (i, 0)),
            pl.BlockSpec((tm, LANES), lambda bi, i: (i, 0)),
        ],
        out_specs=[tok(ATT_W)] * 3 + [tok(M_W)] * 4
        + [tok(N_GATES), pl.BlockSpec((1, N_GATES, tm), lambda bi, i: (bi, 0, i))],
        compiler_params=_cparams(("parallel", "parallel")),
        name="inproj",
    )(x, x8, x8, norm_w, wqk, wav, wmqk, wmvo, wg, wgt, gb, gbt, conv_w, rc, rs1, rs2)


ATT_KEY_BLOCK = 1024


def _attn_kernel(lq_ref, q_ref, k_ref, v_ref, gain_ref, o_ref):
    lq = lq_ref[...]
    lam = (jnp.exp(jnp.sum(lq[0:1] * lq[1:2], axis=1, keepdims=True))
           - jnp.exp(jnp.sum(lq[2:3] * lq[3:4], axis=1, keepdims=True)) + LAM_INIT)
    q = q_ref[0]
    tq = q.shape[0]
    lane = lax.broadcasted_iota(jnp.int32, q.shape, 1)
    zero = jnp.zeros_like(q)
    qq = jnp.concatenate([jnp.where(lane < ATT_QK_DIM, q, zero),
                          jnp.where(lane >= ATT_QK_DIM, q, zero)], axis=0)
    m = jnp.full((2 * tq, 1), NEG_INF, F32)
    ln = jnp.zeros((2 * tq, 1), F32)
    acc = jnp.zeros((2 * tq, ATT_V_DIM), F32)
    kb = min(ATT_KEY_BLOCK, k_ref.shape[1])
    assert k_ref.shape[1] % kb == 0
    for c in range(k_ref.shape[1] // kb):
        ks = slice(c * kb, (c + 1) * kb)
        sc = _dot_nt(qq, k_ref[0, ks, :])
        m_new = jnp.maximum(m, jnp.max(sc, axis=1, keepdims=True))
        alpha = jnp.exp2(m - m_new)
        e = jnp.exp2(sc - m_new)
        ln = alpha * ln + jnp.sum(e, axis=1, keepdims=True)
        acc = alpha * acc + _dot(e.astype(BF16), v_ref[0, ks, :])
        m = m_new
    o = acc[:tq] * (1.0 / ln[:tq]) - acc[tq:] * (lam / ln[tq:])
    o_ref[0] = (_rms(o, gain_ref[...]) * (1.0 - LAM_INIT)).astype(BF16)


def _attn(lq, q, k, v, gain, tq):
    b, s, _ = q.shape
    return pl.pallas_call(
        _attn_kernel,
        out_shape=jax.ShapeDtypeStruct((b, s, ATT_W), BF16),
        grid=(b, ATT_HEADS, s // tq),
        in_specs=[
            pl.BlockSpec(lq.shape, lambda bi, h, i: (0, 0)),
            pl.BlockSpec((1, tq, ATT_V_DIM), lambda bi, h, i: (bi, i, h)),
            pl.BlockSpec((1, s, ATT_V_DIM), lambda bi, h, i: (bi, 0, h)),
            pl.BlockSpec((1, s, ATT_V_DIM), lambda bi, h, i: (bi, 0, h)),
            pl.BlockSpec((1, ATT_V_DIM), lambda bi, h, i: (0, 0)),
        ],
        out_specs=pl.BlockSpec((1, tq, ATT_V_DIM), lambda bi, h, i: (bi, i, h)),
        compiler_params=_cparams(("parallel", "parallel", "parallel")),
        name="attn",
    )(lq, q, k, v, gain)


def _mlstm_chunk(q, k, v, ic_col, lf_col, ic_row, lf_row, cst, n, m, fwd):
    ln = q.shape[0]
    row = lax.broadcasted_iota(jnp.int32, (ln, ln), 0)
    col = lax.broadcasted_iota(jnp.int32, (ln, ln), 1)
    lower = col <= row
    upper = col >= row
    mask, mask_t = (lower, upper) if fwd else (upper, lower)
    bcum_col = jnp.sum(jnp.where(mask, lf_row, 0.0), axis=1, keepdims=True)
    bcum_row = jnp.sum(jnp.where(mask_t, lf_col, 0.0), axis=0, keepdims=True)
    dlog = jnp.where(mask, bcum_col - bcum_row + ic_row, NEG_INF)
    inter = bcum_col + m
    m_t = jnp.maximum(inter, jnp.max(dlog, axis=1, keepdims=True))
    w = jnp.exp(dlog - m_t)
    sc = jnp.exp(inter - m_t)
    wqk = w * _dot_nt(q, k)
    num = sc * _dot(q, cst.astype(BF16)) + _dot(wqk.astype(BF16), v)
    qn = jnp.sum(q.astype(F32) * n, axis=1, keepdims=True)
    den = sc * qn + jnp.sum(wqk, axis=1, keepdims=True)
    h = num / jnp.maximum(jnp.abs(den), jnp.exp(-m_t))
    b_last = jnp.sum(lf_row, axis=1, keepdims=True)
    rel_row = b_last - bcum_row + ic_row
    rel_col = b_last - bcum_col + ic_col
    m_new = jnp.maximum(b_last + m, jnp.max(rel_row, axis=1, keepdims=True))
    dec_col = jnp.exp(rel_col - m_new)
    keep = jnp.exp(b_last + m - m_new)
    kd = k.astype(F32) * dec_col
    c_new = keep * cst + lax.dot_general(kd.astype(BF16), v, (((0,), (0,)), ((), ())),
                                         preferred_element_type=F32)
    n_new = keep * n + jnp.sum(kd, axis=0, keepdims=True)
    return h, c_new, n_new, m_new


M_HEADS_PER_STEP = 1


def _mlstm_kernel(q_ref, k_ref, v_ref, o_ref, gc_ref, gr_ref, gain_ref, out_ref, hf_ref, hb_ref):
    s = q_ref.shape[1]
    nc = s // M_CHUNK
    d = M_HEAD_DIM
    heads = range(M_HEADS_PER_STEP)

    def step(j, c, carry, fwd):
        cst, n, m = carry
        sl = pl.ds(pl.multiple_of(c * M_CHUNK, M_CHUNK), M_CHUNK)
        ln = slice(j * d, (j + 1) * d)
        gc = gc_ref[0, j, c]
        gr = gr_ref[0, j, c]
        t = 0 if fwd else 2
        h, cst, n, m = _mlstm_chunk(
            q_ref[0, sl, ln], k_ref[0, sl, ln], v_ref[0, sl, ln],
            gc[:, t:t + 1], _log_sigmoid(gc[:, t + 1:t + 2]),
            gr[t:t + 1, :], _log_sigmoid(gr[t + 1:t + 2, :]),
            cst, n, m, fwd)
        return sl, ln, h, (cst, n, m)

    init = (jnp.zeros((d, d), F32), jnp.zeros((1, d), F32), jnp.zeros((1, 1), F32))

    def scan_body(c, carry):
        new = []
        for j in heads:
            cf, cb = carry[j]
            sl, ln, h, cf = step(j, c, cf, True)
            hf_ref[sl, ln] = h
            sl, ln, h, cb = step(j, nc - 1 - c, cb, False)
            hb_ref[sl, ln] = h
            new.append((cf, cb))
        return tuple(new)

    lax.fori_loop(0, nc, scan_body, tuple((init, init) for _ in heads))

    def gate_body(c, carry):
        sl = pl.ds(pl.multiple_of(c * M_CHUNK, M_CHUNK), M_CHUNK)
        for j in heads:
            ln = slice(j * d, (j + 1) * d)
            y = _sigmoid(o_ref[0, sl, ln].astype(F32)) * (hf_ref[sl, ln] + hb_ref[sl, ln])
            out_ref[0, sl, ln] = _rms(y, gain_ref[:, ln]).astype(BF16)
        return carry

    lax.fori_loop(0, nc, gate_body, 0)


def _mlstm(q, k, v, o, gc, gr, gain):
    b, s, _ = q.shape
    nc = s // M_CHUNK
    hp = M_HEADS_PER_STEP
    w = hp * M_HEAD_DIM
    head = pl.BlockSpec((1, s, w), lambda bi, h: (bi, 0, h))
    return pl.pallas_call(
        _mlstm_kernel,
        out_shape=jax.ShapeDtypeStruct((b, s, M_W), BF16),
        grid=(b, M_HEADS // hp),
        in_specs=[
            head, head, head, head,
            pl.BlockSpec((1, hp, nc, M_CHUNK, 4), lambda bi, h: (bi, h, 0, 0, 0)),
            pl.BlockSpec((1, hp, nc, 4, M_CHUNK), lambda bi, h: (bi, h, 0, 0, 0)),
            pl.BlockSpec((1, w), lambda bi, h: (0, h)),
        ],
        out_specs=head,
        scratch_shapes=[pltpu.VMEM((s, w), F32), pltpu.VMEM((s, w), F32)],
        compiler_params=_cparams(("parallel", "parallel")),
        name="mlstm",
    )(q, k, v, o, gc, gr, gain)


def _outproj_kernel(x_ref, att_ref, mem_ref, wa_ref, wm_ref, nw_ref, wq_ref, x1_ref, xn_ref, qp_ref):
    x1 = x_ref[...] + _dot(att_ref[...], wa_ref[...]) + _dot(mem_ref[...], wm_ref[...])
    x1_ref[...] = x1
    xn = _rms(x1, nw_ref[...])
    xn_ref[...] = xn
    qp = _dot(xn.astype(BF16), wq_ref[...])
    for c in range(qp_ref.shape[0]):
        qp_ref[c] = qp[:, c * LANES:(c + 1) * LANES].astype(BF16)


def _outproj(x, att, mem, wa, wm, nw, wq, tm):
    t, d = x.shape
    nq = wq.shape[1] // LANES
    full = lambda shape: pl.BlockSpec(shape, lambda i: (0,) * len(shape))
    tok = lambda w: pl.BlockSpec((tm, w), lambda i: (i, 0))
    return pl.pallas_call(
        _outproj_kernel,
        out_shape=[jax.ShapeDtypeStruct((t, d), F32), jax.ShapeDtypeStruct((t, d), F32),
                   jax.ShapeDtypeStruct((nq, t, LANES), BF16)],
        grid=(t // tm,),
        in_specs=[tok(d), tok(ATT_W), tok(M_W), full(wa.shape), full(wm.shape), full(nw.shape),
                  full(wq.shape)],
        out_specs=[tok(d), tok(d), pl.BlockSpec((nq, tm, LANES), lambda i: (0, i, 0))],
        compiler_params=_cparams(("parallel",)),
        name="outproj",
    )(x, att, mem, wa, wm, nw, wq)


ROUTE_HEADS_PER_ITER = 8


def _topk_rows(sc, n):
    pos = lax.broadcasted_iota(jnp.int32, sc.shape, 0).astype(F32)
    vals, poss = [], []
    for _ in range(PEER_TOPK):
        m = jnp.max(sc, axis=0, keepdims=True)
        p = jnp.min(jnp.where(sc == m, pos, float(n)), axis=0, keepdims=True)
        vals.append(m)
        poss.append(p)
        sc = jnp.where(pos == p, NEG_INF, sc)
    return jnp.concatenate(vals, axis=0), jnp.concatenate(poss, axis=0).astype(jnp.int32)


def _route_kernel(qp_ref, sk_ref, idx_ref, g_ref, idx_t, g_t):
    def head(h, carry):
        s1 = _dot_nt(sk_ref[2 * h], qp_ref[2 * h])
        s2 = _dot_nt(sk_ref[2 * h + 1], qp_ref[2 * h + 1])
        v1, i1 = _topk_rows(s1, N_KEYS)
        v2, i2 = _topk_rows(s2, N_KEYS)
        k = PEER_TOPK
        sub = SUBLANES
        brow = lax.broadcasted_iota(jnp.int32, (sub, v2.shape[1]), 0)
        slabs = [v1[0:1, :] + v2]
        for a in range(1, sub):
            slabs.append(jnp.where(brow < k // (a + 1), v1[a:a + 1, :] + v2[0:sub, :], NEG_INF))
        slabs.append(v1[sub:k, :] + v2[0:1, :])
        cand = jnp.concatenate(slabs, axis=0)
        sc, cp = _topk_rows(cand, cand.shape[0])
        mid = cp < k + sub * (sub - 1)
        slab = lax.shift_right_logical(cp, int(math.log2(sub)))
        ca = jnp.where(cp < k, 0, jnp.where(mid, slab - 1, cp - (k + sub * (sub - 1)) + sub))
        cb = jnp.where(cp < k, cp, jnp.where(mid, cp & (sub - 1), 0))
        e1 = jnp.zeros_like(cp)
        e2 = jnp.zeros_like(cp)
        for a in range(PEER_TOPK):
            e1 = jnp.where(ca == a, i1[a:a + 1, :], e1)
            e2 = jnp.where(cb == a, i2[a:a + 1, :], e2)
        ex = jnp.exp(sc - sc[0:1, :])
        rows = pl.ds(pl.multiple_of(h * PEER_TOPK, PEER_TOPK), PEER_TOPK)
        idx_t[rows, :] = e1 * N_KEYS + e2
        g_t[rows, :] = ex / jnp.sum(ex, axis=0, keepdims=True)
        return carry

    def heads(hh, carry):
        for j in range(ROUTE_HEADS_PER_ITER):
            head(hh * ROUTE_HEADS_PER_ITER + j, carry)
        return carry

    lax.fori_loop(0, PEER_HEADS // ROUTE_HEADS_PER_ITER, heads, 0)
    idx_ref[...] = idx_t[...].T
    g_ref[...] = g_t[...].T


def _route(qp, sk, tm):
    nq, t, _ = qp.shape
    return pl.pallas_call(
        _route_kernel,
        out_shape=[jax.ShapeDtypeStruct((t, PEER_SEL), jnp.int32),
                   jax.ShapeDtypeStruct((t, PEER_SEL), F32)],
        grid=(t // tm,),
        in_specs=[pl.BlockSpec((nq, tm, LANES), lambda i: (0, i, 0)),
                  pl.BlockSpec(sk.shape, lambda i: (0, 0, 0))],
        out_specs=[pl.BlockSpec((tm, PEER_SEL), lambda i: (i, 0))] * 2,
        scratch_shapes=[pltpu.VMEM((PEER_SEL, tm), jnp.int32), pltpu.VMEM((PEER_SEL, tm), F32)],
        compiler_params=_cparams(("parallel",)),
        name="route",
    )(qp, sk)


PEER_TB = 512
PEER_NB = SUBLANES
PEER_STAGED_TB = 32
PEER_NG = 4
PEER_LA = 2
HI16 = -65536


def _gelu(a):
    return 0.5 * a * (1.0 + lax.erf(a * (2.0 ** -0.5)))


def _peer_token(xr, gr, tile):
    nch = D_MODEL // LANES
    nkt = PEER_SEL // SUBLANES
    xb = [jnp.broadcast_to(xr[:, c * LANES:(c + 1) * LANES], (SUBLANES, LANES)) for c in range(nch)]
    parts = []
    for kt in range(nkt):
        acc = None
        for c in range(nch):
            u = lax.bitcast_convert_type(tile(kt, c) & HI16, F32)
            acc = u * xb[c] if acc is None else acc + u * xb[c]
        parts.append(acc)
    part = jnp.concatenate(parts, axis=0)
    a_row = jnp.sum(part.T, axis=0, keepdims=True)
    act = _gelu(a_row) * gr
    act_b = jnp.broadcast_to(act, (LANES, PEER_SEL)).T
    outs = []
    for c in range(nch):
        acc = None
        for kt in range(nkt):
            v = lax.bitcast_convert_type(tile(kt, c) << 16, F32)
            t = act_b[kt * SUBLANES:(kt + 1) * SUBLANES, :] * v
            acc = t if acc is None else acc + t
        outs.append(jnp.sum(acc, axis=0, keepdims=True))
    return jnp.concatenate(outs, axis=1)


def _peer_staged_kernel(st_ref, xn_ref, g_ref, rest_ref, after_ref, out_ref):
    del rest_ref, after_ref
    xg = xn_ref[...]
    gg = g_ref[...]
    rows = []
    for tb in range(xn_ref.shape[0]):
        def tile(kt, c, tb=tb):
            return st_ref[tb, kt * SUBLANES:(kt + 1) * SUBLANES, c * LANES:(c + 1) * LANES]
        rows.append(_peer_token(xg[tb:tb + 1, :], gg[tb:tb + 1, :], tile))
    out_ref[...] = jnp.concatenate(rows, axis=0)


def _peer_staged(staged, xn, g, rest, after=None):
    ts = staged.shape[0]
    t, d = xn.shape
    tb = PEER_STAGED_TB
    return pl.pallas_call(
        _peer_staged_kernel,
        out_shape=jax.ShapeDtypeStruct((t, d), F32),
        grid=(ts // tb,),
        in_specs=[pl.BlockSpec((tb, PEER_SEL, d), lambda i: (i, 0, 0)),
                  pl.BlockSpec((tb, d), lambda i: (i, 0)),
                  pl.BlockSpec((tb, PEER_SEL), lambda i: (i, 0)),
                  pl.BlockSpec(memory_space=pl.ANY),
                  pl.BlockSpec(memory_space=pl.ANY)],
        out_specs=pl.BlockSpec((tb, d), lambda i: (i, 0)),
        input_output_aliases={3: 0},
        compiler_params=_cparams(("parallel",)),
        name="peer_staged",
    )(staged, xn, g, rest, _order_after(after))


SC_CORES = 2
SC_SUBCORES = 16
SC_WORKERS = SC_CORES * SC_SUBCORES
SC_ROWS = 32
SC_IDX_BLK = 128
PEER_SC_SHARE_NUM = 12
PEER_SC_SHARE_DEN = 16
PEER_SC_SHARE_LAST_CUT = 2
FIRST_GROUP_PARTS = 4
PEER_SC_QUANTUM = SC_WORKERS * SC_ROWS * SC_IDX_BLK // PEER_SEL


def _stage_rows(uv2, idx):
    n_tok, nsel = idx.shape
    n_rows = n_tok * nsel
    rows_per_w = n_rows // SC_WORKERS
    n_blk = rows_per_w // (SC_ROWS * SC_IDX_BLK)
    idx4 = idx.reshape(SC_WORKERS, n_blk, SC_IDX_BLK, SC_ROWS)
    mesh = plsc.VectorSubcoreMesh(core_axis_name="c", subcore_axis_name="s")

    @functools.partial(
        pl.kernel, mesh=mesh,
        out_type=jax.ShapeDtypeStruct((n_rows, uv2.shape[1]), uv2.dtype),
        scratch_types=[
            pltpu.VMEM((SC_IDX_BLK, SC_ROWS), jnp.int32),
            pltpu.VMEM((2, SC_ROWS, uv2.shape[1]), uv2.dtype),
            pltpu.SemaphoreType.DMA((2,)),
            pltpu.SemaphoreType.DMA((2,)),
        ],
    )
    def stage(uv_hbm, idx_hbm, out_hbm, idx_v, rows_v, gsem, wsem):
        wid = lax.axis_index("s") * SC_CORES + lax.axis_index("c")
        base = wid * rows_per_w

        def gather(j, b):
            return pltpu.make_async_copy(uv_hbm.at[idx_v.at[j]], rows_v.at[b], gsem.at[b])

        def write(row0, b):
            return pltpu.make_async_copy(rows_v.at[b], out_hbm.at[pl.ds(row0, SC_ROWS)], wsem.at[b])

        @pl.loop(0, n_blk)
        def _(blk):
            pltpu.sync_copy(idx_hbm.at[wid, blk], idx_v)
            blk_row0 = base + blk * (SC_IDX_BLK * SC_ROWS)
            gather(0, 0).start()

            @pl.loop(0, SC_IDX_BLK, step=2)
            def _(j):
                for b in range(2):
                    jj = j + b
                    row0 = pl.multiple_of(blk_row0 + jj * SC_ROWS, SC_ROWS)
                    gather(jj, b).wait()

                    @pl.when(jj > 0)
                    def _():
                        write(row0 - SC_ROWS, 1 - b).wait()

                    @pl.when(jj + 1 < SC_IDX_BLK)
                    def _():
                        gather(jj + 1, 1 - b).start()

                    write(row0, b).start()

            write(blk_row0 + (SC_IDX_BLK - 1) * SC_ROWS, 1).wait()

    return stage(uv2, idx4)


def _peer_kernel(step0, idx_hbm, xn_ref, g_ref, uv_hbm, after_ref, out_ref, idx_s, buf, sem_i, sem):
    del after_ref
    step = pl.program_id(0) + step0
    nbatch = xn_ref.shape[0] // PEER_NB
    nch = D_MODEL // LANES
    nkt = PEER_SEL // SUBLANES

    cp = pltpu.make_async_copy(idx_hbm.at[step], idx_s, sem_i.at[0])
    cp.start()
    cp.wait()

    def issue(b):
        grp = b & (PEER_NG - 1)

        def tok_body(tb, carry):
            base = (b * PEER_NB + tb) * PEER_SEL
            for k in range(PEER_SEL):
                e = idx_s[base + k]
                pltpu.make_async_copy(
                    uv_hbm.at[e], buf.at[grp, tb, k // SUBLANES, :, k % SUBLANES, :], sem.at[grp]
                ).start(priority=k % 2)
            return carry

        lax.fori_loop(0, PEER_NB, tok_body, 0)

    def wait(b):
        grp = b & (PEER_NG - 1)
        pltpu.make_async_copy(buf.at[grp], buf.at[grp], sem.at[grp]).wait()

    def compute(b):
        grp = b & (PEER_NG - 1)
        r0 = pl.multiple_of(b * PEER_NB, PEER_NB)
        xg = xn_ref[pl.ds(r0, PEER_NB), :]
        gg = g_ref[pl.ds(r0, PEER_NB), :]
        rows = [_peer_token(xg[tb:tb + 1, :], gg[tb:tb + 1, :],
                            functools.partial(lambda tb, kt, c: buf[grp, tb, kt, c], tb))
                for tb in range(PEER_NB)]
        out_ref[pl.ds(r0, PEER_NB), :] = jnp.concatenate(rows, axis=0)

    for b in range(PEER_LA):
        issue(b)

    def body(b, carry):
        wait(b)

        @pl.when(b + PEER_LA < nbatch)
        def _():
            issue(b + PEER_LA)

        compute(b)
        return carry

    lax.fori_loop(0, nbatch, body, 0)


def _order_after(after):
    return jnp.zeros((SUBLANES, LANES), F32) if after is None else after


def _peer(idx, xn, g, uv, ts, after=None):
    t, d = xn.shape
    tb = PEER_TB
    idx2 = idx.reshape(t // tb, tb * PEER_SEL)
    s0 = ts // tb
    return pl.pallas_call(
        functools.partial(_peer_kernel, s0),
        out_shape=jax.ShapeDtypeStruct((t, d), F32),
        grid=((t - ts) // tb,),
        in_specs=[pl.BlockSpec(memory_space=pl.ANY),
                  pl.BlockSpec((tb, d), lambda i: (i + s0, 0)),
                  pl.BlockSpec((tb, PEER_SEL), lambda i: (i + s0, 0)),
                  pl.BlockSpec(memory_space=pl.ANY),
                  pl.BlockSpec(memory_space=pl.ANY)],
        out_specs=pl.BlockSpec((tb, d), lambda i: (i + s0, 0)),
        scratch_shapes=[pltpu.SMEM((tb * PEER_SEL,), jnp.int32),
                        pltpu.VMEM((PEER_NG, PEER_NB, PEER_SEL // SUBLANES, d // LANES, SUBLANES, LANES),
                                   jnp.int32),
                        pltpu.SemaphoreType.DMA((1,)),
                        pltpu.SemaphoreType.DMA((PEER_NG,))],
        compiler_params=_cparams(("arbitrary",)),
        name="peer",
    )(idx2, xn, g, uv, _order_after(after))


def _ple_kernel(x1_ref, pe_ref, p_ref, nw_ref, wg_ref, wp_ref, nf_ref, y_ref):
    x2 = x1_ref[...] + pe_ref[...]
    gate = _sigmoid(_dot(_rms(x2, nw_ref[...]).astype(BF16), wg_ref[...]))
    x3 = x2 + _dot(p_ref[...].astype(BF16), wp_ref[...]) * gate
    y_ref[...] = _rms(x3, nf_ref[...])


def _ple(x1, pe, p, nw, wg, wp, nf, tm):
    t, d = x1.shape
    full = lambda shape: pl.BlockSpec(shape, lambda i: (0,) * len(shape))
    tok = lambda w: pl.BlockSpec((tm, w), lambda i: (i, 0))
    return pl.pallas_call(
        _ple_kernel,
        out_shape=jax.ShapeDtypeStruct((t, d), F32),
        grid=(t // tm,),
        in_specs=[tok(d), tok(d), tok(PLE_DIM), full(nw.shape), full(wg.shape), full(wp.shape),
                  full(nf.shape)],
        out_specs=tok(d),
        compiler_params=_cparams(("parallel",)),
        name="ple",
    )(x1, pe, p, nw, wg, wp, nf)


def _rope_tables(s):
    half = ROT_DIM // 2
    inv = ROPE_THETA ** (-jnp.arange(0, ROT_DIM, 2, dtype=F32) / ROT_DIM)
    ang = jnp.arange(s, dtype=F32)[:, None] * inv[None, :]
    cos, sin = jnp.cos(ang), jnp.sin(ang)
    ones = jnp.ones((s, ATT_QK_DIM - ROT_DIM), F32)
    zeros = jnp.zeros((s, ATT_QK_DIM - ROT_DIM), F32)
    zh = jnp.zeros((s, half), F32)
    c64 = jnp.concatenate([cos, cos, ones], axis=1)
    s1_64 = jnp.concatenate([zh, sin, zeros], axis=1)
    s2_64 = jnp.concatenate([-sin, zh, zeros], axis=1)
    rep = LANES // ATT_QK_DIM
    return jnp.tile(c64, (1, rep)), jnp.tile(s1_64, (1, rep)), jnp.tile(s2_64, (1, rep))


def _pack_uv(u, v):
    ub = lax.bitcast_convert_type(u.astype(BF16), jnp.uint16).astype(jnp.uint32)
    vb = lax.bitcast_convert_type(v.astype(BF16), jnp.uint16).astype(jnp.uint32)
    return lax.bitcast_convert_type((ub << 16) | vb, jnp.int32)


def _mix_route(x, w):
    b, s, d = x.shape
    t = b * s
    nc = s // M_CHUNK
    rc, rs1, rs2 = _rope_tables(s)
    aq, ak, av, mq, mk, mv, mo, gc, gr = _inproj(
        x, w["norm_mix"], w["wqk"], w["wav"], w["wmqk"], w["wmvo"], w["wg"], w["wgt"], w["gb"],
        w["gbt"], w["conv"], rc, rs1, rs2, tm=512)
    att = _attn(w["lq"], aq, ak, av, w["attn_gain"], tq=256)
    gc5 = gc.reshape(b, nc, M_CHUNK, M_HEADS, 4).transpose(0, 3, 1, 2, 4)
    gr5 = gr.reshape(b, M_HEADS, 4, nc, M_CHUNK).transpose(0, 1, 3, 2, 4)
    mem = _mlstm(mq, mk, mv, mo, gc5, gr5, w["mlstm_gain"])
    x1, xn, qp = _outproj(x.reshape(t, d), att.reshape(t, ATT_W), mem.reshape(t, M_W),
                          w["wo_a"], w["wo_m"], w["norm_ffn"], w["wq"], tm=512)
    idx, g = _route(qp, w["sk"], tm=256)
    return x1, xn, idx, g


def _peer_all(groups, uv2):
    uv3 = uv2.reshape(-1, D_MODEL // LANES, LANES)
    staged, outs = [], []
    prev = None
    for gi, (xn, idx, g) in enumerate(groups):
        t, d = xn.shape
        num = PEER_SC_SHARE_NUM - PEER_SC_SHARE_LAST_CUT * (gi == len(groups) - 1)
        ts = (t * num // PEER_SC_SHARE_DEN) // PEER_SC_QUANTUM * PEER_SC_QUANTUM
        assert 0 < ts < t and (t - ts) % PEER_TB == 0, (t, ts)
        staged.append(_stage_rows(uv2, idx[:ts]).reshape(ts, PEER_SEL, d))
        prev = _peer(idx, xn, g, uv3, ts, after=prev)
        outs.append(prev)
    for i, (xn, idx, g) in enumerate(groups):
        prev = _peer_staged(staged[i], xn, g, outs[i], after=prev)
        outs[i] = prev
    return outs


def kernel(x_prompt, x_sample, p_prompt, p_sample, norm_mix, w_in, gate_bias, conv_qk, lambda_qk,
           attn_gain, mlstm_gain, w_out, norm_ffn, peer_wq, peer_subkeys, peer_u, peer_v, norm_ple,
           ple_w_gate, ple_w_proj, norm_final):
    li = 0
    win = w_in[li]
    c0 = 0
    cols = {}
    for name, width in (("aq", ATT_W), ("ak", ATT_W), ("av", ATT_W), ("mq", M_W), ("mk", M_W),
                        ("mv", M_W), ("mo", M_W), ("gt", N_GATES)):
        cols[name] = win[:, c0:c0 + width]
        c0 += width
    perm = jnp.array([ty * M_HEADS + h for h in range(M_HEADS) for ty in range(4)], jnp.int32)
    wg = cols["gt"][:, perm].astype(BF16)
    gb = gate_bias[li][perm]
    w = {
        "norm_mix": norm_mix[li][None, :],
        "wqk": jnp.concatenate([cols["aq"], cols["ak"]], axis=1).astype(BF16),
        "wav": cols["av"].astype(BF16),
        "wmqk": jnp.concatenate([cols["mq"], cols["mk"]], axis=1).astype(BF16),
        "wmvo": jnp.concatenate([cols["mv"], cols["mo"]], axis=1).astype(BF16),
        "wg": wg,
        "wgt": wg.T,
        "gb": gb[None, :],
        "gbt": gb[:, None],
        "conv": conv_qk[li],
        "lq": lambda_qk[li],
        "attn_gain": attn_gain[li][None, :],
        "mlstm_gain": mlstm_gain[li][None, :],
        "wo_a": w_out[li][:ATT_W].astype(BF16),
        "wo_m": w_out[li][ATT_W:].astype(BF16),
        "norm_ffn": norm_ffn[li][None, :],
        "wq": peer_wq[li].astype(BF16),
        "sk": peer_subkeys[li].reshape(PEER_HEADS * 2, N_KEYS, PEER_HALF).astype(BF16),
        "uv": _pack_uv(peer_u[li], peer_v[li]),
        "norm_ple": norm_ple[li][None, :],
        "ple_wg": ple_w_gate[li].astype(BF16),
        "ple_wp": ple_w_proj[li].astype(BF16),
        "norm_final": norm_final[None, :],
    }
    parts = FIRST_GROUP_PARTS
    rows = x_prompt.shape[0] // parts
    xs = tuple(x_prompt[i * rows:(i + 1) * rows] for i in range(parts)) + (x_sample,)
    ps = tuple(p_prompt[li][i * rows:(i + 1) * rows] for i in range(parts)) + (p_sample[li],)
    mixed = [_mix_route(x, w) for x in xs]
    pes = _peer_all([(xn, idx, g) for _, xn, idx, g in mixed], w["uv"])
    ys = []
    for x, p, (x1, _, _, _), pe in zip(xs, ps, mixed, pes):
        y = _ple(x1, pe, p.reshape(-1, PLE_DIM), w["norm_ple"], w["ple_wg"], w["ple_wp"],
                 w["norm_final"], tm=512)
        ys.append(y.reshape(x.shape))
    return (jnp.concatenate(ys[:parts], axis=0), ys[parts])
```
